```python
import jax, jax.numpy as jnp
from jax import lax
import numpy as np

D_MODEL = 2048
BATCH = 16
SEQ = 256
DEPTH = 2
DEC_BATCH = 2
DEC_SEQ = 2048
PAST_LEN = 256

GRID_W = 64
ROPE_BASE = 10000.0
NORM_EPS = 1e-6
Q_BLOCK = 128
MLA_HEADS = 8
MLA_Q_LORA = 512
MLA_KV_LORA = 512
MLA_NOPE = 128
MLA_ROPE = 64
MLA_V = 128
MLA_QK = MLA_NOPE + MLA_ROPE
MLA_SCALE = MLA_QK ** -0.5
MLA_WIDTH = MLA_HEADS * MLA_V
GLA_HEADS = 4
GLA_DK = 128
GLA_DV = 256
GLA_GATE_RANK = 16
GLA_GATE_NORM = 16.0
GLA_CHUNK = 16
GLA_WIDTH = GLA_HEADS * GLA_DV
SWA_HEADS = 16
SWA_KV_HEADS = 4
SWA_GROUPS = SWA_HEADS // SWA_KV_HEADS
SWA_HEAD_DIM = 64
SWA_WINDOW = 128
SWA_SCALE = SWA_HEAD_DIM ** -0.5
SWA_WIDTH = SWA_HEADS * SWA_HEAD_DIM
D_FF = 5632
N_MOD = 9
IN_SPLITS = (MLA_Q_LORA, MLA_KV_LORA, MLA_ROPE,
             GLA_HEADS * GLA_DK, GLA_HEADS * GLA_DK, GLA_WIDTH, GLA_WIDTH, GLA_GATE_RANK, GLA_GATE_RANK,
             SWA_WIDTH, SWA_KV_HEADS * SWA_HEAD_DIM, SWA_KV_HEADS * SWA_HEAD_DIM,
             D_MODEL, D_MODEL, D_MODEL)
IN_COLS = sum(IN_SPLITS)

kernel_name = 'hybrid_mla_gla_swa_prefix_dit_step'


def rms_norm(x, g):
    xf = x.astype(jnp.float32)
    y = xf * lax.rsqrt(jnp.mean(xf * xf, axis=-1, keepdims=True) + NORM_EPS)
    return (y * g.astype(jnp.float32)).astype(x.dtype)


def adaln(cond, P):
    return jnp.split(jax.nn.silu(cond) @ P['w_ada'] + P['b_ada'], N_MOD, axis=-1)


def modulate(x, g, shift, scale):
    return rms_norm(x, g) * (1 + scale[..., None, :]) + shift[..., None, :]


def swiglu(h, w_gu, w_down):
    a, u = jnp.split(h @ w_gu, 2, axis=-1)
    return (jax.nn.silu(a) * u) @ w_down


def rope_rotate(x, pos):
    half = x.shape[-1] // 2
    inv_freq = ROPE_BASE ** (-jnp.arange(half, dtype=jnp.float32) / half)
    ang = pos.astype(jnp.float32)[:, None] * inv_freq[None, :]
    shape = (1, x.shape[1]) + (1,) * (x.ndim - 3) + (half,)
    cos = jnp.cos(ang).reshape(shape).astype(x.dtype)
    sin = jnp.sin(ang).reshape(shape).astype(x.dtype)
    x1, x2 = x[..., :half], x[..., half:]
    return jnp.concatenate([x1 * cos - x2 * sin, x1 * sin + x2 * cos], axis=-1)


def axial_rope(x):
    rows = x.shape[1] // GRID_W
    pos = jnp.arange(rows * GRID_W)
    r2 = x.shape[-1] // 2
    return jnp.concatenate([rope_rotate(x[..., :r2], pos // GRID_W),
                            rope_rotate(x[..., r2:], pos % GRID_W)], axis=-1)


def mla_rope(x):
    return jnp.concatenate([x[..., :MLA_NOPE], axial_rope(x[..., MLA_NOPE:])], axis=-1)


def block_softmax_attention(q, k, v, sink=None):
    B, Tq, Hk, G, d = q.shape
    nb = Tq // Q_BLOCK
    qb = jnp.moveaxis(q.reshape(B, nb, Q_BLOCK, Hk, G, d), 1, 0)

    def one_block(qblk):
        s = jnp.einsum('bqhgd,bkhd->bhgqk', qblk, k).astype(jnp.float32)
        if sink is not None:
            s_sink = jnp.broadcast_to(sink.astype(jnp.float32)[None, :, :, None, None], s.shape[:-1] + (1,))
            p = jax.nn.softmax(jnp.concatenate([s, s_sink], axis=-1), axis=-1)[..., :-1]
        else:
            p = jax.nn.softmax(s, axis=-1)
        return jnp.einsum('bhgqk,bkhv->bqhgv', p.astype(v.dtype), v)

    o = lax.map(one_block, qb)
    return jnp.moveaxis(o, 0, 1).reshape(B, Tq, Hk, G, v.shape[-1])


def banded_window_attention(q, k, v, k_ctx, v_ctx, sink):
    B, T, Hk, G, d = q.shape
    W = SWA_WINDOW
    nb = T // W
    L = k_ctx.shape[1]
    qb = q.reshape(B, nb, W, Hk, G, d)
    pad = ((0, 0), (W, W), (0, 0), (0, 0))
    kp = jnp.pad(k, pad).reshape(B, nb + 2, W, Hk, d)
    vp = jnp.pad(v, pad).reshape(B, nb + 2, W, Hk, d)
    kb = jnp.concatenate([kp[:, :-2], kp[:, 1:-1], kp[:, 2:]], axis=2)
    vb = jnp.concatenate([vp[:, :-2], vp[:, 1:-1], vp[:, 2:]], axis=2)
    qpos = jnp.arange(nb)[:, None] * W + jnp.arange(W)[None, :]
    kpos = jnp.arange(nb)[:, None] * W - W + jnp.arange(3 * W)[None, :]
    valid = ((kpos[:, None, :] >= 0) & (kpos[:, None, :] < T)
             & (jnp.abs(qpos[:, :, None] - kpos[:, None, :]) <= W))
    s_loc = jnp.einsum('bnqhgd,bnkhd->bnhgqk', qb, kb).astype(jnp.float32)
    s_loc = jnp.where(valid[None, :, None, None], s_loc, -jnp.inf)
    s_ctx = jnp.einsum('bnqhgd,bkhd->bnhgqk', qb, k_ctx).astype(jnp.float32)
    s_sink = jnp.broadcast_to(sink.astype(jnp.float32)[None, None, :, :, None, None], s_loc.shape[:-1] + (1,))
    p = jax.nn.softmax(jnp.concatenate([s_loc, s_ctx, s_sink], axis=-1), axis=-1)
    p_loc = p[..., :3 * W].astype(v.dtype)
    p_ctx = p[..., 3 * W:3 * W + L].astype(v.dtype)
    o = (jnp.einsum('bnhgqk,bnkhd->bnqhgd', p_loc, vb)
         + jnp.einsum('bnhgqk,bkhd->bnqhgd', p_ctx, v_ctx))
    return o.reshape(B, T, Hk, G, d)


def gla_chunked(q, k, v, log_a, s0):
    B, T, H, dk = q.shape
    C = GLA_CHUNK
    n = T // C

    def rs(a):
        return a.reshape(B, n, C, H, a.shape[-1])

    q, k, v, log_a = rs(q), rs(k), rs(v), rs(log_a)
    b = jnp.cumsum(log_a, axis=2)
    b_last = b[:, :, -1]
    causal = jnp.tril(jnp.ones((C, C), dtype=bool))
    diff = b[:, :, :, None] - b[:, :, None, :]
    decay = jnp.exp(jnp.where(causal[None, None, :, :, None, None], diff, -jnp.inf))
    A = jnp.einsum('bnthk,bnshk,bntshk->bnhts', q, k, decay)
    o_intra = jnp.einsum('bnhts,bnshv->bnthv', A, v)
    q_dec = q * jnp.exp(b)
    k_dec = k * jnp.exp(b_last[:, :, None] - b)
    a_chunk = jnp.exp(b_last)

    def step(S, xs):
        qd, kd, vv, ac = xs
        o = jnp.einsum('bthk,bhkv->bthv', qd, S)
        S = ac[..., None] * S + jnp.einsum('bthk,bthv->bhkv', kd, vv)
        return S, o

    xs = (jnp.moveaxis(q_dec, 1, 0), jnp.moveaxis(k_dec, 1, 0), jnp.moveaxis(v, 1, 0), jnp.moveaxis(a_chunk, 1, 0))
    s_fin, o_inter = lax.scan(step, s0, xs)
    o = o_intra + jnp.moveaxis(o_inter, 0, 1)
    return o.reshape(B, T, H, v.shape[-1]), s_fin


def gla_bidirectional(q, k, v, la_f, la_b, s0_f, s0_b):
    o_f, s_f = gla_chunked(q, k, v, la_f, s0_f)

    def fl(a):
        return jnp.flip(a, axis=1)

    o_b, s_b = gla_chunked(fl(q), fl(k), fl(v), fl(la_b), s0_b)
    return o_f + fl(o_b), s_f, s_b


def mla_query(cq, P):
    B, T, _ = cq.shape
    q = (rms_norm(cq, P['g_mla_q']) @ P['w_mla_uq']).reshape(B, T, MLA_HEADS, MLA_QK)
    return rms_norm(q, P['g_mla_qn'])


def mla_keys_values(c_kv, k_pe, P):
    B, L, _ = c_kv.shape
    kv = (c_kv @ P['w_mla_ukv']).reshape(B, L, MLA_HEADS, MLA_NOPE + MLA_V)
    k_nope, v = kv[..., :MLA_NOPE], kv[..., MLA_NOPE:]
    k = jnp.concatenate([k_nope, jnp.broadcast_to(k_pe[:, :, None, :], (B, L, MLA_HEADS, MLA_ROPE))], axis=-1)
    return rms_norm(k, P['g_mla_kn']), v


def swa_qkv(sq, sk, sv, P):
    B, T, _ = sq.shape
    q = rms_norm(sq.reshape(B, T, SWA_KV_HEADS, SWA_GROUPS, SWA_HEAD_DIM), P['g_swa_qn'])
    k = rms_norm(sk.reshape(B, T, SWA_KV_HEADS, SWA_HEAD_DIM), P['g_swa_kn'])
    v = sv.reshape(B, T, SWA_KV_HEADS, SWA_HEAD_DIM)
    return q, k, v


def token_mixer(h, P, ctx=None):
    B, T, _ = h.shape
    points = np.cumsum(IN_SPLITS)[:-1].tolist()
    (cq, ckv, kpe, gq, gk, gv, gout, ggf, ggb, sq, sk, sv,
     gate_mla, gate_gla, gate_swa) = jnp.split(h @ P['w_in'], points, axis=-1)
    sink = P['swa_sink'].reshape(SWA_KV_HEADS, SWA_GROUPS)
    q_m = mla_query(cq, P)
    c_kv = rms_norm(ckv, P['g_mla_kv'])
    k_m, v_m = mla_keys_values(c_kv, kpe, P)
    q_s, k_s, v_s = swa_qkv(sq, sk, sv, P)
    q_g = gq.reshape(B, T, GLA_HEADS, GLA_DK).astype(jnp.float32) * GLA_DK ** -0.5
    k_g = gk.reshape(B, T, GLA_HEADS, GLA_DK).astype(jnp.float32)
    v_g = gv.reshape(B, T, GLA_HEADS, GLA_DV).astype(jnp.float32)
    la_f = (jax.nn.log_sigmoid((ggf @ P['w_gla_gf'] + P['b_gla_gf']).astype(jnp.float32))
            / GLA_GATE_NORM).reshape(B, T, GLA_HEADS, GLA_DK)
    la_b = (jax.nn.log_sigmoid((ggb @ P['w_gla_gb'] + P['b_gla_gb']).astype(jnp.float32))
            / GLA_GATE_NORM).reshape(B, T, GLA_HEADS, GLA_DK)
    if ctx is None:
        o_mla = block_softmax_attention(q_m[:, :, :, None] * MLA_SCALE, k_m, v_m)
        o_swa = block_softmax_attention(q_s * SWA_SCALE, k_s, v_s, sink)
        s0 = jnp.zeros((B, 2, GLA_HEADS, GLA_DK, GLA_DV), jnp.float32)
    else:
        ctx_ckv, ctx_kpe, ctx_k, ctx_v, s0 = ctx
        k_mc, v_mc = mla_keys_values(ctx_ckv, ctx_kpe, P)
        o_mla = block_softmax_attention(mla_rope(q_m)[:, :, :, None] * MLA_SCALE,
                                        jnp.concatenate([mla_rope(k_m), k_mc], axis=1),
                                        jnp.concatenate([v_m, v_mc], axis=1))
        o_swa = banded_window_attention(axial_rope(q_s) * SWA_SCALE, axial_rope(k_s), v_s, ctx_k, ctx_v, sink)
    s0 = s0.astype(jnp.float32)
    o_g, s_f, s_b = gla_bidirectional(q_g, k_g, v_g, la_f, la_b, s0[:, 0], s0[:, 1])
    o_g = rms_norm(o_g.astype(h.dtype), P['g_gla_out']).reshape(B, T, GLA_WIDTH) * jax.nn.silu(gout)
    merged = (jax.nn.sigmoid(gate_mla) * (o_mla.reshape(B, T, MLA_WIDTH) @ P['w_br_mla'])
              + jax.nn.sigmoid(gate_gla) * (o_g @ P['w_br_gla'])
              + jax.nn.sigmoid(gate_swa) * (o_swa.reshape(B, T, SWA_WIDTH) @ P['w_br_swa']))
    out = merged @ P['w_out']
    if ctx is None:
        return out, (c_kv, kpe, k_s, v_s, jnp.stack([s_f, s_b], axis=1))
    return out, None


def trunk_layer(x, mod, P, ctx=None):
    sh1, sc1, gt1, sh2, sc2, gt2, sh3, sc3, gt3 = mod
    x = x + 0.5 * gt1[..., None, :] * swiglu(modulate(x, P['g_norm1'], sh1, sc1), P['w_ff1_gu'], P['w_ff1_down'])
    m, new_ctx = token_mixer(modulate(x, P['g_norm2'], sh2, sc2), P, ctx)
    x = x + gt2[..., None, :] * m
    x = x + 0.5 * gt3[..., None, :] * swiglu(modulate(x, P['g_norm3'], sh3, sc3), P['w_ff2_gu'], P['w_ff2_down'])
    return x, new_ctx


def setup_inputs(seed: int = 0) -> dict:
    key = jax.random.key(seed)
    ks = jax.random.split(key, 64)
    counter = iter(range(64))

    def nrm(shape, scale=1.0):
        return jax.random.normal(ks[next(counter)], shape, jnp.float32) * scale

    def gain(shape):
        return 1.0 + 0.02 * nrm(shape)

    return {
        'x_prompt': nrm((BATCH, SEQ, D_MODEL)),
        'x_sample': nrm((DEC_BATCH, DEC_SEQ, D_MODEL)),
        'cache_mla_ckv': nrm((DEC_BATCH, DEPTH, PAST_LEN, MLA_KV_LORA)),
        'cache_mla_kpe': nrm((DEC_BATCH, DEPTH, PAST_LEN, MLA_ROPE)),
        'cache_swa_k': nrm((DEC_BATCH, DEPTH, PAST_LEN, SWA_KV_HEADS, SWA_HEAD_DIM)),
        'cache_swa_v': nrm((DEC_BATCH, DEPTH, PAST_LEN, SWA_KV_HEADS, SWA_HEAD_DIM)),
        'state_gla': nrm((DEC_BATCH, DEPTH, 2, GLA_HEADS, GLA_DK, GLA_DV)),
        'c': nrm((DEC_BATCH, D_MODEL)),
        'c_ctx': nrm((D_MODEL,)),
        'w_ada': nrm((DEPTH, D_MODEL, N_MOD * D_MODEL), 0.5 * D_MODEL ** -0.5),
        'b_ada': nrm((DEPTH, N_MOD * D_MODEL), 0.02),
        'g_norm1': gain((DEPTH, D_MODEL)),
        'g_norm2': gain((DEPTH, D_MODEL)),
        'g_norm3': gain((DEPTH, D_MODEL)),
        'w_ff1_gu': nrm((DEPTH, D_MODEL, 2 * D_FF), D_MODEL ** -0.5),
        'w_ff1_down': nrm((DEPTH, D_FF, D_MODEL), D_FF ** -0.5),
        'w_ff2_gu': nrm((DEPTH, D_MODEL, 2 * D_FF), D_MODEL ** -0.5),
        'w_ff2_down': nrm((DEPTH, D_FF, D_MODEL), D_FF ** -0.5),
        'w_in': nrm((DEPTH, D_MODEL, IN_COLS), D_MODEL ** -0.5),
        'g_mla_q': gain((DEPTH, MLA_Q_LORA)),
        'w_mla_uq': nrm((DEPTH, MLA_Q_LORA, MLA_HEADS * MLA_QK), MLA_Q_LORA ** -0.5),
        'g_mla_kv': gain((DEPTH, MLA_KV_LORA)),
        'w_mla_ukv': nrm((DEPTH, MLA_KV_LORA, MLA_HEADS * (MLA_NOPE + MLA_V)), MLA_KV_LORA ** -0.5),
        'g_mla_qn': gain((DEPTH, MLA_QK)),
        'g_mla_kn': gain((DEPTH, MLA_QK)),
        'w_gla_gf': nrm((DEPTH, GLA_GATE_RANK, GLA_HEADS * GLA_DK), GLA_GATE_RANK ** -0.5),
        'b_gla_gf': nrm((DEPTH, GLA_HEADS * GLA_DK), 0.1),
        'w_gla_gb': nrm((DEPTH, GLA_GATE_RANK, GLA_HEADS * GLA_DK), GLA_GATE_RANK ** -0.5),
        'b_gla_gb': nrm((DEPTH, GLA_HEADS * GLA_DK), 0.1),
        'g_gla_out': gain((DEPTH, GLA_DV)),
        'g_swa_qn': gain((DEPTH, SWA_HEAD_DIM)),
        'g_swa_kn': gain((DEPTH, SWA_HEAD_DIM)),
        'swa_sink': nrm((DEPTH, SWA_HEADS)),
        'w_br_mla': nrm((DEPTH, MLA_WIDTH, D_MODEL), MLA_WIDTH ** -0.5),
        'w_br_gla': nrm((DEPTH, GLA_WIDTH, D_MODEL), GLA_WIDTH ** -0.5),
        'w_br_swa': nrm((DEPTH, SWA_WIDTH, D_MODEL), SWA_WIDTH ** -0.5),
        'w_out': nrm((DEPTH, D_MODEL, D_MODEL), D_MODEL ** -0.5),
    }


def reference(x_prompt, x_sample, cache_mla_ckv, cache_mla_kpe, cache_swa_k, cache_swa_v, state_gla,
              c, c_ctx, w_ada, b_ada, g_norm1, g_norm2, g_norm3,
              w_ff1_gu, w_ff1_down, w_ff2_gu, w_ff2_down, w_in,
              g_mla_q, w_mla_uq, g_mla_kv, w_mla_ukv, g_mla_qn, g_mla_kn,
              w_gla_gf, b_gla_gf, w_gla_gb, b_gla_gb, g_gla_out,
              g_swa_qn, g_swa_kn, swa_sink, w_br_mla, w_br_gla, w_br_swa, w_out):
    stacked = {
        'w_ada': w_ada, 'b_ada': b_ada, 'g_norm1': g_norm1, 'g_norm2': g_norm2, 'g_norm3': g_norm3,
        'w_ff1_gu': w_ff1_gu, 'w_ff1_down': w_ff1_down, 'w_ff2_gu': w_ff2_gu, 'w_ff2_down': w_ff2_down,
        'w_in': w_in, 'g_mla_q': g_mla_q, 'w_mla_uq': w_mla_uq, 'g_mla_kv': g_mla_kv,
        'w_mla_ukv': w_mla_ukv, 'g_mla_qn': g_mla_qn, 'g_mla_kn': g_mla_kn,
        'w_gla_gf': w_gla_gf, 'b_gla_gf': b_gla_gf, 'w_gla_gb': w_gla_gb, 'b_gla_gb': b_gla_gb,
        'g_gla_out': g_gla_out, 'g_swa_qn': g_swa_qn, 'g_swa_kn': g_swa_kn, 'swa_sink': swa_sink,
        'w_br_mla': w_br_mla, 'w_br_gla': w_br_gla, 'w_br_swa': w_br_swa, 'w_out': w_out,
    }
    new_ckv, new_kpe, new_sk, new_sv, new_sg = [], [], [], [], []
    y_p, y_s = x_prompt, x_sample
    for l in range(DEPTH):
        P = {name: arr[l] for name, arr in stacked.items()}
        y_p, (ckv, kpe, sk, sv, sg) = trunk_layer(y_p, adaln(c_ctx, P), P)
        new_ckv.append(ckv)
        new_kpe.append(kpe)
        new_sk.append(sk)
        new_sv.append(sv)
        new_sg.append(sg)
        ctx = (cache_mla_ckv[:, l], cache_mla_kpe[:, l], cache_swa_k[:, l], cache_swa_v[:, l], state_gla[:, l])
        y_s, _ = trunk_layer(y_s, adaln(c, P), P, ctx)
    return (y_p, y_s, jnp.stack(new_ckv, axis=1), jnp.stack(new_kpe, axis=1),
            jnp.stack(new_sk, axis=1), jnp.stack(new_sv, axis=1), jnp.stack(new_sg, axis=1))
```

```python
import functools

import numpy as np
import jax
import jax.numpy as jnp
from jax import lax
from jax.experimental import pallas as pl
from jax.experimental.pallas import tpu as pltpu

F32 = jnp.float32
BF16 = jnp.bfloat16

D_MODEL = 2048
DEPTH = 2
GRID_W = 64
ROPE_BASE = 10000.0
NORM_EPS = 1e-6
MLA_HEADS = 8
MLA_Q_LORA = 512
MLA_KV_LORA = 512
MLA_NOPE = 128
MLA_ROPE = 64
MLA_V = 128
MLA_QK = MLA_NOPE + MLA_ROPE
MLA_SCALE = MLA_QK ** -0.5
MLA_WIDTH = MLA_HEADS * MLA_V
MLA_HEAD_PAD = 256
GLA_HEADS = 4
GLA_DK = 128
GLA_DV = 256
GLA_GATE_RANK = 16
GLA_GATE_NORM = 16.0
GLA_WIDTH = GLA_HEADS * GLA_DV
GLA_CHUNK = 128
SWA_HEADS = 16
SWA_KV_HEADS = 4
SWA_HEAD_DIM = 64
SWA_WINDOW = 128
SWA_SCALE = SWA_HEAD_DIM ** -0.5
SWA_WIDTH = SWA_HEADS * SWA_HEAD_DIM
D_FF = 5632
N_MOD = 9
IN_SPLITS = (MLA_Q_LORA, MLA_KV_LORA, MLA_ROPE,
             GLA_HEADS * GLA_DK, GLA_HEADS * GLA_DK, GLA_WIDTH, GLA_WIDTH, GLA_GATE_RANK, GLA_GATE_RANK,
             SWA_WIDTH, SWA_KV_HEADS * SWA_HEAD_DIM, SWA_KV_HEADS * SWA_HEAD_DIM,
             D_MODEL, D_MODEL, D_MODEL)

V7X_LANES = 128
V7X_SUBLANES = 8
V7X_VMEM_BYTES = 64 * 1024 * 1024
MIB = 1024 * 1024

COL_GATE_MLA = 0
COL_GATE_GLA = 2048
COL_GATE_SWA = 4096
COL_CQ = 6144
COL_CKV = 6656
COL_GQ = 7168
COL_GK = 7680
COL_GV = 8192
COL_GOUT = 9216
COL_SQ = 10240
COL_SK = 11264
COL_SV = 11520
COL_KPE = 11776
COL_GG = 11904
PROJ_COLS_PROMPT = 12288
COL_SQ_PERM = 12288
COL_SK_PERM = 13312
COL_KPE_PERM = 13568
PROJ_COLS_SAMPLE = 14336
PROJ_TN = 1024


def _vmem_limit(nbytes):
    return int(min(nbytes + 12 * MIB, V7X_VMEM_BYTES - 6 * MIB))


def _rope_partner():
    i = np.arange(64)
    within = i % 32
    perm = (i // 32) * 32 + (within + 16) % 32
    sign = np.where(within < 16, -1.0, 1.0).astype(np.float32)
    return perm, sign


ADA_TN = 1024


def _adaln_kernel(cond_ref, w_ref, b_ref, o_ref):
    c = cond_ref[...]
    s = (c * jax.nn.sigmoid(c)).astype(BF16)
    o_ref[...] = jnp.dot(s, w_ref[...].astype(BF16), preferred_element_type=F32) + b_ref[...]


def _adaln(cond8, w_ada, b_ada):
    halves = D_MODEL // ADA_TN
    nj = N_MOD * halves
    return pl.pallas_call(
        _adaln_kernel,
        grid=(DEPTH, nj),
        in_specs=[
            pl.BlockSpec((8, D_MODEL), lambda l, j: (0, 0)),
            pl.BlockSpec((None, D_MODEL, ADA_TN), lambda l, j: (l, 0, j)),
            pl.BlockSpec((None, 1, ADA_TN), lambda l, j: (l, 0, j)),
        ],
        out_specs=pl.BlockSpec((None, None, 8, ADA_TN), lambda l, j: (l, j // halves, 0, j % halves)),
        out_shape=jax.ShapeDtypeStruct((DEPTH, N_MOD, 8, D_MODEL), F32),
        compiler_params=pltpu.CompilerParams(
            dimension_semantics=("arbitrary", "arbitrary"),
            vmem_limit_bytes=_vmem_limit(2 * D_MODEL * ADA_TN * 4)),
        name="adaln",
    )(cond8, w_ada, b_ada.reshape(DEPTH, 1, N_MOD * D_MODEL))


NORM_ROWS = 32


def _modulated_norm_to(h_ref, x_ref, g, shift, scale):
    one_plus = 1.0 + scale

    def body(c, carry):
        r0 = pl.multiple_of(c * NORM_ROWS, NORM_ROWS)
        x = x_ref[pl.ds(r0, NORM_ROWS), :]
        ms = jnp.mean(x * x, axis=-1, keepdims=True)
        y = (x * lax.rsqrt(ms + NORM_EPS)) * g
        h_ref[pl.ds(r0, NORM_ROWS), :] = (y * one_plus + shift).astype(BF16)
        return carry

    lax.fori_loop(0, x_ref.shape[0] // NORM_ROWS, body, 0)


def _cond_row(i, tm, base_row, rows_per_cond):
    return base_row + (i * tm) // rows_per_cond


FFN_TM = 1024
FFN_TF = 256
FFN_TN = 512


def _ffn_kernel(x_ref, shift_ref, scale_ref, gate_ref, g_ref, wg_ref, wu_ref, wd_ref, o_ref, h_ref,
                *, layer, base_row, rows_per_cond):
    i = pl.program_id(0)
    j = pl.program_id(1)
    row = _cond_row(i, x_ref.shape[0], base_row, rows_per_cond)

    @pl.when(j == 0)
    def _():
        _modulated_norm_to(h_ref, x_ref, g_ref[pl.ds(layer, 1), :],
                           shift_ref[pl.ds(row, 1), :], scale_ref[pl.ds(row, 1), :])
        o_ref[...] = jnp.zeros_like(o_ref)

    h = h_ref[...]
    a = jnp.dot(h, wg_ref[...].astype(BF16), preferred_element_type=F32)
    u = jnp.dot(h, wu_ref[...].astype(BF16), preferred_element_type=F32)
    act = (a * jax.nn.sigmoid(a) * u).astype(BF16)
    for n in range(0, D_MODEL, FFN_TN):
        o_ref[:, n:n + FFN_TN] += jnp.dot(act, wd_ref[:, n:n + FFN_TN].astype(BF16), preferred_element_type=F32)

    @pl.when(j == pl.num_programs(1) - 1)
    def _():
        o_ref[...] = x_ref[...] + (0.5 * gate_ref[pl.ds(row, 1), :]) * o_ref[...]


def _ffn(x, mods, layer, first_mod, g_norm, w_gu, w_down, base_row, rows_per_cond):
    m = x.shape[0]
    nf = D_FF // FFN_TF
    mod_spec = lambda k: pl.BlockSpec((None, None, 8, D_MODEL), lambda i, j: (layer, first_mod + k, 0, 0))
    est = (2 * FFN_TM * D_MODEL * 4 * 2 + FFN_TM * D_MODEL * 2
           + 2 * 3 * D_MODEL * FFN_TF * w_gu.dtype.itemsize)
    return pl.pallas_call(
        functools.partial(_ffn_kernel, layer=layer, base_row=base_row, rows_per_cond=rows_per_cond),
        grid=(m // FFN_TM, nf),
        in_specs=[
            pl.BlockSpec((FFN_TM, D_MODEL), lambda i, j: (i, 0)),
            mod_spec(0), mod_spec(1), mod_spec(2),
            pl.BlockSpec((DEPTH, D_MODEL), lambda i, j: (0, 0)),
            pl.BlockSpec((None, D_MODEL, FFN_TF), lambda i, j: (layer, 0, j)),
            pl.BlockSpec((None, D_MODEL, FFN_TF), lambda i, j: (layer, 0, j + nf)),
            pl.BlockSpec((None, FFN_TF, D_MODEL), lambda i, j: (layer, j, 0)),
        ],
        out_specs=pl.BlockSpec((FFN_TM, D_MODEL), lambda i, j: (i, 0)),
        out_shape=jax.ShapeDtypeStruct((m, D_MODEL), F32),
        scratch_shapes=[pltpu.VMEM((FFN_TM, D_MODEL), BF16)],
        compiler_params=pltpu.CompilerParams(
            dimension_semantics=("arbitrary", "arbitrary"), vmem_limit_bytes=_vmem_limit(est)),
        name="ffn",
    )(x, mods, mods, mods, g_norm, w_gu, w_gu, w_down)


PROJ_TM = 1024


def _pack_w_in(w_in_l):
    parts = jnp.split(w_in_l, np.cumsum(IN_SPLITS)[:-1].tolist(), axis=1)
    cq, ckv, kpe, gq, gk, gv, gout, ggf, ggb, sq, sk, sv, gate_mla, gate_gla, gate_swa = parts
    perm, _ = _rope_partner()
    zeros = lambda n: jnp.zeros((D_MODEL, n), w_in_l.dtype)
    head_perm = lambda w, heads: w.reshape(D_MODEL, heads, SWA_HEAD_DIM)[:, :, perm].reshape(D_MODEL, -1)
    cols = [gate_mla, gate_gla, gate_swa, cq, ckv, gq, gk, gv, gout, sq, sk, sv,
            kpe, zeros(64), ggf, ggb, zeros(96), zeros(PROJ_COLS_PROMPT - COL_GG - 128),
            head_perm(sq, SWA_HEADS), head_perm(sk, SWA_KV_HEADS), kpe[:, perm], zeros(64),
            zeros(PROJ_COLS_SAMPLE - COL_KPE_PERM - 128)]
    return jnp.concatenate(cols, axis=1).astype(BF16)


def _proj_kernel(x_ref, shift_ref, scale_ref, g_ref, w_ref, o_ref, h_ref, *, layer, base_row, rows_per_cond):
    i = pl.program_id(0)
    j = pl.program_id(1)
    row = _cond_row(i, x_ref.shape[0], base_row, rows_per_cond)

    @pl.when(j == 0)
    def _():
        _modulated_norm_to(h_ref, x_ref, g_ref[pl.ds(layer, 1), :],
                           shift_ref[pl.ds(row, 1), :], scale_ref[pl.ds(row, 1), :])

    o_ref[...] = jnp.dot(h_ref[...], w_ref[...], preferred_element_type=F32)


def _in_proj(x, mods, layer, g_norm2, w_packed, ncols, base_row, rows_per_cond):
    m = x.shape[0]
    mod_spec = lambda k: pl.BlockSpec((None, None, 8, D_MODEL), lambda i, j: (layer, 3 + k, 0, 0))
    est = 2 * PROJ_TM * D_MODEL * 4 + PROJ_TM * D_MODEL * 2 + 2 * D_MODEL * PROJ_TN * 2 + 2 * PROJ_TM * PROJ_TN * 4
    return pl.pallas_call(
        functools.partial(_proj_kernel, layer=layer, base_row=base_row, rows_per_cond=rows_per_cond),
        grid=(m // PROJ_TM, ncols // PROJ_TN),
        in_specs=[
            pl.BlockSpec((PROJ_TM, D_MODEL), lambda i, j: (i, 0)),
            mod_spec(0), mod_spec(1),
            pl.BlockSpec((DEPTH, D_MODEL), lambda i, j: (0, 0)),
            pl.BlockSpec((D_MODEL, PROJ_TN), lambda i, j: (0, j)),
        ],
        out_specs=pl.BlockSpec((PROJ_TM, PROJ_TN), lambda i, j: (i, j)),
        out_shape=jax.ShapeDtypeStruct((m, ncols), F32),
        scratch_shapes=[pltpu.VMEM((PROJ_TM, D_MODEL), BF16)],
        compiler_params=pltpu.CompilerParams(
            dimension_semantics=("arbitrary", "arbitrary"), vmem_limit_bytes=_vmem_limit(est)),
        name="in_proj",
    )(x, mods, mods, g_norm2, w_packed)


def _rope_tables(t):
    pos = jnp.arange(t)
    inv_freq = ROPE_BASE ** (-jnp.arange(16, dtype=F32) / 16)

    def cs(p):
        ang = p.astype(F32)[:, None] * inv_freq[None, :]
        return jnp.concatenate([jnp.cos(ang)] * 2, axis=1), jnp.concatenate([jnp.sin(ang)] * 2, axis=1)

    cr, sr = cs(pos // GRID_W)
    cc, sc = cs(pos % GRID_W)
    _, sign = _rope_partner()
    return jnp.concatenate([cr, cc], axis=1), jnp.concatenate([sr, sc], axis=1) * sign[None, :]


MLA_TR = 256
MLA_TQ = 256
NT_DIMS = (((1,), (1,)), ((), ()))


def _pack_mla(w_uq_l, g_qn_l, g_kn_l):
    perm, _ = _rope_partner()
    w = w_uq_l.reshape(MLA_Q_LORA, MLA_HEADS, MLA_QK)
    nope, rope_w = w[:, :, :MLA_NOPE], w[:, :, MLA_NOPE:]
    z64 = jnp.zeros((MLA_Q_LORA, MLA_HEADS, 64), w.dtype)
    z128 = jnp.zeros((MLA_Q_LORA, MLA_HEADS, 128), w.dtype)
    wq = jnp.concatenate([nope, rope_w, z64], axis=-1).reshape(MLA_Q_LORA, -1).astype(BF16)
    wqp = jnp.concatenate([z128, rope_w[:, :, perm], z64], axis=-1).reshape(MLA_Q_LORA, -1).astype(BF16)
    v64 = jnp.zeros((64,), F32)
    v128 = jnp.zeros((128,), F32)
    gq_full = jnp.concatenate([g_qn_l, v64])[None]
    gq_perm = jnp.concatenate([v128, g_qn_l[MLA_NOPE:][perm], v64])[None]
    gk_n = g_kn_l[:MLA_NOPE][None]
    gk_r = jnp.concatenate([g_kn_l[MLA_NOPE:], v64])[None]
    gk_rp = jnp.concatenate([g_kn_l[MLA_NOPE:][perm], v64])[None]
    return wq, wqp, gq_full, gq_perm, gk_n, gk_r, gk_rp


def _mla_kv_kernel(*refs, layer, normalize, rope, emit_ckv):
    it = iter(refs)
    ckv_ref, kpe_ref = next(it), next(it)
    if rope:
        kpp_ref, c_ref, s_ref = next(it), next(it), next(it)
    w_ref, gkv_ref, gn_ref, gr_ref, grp_ref = next(it), next(it), next(it), next(it), next(it)
    k_ref, v_ref = next(it), next(it)
    ckv = ckv_ref[...]
    if normalize:
        ckv = (ckv * lax.rsqrt(jnp.mean(ckv * ckv, axis=-1, keepdims=True) + NORM_EPS)) * gkv_ref[pl.ds(layer, 1), :]
    if emit_ckv:
        next(it)[...] = ckv
    kv = jnp.dot(ckv.astype(BF16), w_ref[...].astype(BF16), preferred_element_type=F32)
    kpe = kpe_ref[...]
    ss_pe = jnp.sum(kpe * kpe, axis=-1, keepdims=True)
    kr = kpe * gr_ref[...]
    if rope:
        kr = kr * c_ref[...] + (kpp_ref[...] * grp_ref[...]) * s_ref[...]
    for h in range(MLA_HEADS):
        kn = kv[:, 256 * h:256 * h + 128]
        r = lax.rsqrt((jnp.sum(kn * kn, axis=-1, keepdims=True) + ss_pe) * (1.0 / MLA_QK) + NORM_EPS)
        k_ref[:, 256 * h:256 * h + 128] = ((kn * r) * gn_ref[...]).astype(BF16)
        k_ref[:, 256 * h + 128:256 * h + 256] = (kr * r).astype(BF16)
        v_ref[:, 128 * h:128 * h + 128] = kv[:, 256 * h + 128:256 * h + 256].astype(BF16)


def _mla_kv(ckv_src, ckv_spec, kpe_src, kpe_spec, rope_ins, w_ukv, g_kv, gk_n, gk_r, gk_rp,
            batch, rows, layer, normalize, emit_ckv):
    rope = rope_ins is not None
    nt = rows // MLA_TR
    ins = [ckv_src, kpe_src]
    specs = [ckv_spec, kpe_spec]
    if rope:
        kpp_src, kpp_spec, ctab, stab = rope_ins
        ins += [kpp_src, ctab, stab]
        specs += [kpp_spec, pl.BlockSpec((MLA_TR, 128), lambda b, t: (t, 0)), pl.BlockSpec((MLA_TR, 128), lambda b, t: (t, 0))]
    ins += [w_ukv, g_kv, gk_n, gk_r, gk_rp]
    specs += [pl.BlockSpec((None, MLA_KV_LORA, 2048), lambda b, t: (layer, 0, 0)),
              pl.BlockSpec((DEPTH, MLA_KV_LORA), lambda b, t: (0, 0))] + [pl.BlockSpec((1, 128), lambda b, t: (0, 0))] * 3
    out_shape = [jax.ShapeDtypeStruct((batch, rows, MLA_HEADS * MLA_HEAD_PAD), BF16),
                 jax.ShapeDtypeStruct((batch, rows, MLA_WIDTH), BF16)]
    out_specs = [pl.BlockSpec((None, MLA_TR, MLA_HEADS * MLA_HEAD_PAD), lambda b, t: (b, t, 0)),
                 pl.BlockSpec((None, MLA_TR, MLA_WIDTH), lambda b, t: (b, t, 0))]
    if emit_ckv:
        out_shape.append(jax.ShapeDtypeStruct((batch, rows, MLA_KV_LORA), F32))
        out_specs.append(pl.BlockSpec((None, MLA_TR, MLA_KV_LORA), lambda b, t: (b, t, 0)))
    est = 2 * MLA_KV_LORA * 2048 * 4 + 4 * MLA_TR * 2048 * 4
    return pl.pallas_call(
        functools.partial(_mla_kv_kernel, layer=layer, normalize=normalize, rope=rope, emit_ckv=emit_ckv),
        grid=(batch, nt), in_specs=specs, out_specs=out_specs, out_shape=out_shape,
        compiler_params=pltpu.CompilerParams(
            dimension_semantics=("arbitrary", "arbitrary"), vmem_limit_bytes=_vmem_limit(est)),
        name="mla_kv",
    )(*ins)


def _mla_attn_kernel(*refs, layer, rope, has_ctx):
    it = iter(refs)
    cq_ref, wq_ref, gq_ref, gfull_ref = next(it), next(it), next(it), next(it)
    if rope:
        wqp_ref, gperm_ref, c_ref, s_ref = next(it), next(it), next(it), next(it)
    k_ref, v_ref = next(it), next(it)
    if has_ctx:
        kc_ref, vc_ref = next(it), next(it)
    o_ref = next(it)
    cq = cq_ref[...]
    ql = ((cq * lax.rsqrt(jnp.mean(cq * cq, axis=-1, keepdims=True) + NORM_EPS)) * gq_ref[pl.ds(layer, 1), :]).astype(BF16)
    q_raw = jnp.dot(ql, wq_ref[...], preferred_element_type=F32)
    if rope:
        q_perm = jnp.dot(ql, wqp_ref[...], preferred_element_type=F32)
    for h in range(MLA_HEADS):
        sl = slice(256 * h, 256 * h + 256)
        q = q_raw[:, sl]
        r = lax.rsqrt(jnp.sum(q * q, axis=-1, keepdims=True) * (1.0 / MLA_QK) + NORM_EPS)
        qh = q * gfull_ref[...]
        if rope:
            qh = qh * c_ref[...] + (q_perm[:, sl] * gperm_ref[...]) * s_ref[...]
        qh = ((qh * r) * MLA_SCALE).astype(BF16)
        s = lax.dot_general(qh, k_ref[:, sl], NT_DIMS, preferred_element_type=F32)
        m = jnp.max(s, axis=-1, keepdims=True)
        if has_ctx:
            sc = lax.dot_general(qh, kc_ref[:, sl], NT_DIMS, preferred_element_type=F32)
            m = jnp.maximum(m, jnp.max(sc, axis=-1, keepdims=True))
        e = jnp.exp(s - m)
        den = jnp.sum(e, axis=-1, keepdims=True)
        if has_ctx:
            ec = jnp.exp(sc - m)
            den = den + jnp.sum(ec, axis=-1, keepdims=True)
        inv = 1.0 / den
        vs = slice(128 * h, 128 * h + 128)
        o = jnp.dot((e * inv).astype(BF16), v_ref[:, vs], preferred_element_type=F32)
        if has_ctx:
            o = o + jnp.dot((ec * inv).astype(BF16), vc_ref[:, vs], preferred_element_type=F32)
        o_ref[:, vs] = o.astype(BF16)


def _mla_attn(proj, wq, g_q, gq_full, rope_ins, k, v, ctx_kv, batch, t, layer):
    rope = rope_ins is not None
    has_ctx = ctx_kv is not None
    nq = t // MLA_TQ
    const = lambda shape: pl.BlockSpec(shape, lambda b, i: (0,) * len(shape))
    ins = [proj, wq, g_q, gq_full]
    specs = [pl.BlockSpec((MLA_TQ, MLA_Q_LORA), lambda b, i: (b * nq + i, COL_CQ // MLA_Q_LORA)),
             const((MLA_Q_LORA, 2048)), const((DEPTH, MLA_Q_LORA)), const((1, 256))]
    if rope:
        wqp, gq_perm, ctab, stab = rope_ins
        ins += [wqp, gq_perm, ctab, stab]
        specs += [const((MLA_Q_LORA, 2048)), const((1, 256)),
                  pl.BlockSpec((MLA_TQ, 256), lambda b, i: (i, 0)), pl.BlockSpec((MLA_TQ, 256), lambda b, i: (i, 0))]
    ins += [k, v]
    specs += [pl.BlockSpec((None, t, 2048), lambda b, i: (b, 0, 0)), pl.BlockSpec((None, t, MLA_WIDTH), lambda b, i: (b, 0, 0))]
    est = 2 * (t * 2048 * 2 + t * MLA_WIDTH * 2) + 6 * MLA_TQ * t * 4
    if has_ctx:
        kc, vc = ctx_kv
        lc = kc.shape[1]
        ins += [kc, vc]
        specs += [pl.BlockSpec((None, lc, 2048), lambda b, i: (b, 0, 0)), pl.BlockSpec((None, lc, MLA_WIDTH), lambda b, i: (b, 0, 0))]
        est += 2 * lc * 3072 * 2
    return pl.pallas_call(
        functools.partial(_mla_attn_kernel, layer=layer, rope=rope, has_ctx=has_ctx),
        grid=(batch, nq), in_specs=specs,
        out_specs=pl.BlockSpec((MLA_TQ, MLA_WIDTH), lambda b, i: (b * nq + i, 0)),
        out_shape=jax.ShapeDtypeStruct((batch * t, MLA_WIDTH), BF16),
        compiler_params=pltpu.CompilerParams(
            dimension_semantics=("arbitrary", "arbitrary"), vmem_limit_bytes=_vmem_limit(est)),
        name="mla_attn",
    )(*ins)


def _mla_branch(proj, batch, t, layer, mla_w, w_ukv, g_q, g_kv, rope, ctx):
    wq, wqp, gq_full, gq_perm, gk_n, gk_r, gk_rp = mla_w
    nt = t // MLA_TR
    ckv_spec = pl.BlockSpec((MLA_TR, MLA_KV_LORA), lambda b, i: (b * nt + i, COL_CKV // MLA_KV_LORA))
    kpe_spec = pl.BlockSpec((MLA_TR, 128), lambda b, i: (b * nt + i, COL_KPE // 128))
    if rope is None:
        k, v, ckv_n = _mla_kv(proj, ckv_spec, proj, kpe_spec, None, w_ukv, g_kv, gk_n, gk_r, gk_rp,
                              batch, t, layer, normalize=True, emit_ckv=True)
        return _mla_attn(proj, wq, g_q, gq_full, None, k, v, None, batch, t, layer), ckv_n
    c64, s64 = rope
    z64 = jnp.zeros_like(c64)
    ck, sk = jnp.concatenate([c64, z64], axis=1), jnp.concatenate([s64, z64], axis=1)
    cq = jnp.concatenate([jnp.ones((t, 128), F32), c64, z64], axis=1)
    sq = jnp.concatenate([jnp.zeros((t, 128), F32), s64, z64], axis=1)
    kpp_spec = pl.BlockSpec((MLA_TR, 128), lambda b, i: (b * nt + i, COL_KPE_PERM // 128))
    k, v = _mla_kv(proj, ckv_spec, proj, kpe_spec, (proj, kpp_spec, ck, sk), w_ukv, g_kv, gk_n, gk_r, gk_rp,
                   batch, t, layer, normalize=True, emit_ckv=False)
    ctx_ckv, ctx_kpe = ctx
    lc = ctx_ckv.shape[2]
    kc, vc = _mla_kv(ctx_ckv, pl.BlockSpec((None, None, MLA_TR, MLA_KV_LORA), lambda b, i: (b, layer, i, 0)),
                     ctx_kpe, pl.BlockSpec((None, None, MLA_TR, 128), lambda b, i: (b, layer, i, 0)),
                     None, w_ukv, g_kv, gk_n, gk_r, gk_rp, batch, lc, layer, normalize=False, emit_ckv=False)
    return _mla_attn(proj, wq, g_q, gq_full, (wqp, gq_perm, cq, sq), k, v, (kc, vc), batch, t, layer), None


SWA_TR = 128
SWA_NEG = -1e30


def _pair_sum_matrix():
    g = (np.arange(128)[:, None] // 64 == np.arange(128)[None, :] // 64).astype(np.float32)
    return jnp.asarray(np.concatenate([g, g], axis=0), BF16)


def _group_rms_scale(x, pair_ref):
    sq = x * x
    hi = sq.astype(BF16)
    lo = (sq - hi.astype(F32)).astype(BF16)
    ss = jnp.dot(jnp.concatenate([hi, lo], axis=1), pair_ref[...], preferred_element_type=F32)
    return lax.rsqrt(ss * (1.0 / SWA_HEAD_DIM) + NORM_EPS)


def _swa_kv_kernel(*refs, normalize, rope, emit, pad_blocks, layer):
    it = iter(refs)
    k_ref, v_ref = next(it), next(it)
    if rope:
        kp_ref, c_ref, s_ref = next(it), next(it), next(it)
    if normalize:
        g_ref, gp_ref, pair_ref = next(it), next(it), next(it)
    ko_ref, vo_ref = next(it), next(it)
    if emit:
        kn_ref, vn_ref = next(it), next(it)
    lane = lax.broadcasted_iota(jnp.int32, (SWA_TR, 128), 1)
    low = lane < 64

    def halves(x, c, o_ref):
        sw = pltpu.roll(x, 64, axis=1)
        zero = jnp.zeros_like(x)
        o_ref[4 * c + 0] = jnp.where(low, x, zero).astype(BF16)
        o_ref[4 * c + 1] = jnp.where(low, zero, sw).astype(BF16)
        o_ref[4 * c + 2] = jnp.where(low, sw, zero).astype(BF16)
        o_ref[4 * c + 3] = jnp.where(low, zero, x).astype(BF16)

    def compute():
        for c in range(2):
            sl = slice(128 * c, 128 * c + 128)
            kb = k_ref[:, sl]
            if normalize:
                r = _group_rms_scale(kb, pair_ref)
                kn = (kb * r) * g_ref[...]
                if emit:
                    kn_ref[:, sl] = kn
                if rope:
                    kn = kn * c_ref[...] + ((kp_ref[:, sl] * r) * gp_ref[...]) * s_ref[...]
            else:
                kn = kb
            halves(kn, c, ko_ref)
            vb = v_ref[:, sl]
            if emit:
                vn_ref[:, sl] = vb
            halves(vb, c, vo_ref)

    if pad_blocks:
        t = pl.program_id(1)
        is_pad = jnp.logical_or(t == 0, t == pl.num_programs(1) - 1)

        @pl.when(is_pad)
        def _():
            ko_ref[...] = jnp.zeros_like(ko_ref)
            vo_ref[...] = jnp.zeros_like(vo_ref)

        pl.when(jnp.logical_not(is_pad))(compute)
    else:
        compute()


def _swa_kv(k_src, k_spec, v_src, v_spec, rope_ins, norm_ins, batch, rows, layer, emit, pad_blocks):
    rope = rope_ins is not None
    normalize = norm_ins is not None
    nt = rows // SWA_TR + (2 if pad_blocks else 0)
    ins, specs = [k_src, v_src], [k_spec, v_spec]
    if rope:
        kp_src, kp_spec, ctab, stab = rope_ins
        tab_row = (lambda t: jnp.clip(t - 1, 0, nt - 3)) if pad_blocks else (lambda t: t)
        tab_spec = pl.BlockSpec((SWA_TR, 128), lambda b, t: (tab_row(t), 0))
        ins += [kp_src, ctab, stab]
        specs += [kp_spec, tab_spec, tab_spec]
    if normalize:
        ins += list(norm_ins)
        specs += [pl.BlockSpec((1, 128), lambda b, t: (0, 0)), pl.BlockSpec((1, 128), lambda b, t: (0, 0)),
                  pl.BlockSpec((256, 128), lambda b, t: (0, 0))]
    out_shape = [jax.ShapeDtypeStruct((batch, 8, nt * SWA_TR, 128), BF16)] * 2
    out_specs = [pl.BlockSpec((None, 8, SWA_TR, 128), lambda b, t: (b, 0, t, 0))] * 2
    if emit:
        out_shape += [jax.ShapeDtypeStruct((batch, rows, 256), F32)] * 2
        out_specs += [pl.BlockSpec((None, SWA_TR, 256), lambda b, t: (b, t, 0))] * 2
    return pl.pallas_call(
        functools.partial(_swa_kv_kernel, normalize=normalize, rope=rope, emit=emit, pad_blocks=pad_blocks, layer=layer),
        grid=(batch, nt), in_specs=specs, out_specs=out_specs, out_shape=out_shape,
        compiler_params=pltpu.CompilerParams(dimension_semantics=("arbitrary", "arbitrary")),
        name="swa_kv",
    )(*ins)


def _swa_attn_kernel(*refs, rope, windowed, has_ctx, t_total, layer):
    it = iter(refs)
    sink_ref, q_ref = next(it), next(it)
    if rope:
        qp_ref, c_ref, s_ref = next(it), next(it), next(it)
    g_ref, gp_ref, pair_ref = next(it), next(it), next(it)
    k_ref, v_ref = next(it), next(it)
    if has_ctx:
        kc_ref, vc_ref = next(it), next(it)
    o_ref = next(it)
    n = pl.program_id(1)
    if windowed:
        start = pl.multiple_of(n * SWA_TR, SWA_TR)
        win = pl.ds(start, 3 * SWA_TR)
        r_i = lax.broadcasted_iota(jnp.int32, (SWA_TR, 3 * SWA_TR), 0)
        c_i = lax.broadcasted_iota(jnp.int32, (SWA_TR, 3 * SWA_TR), 1)
        kpos = (n - 1) * SWA_TR + c_i
        diff = SWA_TR + r_i - c_i
        valid = (kpos >= 0) & (kpos < t_total) & (diff <= SWA_WINDOW) & (diff >= -SWA_WINDOW)
    else:
        win = slice(None)
    for cb in range(SWA_HEADS // 2):
        sl = slice(128 * cb, 128 * cb + 128)
        qb = q_ref[:, sl]
        r = _group_rms_scale(qb, pair_ref)
        qn = (qb * r) * g_ref[...]
        if rope:
            qn = qn * c_ref[...] + ((qp_ref[:, sl] * r) * gp_ref[...]) * s_ref[...]
        qn = (qn * SWA_SCALE).astype(BF16)
        acc = jnp.zeros((SWA_TR, 128), F32)
        for e in range(2):
            h = 2 * cb + e
            idx = 2 * (h // 4) + e
            sink = sink_ref[layer, h]
            s = lax.dot_general(qn, k_ref[idx, win, :], NT_DIMS, preferred_element_type=F32)
            if windowed:
                s = jnp.where(valid, s, SWA_NEG)
            m = jnp.maximum(jnp.max(s, axis=-1, keepdims=True), sink)
            if has_ctx:
                sc = lax.dot_general(qn, kc_ref[idx], NT_DIMS, preferred_element_type=F32)
                m = jnp.maximum(m, jnp.max(sc, axis=-1, keepdims=True))
            p = jnp.exp(s - m)
            den = jnp.sum(p, axis=-1, keepdims=True) + jnp.exp(sink - m)
            if has_ctx:
                pc = jnp.exp(sc - m)
                den = den + jnp.sum(pc, axis=-1, keepdims=True)
            inv = 1.0 / den
            acc = acc + jnp.dot((p * inv).astype(BF16), v_ref[idx, win, :], preferred_element_type=F32)
            if has_ctx:
                acc = acc + jnp.dot((pc * inv).astype(BF16), vc_ref[idx], preferred_element_type=F32)
        o_ref[:, sl] = acc.astype(BF16)


def _swa_attn(sink, proj, rope_ins, norm_ins, k, v, ctx_kv, batch, t, layer):
    rope = rope_ins is not None
    has_ctx = ctx_kv is not None
    nq = t // SWA_TR
    ins = [sink, proj]
    specs = [pl.BlockSpec(memory_space=pltpu.SMEM),
             pl.BlockSpec((SWA_TR, SWA_WIDTH), lambda b, i: (b * nq + i, COL_SQ // SWA_WIDTH))]
    if rope:
        ctab, stab = rope_ins
        tab_spec = pl.BlockSpec((SWA_TR, 128), lambda b, i: (i, 0))
        ins += [proj, ctab, stab]
        specs += [pl.BlockSpec((SWA_TR, SWA_WIDTH), lambda b, i: (b * nq + i, COL_SQ_PERM // SWA_WIDTH)), tab_spec, tab_spec]
    ins += list(norm_ins)
    specs += [pl.BlockSpec((1, 128), lambda b, i: (0, 0)), pl.BlockSpec((1, 128), lambda b, i: (0, 0)),
              pl.BlockSpec((256, 128), lambda b, i: (0, 0))]
    lk = k.shape[2]
    ins += [k, v]
    specs += [pl.BlockSpec((None, 8, lk, 128), lambda b, i: (b, 0, 0, 0))] * 2
    if has_ctx:
        lc = ctx_kv[0].shape[2]
        ins += list(ctx_kv)
        specs += [pl.BlockSpec((None, 8, lc, 128), lambda b, i: (b, 0, 0, 0))] * 2
    return pl.pallas_call(
        functools.partial(_swa_attn_kernel, rope=rope, windowed=has_ctx, has_ctx=has_ctx, t_total=t, layer=layer),
        grid=(batch, nq), in_specs=specs,
        out_specs=pl.BlockSpec((SWA_TR, SWA_WIDTH), lambda b, i: (b * nq + i, 0)),
        out_shape=jax.ShapeDtypeStruct((batch * t, SWA_WIDTH), BF16),
        compiler_params=pltpu.CompilerParams(dimension_semantics=("arbitrary", "arbitrary")),
        name="swa_attn",
    )(*ins)


def _swa_branch(proj, batch, t, layer, g_qn_l, g_kn_l, sink, rope, ctx):
    perm, _ = _rope_partner()
    pair = _pair_sum_matrix()
    tile2 = lambda g: jnp.concatenate([g, g])[None]
    nt = t // SWA_TR
    if rope is None:
        k_spec = pl.BlockSpec((SWA_TR, 256), lambda b, i: (b * nt + i, COL_SK // 256))
        v_spec = pl.BlockSpec((SWA_TR, 256), lambda b, i: (b * nt + i, COL_SV // 256))
        k, v, k_n, v_raw = _swa_kv(proj, k_spec, proj, v_spec, None, (tile2(g_kn_l), tile2(g_kn_l[perm]), pair),
                                   batch, t, layer, emit=True, pad_blocks=False)
        o = _swa_attn(sink, proj, None, (tile2(g_qn_l), tile2(g_qn_l[perm]), pair), k, v, None, batch, t, layer)
        return o, k_n, v_raw
    c64, s64 = rope
    ctab, stab = jnp.concatenate([c64, c64], axis=1), jnp.concatenate([s64, s64], axis=1)
    row = lambda b, i: b * nt + jnp.clip(i - 1, 0, nt - 1)
    k_spec = pl.BlockSpec((SWA_TR, 256), lambda b, i: (row(b, i), COL_SK // 256))
    v_spec = pl.BlockSpec((SWA_TR, 256), lambda b, i: (row(b, i), COL_SV // 256))
    kp_spec = pl.BlockSpec((SWA_TR, 256), lambda b, i: (row(b, i), COL_SK_PERM // 256))
    k, v = _swa_kv(proj, k_spec, proj, v_spec, (proj, kp_spec, ctab, stab), (tile2(g_kn_l), tile2(g_kn_l[perm]), pair),
                   batch, t, layer, emit=False, pad_blocks=True)
    ctx_k, ctx_v = ctx
    lc = ctx_k.shape[2]
    c_spec = pl.BlockSpec((None, None, SWA_TR, 256), lambda b, i: (b, layer, i, 0))
    kc, vc = _swa_kv(ctx_k, c_spec, ctx_v, c_spec, None, None, batch, lc, layer, emit=False, pad_blocks=False)
    o = _swa_attn(sink, proj, (ctab, stab), (tile2(g_qn_l), tile2(g_qn_l[perm]), pair), k, v, (kc, vc), batch, t, layer)
    return o, None, None


GLA_LEVELS = (64, 32, 16, 8, 4, 2, 1)
GLA_ROWSETS = 2 + len(GLA_LEVELS)


def _gla_sum_matrix(backward):
    c = GLA_CHUNK
    t = np.arange(c)[:, None]
    j = np.arange(c)[None, :]
    sets = [j <= t, j > t]
    for g in GLA_LEVELS:
        e = (t // (2 * g)) * 2 * g + g - 1
        upper = (t // g) % 2 == 1
        sets.append(np.where(upper, (j > e) & (j <= t), (j > t) & (j <= e)))
    n = np.concatenate(sets, axis=0).astype(np.float32)
    if backward:
        n = n.reshape(GLA_ROWSETS, c, c)[:, ::-1, ::-1].reshape(GLA_ROWSETS * c, c)
    return jnp.asarray(np.concatenate([n, n, n], axis=1), BF16)


def _gla_kernel(*refs, has_s0):
    it = iter(refs)
    srcs = [tuple(next(it) for _ in range(4)) for _ in range(2)]
    nmat = (next(it), next(it))
    wg = (next(it), next(it))
    bg = (next(it), next(it))
    s0_ref = next(it) if has_s0 else None
    o_refs = (next(it), next(it))
    sfin_ref = next(it)
    s_ref = next(it)
    i = pl.program_id(1)
    c = GLA_CHUNK

    @pl.when(i == 0)
    def _():
        s_ref[...] = s0_ref[...] if has_s0 else jnp.zeros_like(s_ref)

    row = lax.broadcasted_iota(jnp.int32, (c, c), 0)
    col = lax.broadcasted_iota(jnp.int32, (c, c), 1)
    diag = row == col
    log2 = lambda g: int(g).bit_length() - 1
    odd_half = [((row >> log2(g)) & 1) == 1 for g in GLA_LEVELS]
    same_block = [(row >> (log2(g) + 1)) == (col >> (log2(g) + 1)) for g in GLA_LEVELS]
    for d in range(2):
        q_ref, k_ref, v_ref, gg_ref = srcs[d]
        z = jnp.dot(gg_ref[...].astype(BF16), wg[d][...], preferred_element_type=F32) + bg[d][...]
        la_all = (jnp.minimum(z, 0.0) - jnp.log1p(jnp.exp(-jnp.abs(z)))) * (1.0 / GLA_GATE_NORM)
        for h in range(GLA_HEADS):
            sl = slice(GLA_DK * h, GLA_DK * h + GLA_DK)
            vs = slice(GLA_DV * h, GLA_DV * h + GLA_DV)
            la = la_all[:, sl]
            hi = la.astype(BF16)
            r1 = la - hi.astype(F32)
            mid = r1.astype(BF16)
            lo = (r1 - mid.astype(F32)).astype(BF16)
            ex = jnp.dot(nmat[d][...], jnp.concatenate([hi, mid, lo], axis=0), preferred_element_type=F32)
            q = q_ref[:, sl] * (GLA_DK ** -0.5)
            k = k_ref[:, sl]
            v = v_ref[:, vs].astype(BF16)
            eb = jnp.exp(ex[0:c])
            ek = jnp.exp(ex[c:2 * c])
            s_old = s_ref[d, h]
            o = jnp.dot((q * eb).astype(BF16), s_old.astype(BF16), preferred_element_type=F32)
            a = jnp.where(diag, jnp.sum(q * k, axis=-1, keepdims=True), 0.0)
            for li, g in enumerate(GLA_LEVELS):
                eg = jnp.exp(ex[(2 + li) * c:(3 + li) * c])
                qe, ke = q * eg, k * eg
                qg = (jnp.where(odd_half[li], qe, 0.0) if d == 0 else jnp.where(odd_half[li], 0.0, qe)).astype(BF16)
                kg = (jnp.where(odd_half[li], 0.0, ke) if d == 0 else jnp.where(odd_half[li], ke, 0.0)).astype(BF16)
                ag = lax.dot_general(qg, kg, NT_DIMS, preferred_element_type=F32)
                if 2 * g < c:
                    ag = jnp.where(same_block[li], ag, 0.0)
                a = a + ag
            o = o + jnp.dot(a.astype(BF16), v, preferred_element_type=F32)
            o_refs[d][:, vs] = o
            a_col = eb.T[:, c - 1:c] if d == 0 else eb.T[:, 0:1]
            kt = (k * ek).T.astype(BF16)
            s_ref[d, h] = a_col * s_old + jnp.dot(kt, v, preferred_element_type=F32)

    @pl.when(i == pl.num_programs(1) - 1)
    def _():
        sfin_ref[...] = s_ref[...]


def _gla(proj, batch, t, layer, w_gf, b_gf, w_gb, b_gb, s0):
    c = GLA_CHUNK
    n = t // c
    fwd = lambda b, i: b * n + i
    bwd = lambda b, i: b * n + (n - 1 - i)
    ins, specs = [], []
    for rowf in (fwd, bwd):
        ins += [proj] * 4
        specs += [pl.BlockSpec((c, 512), lambda b, i, rowf=rowf: (rowf(b, i), COL_GQ // 512)),
                  pl.BlockSpec((c, 512), lambda b, i, rowf=rowf: (rowf(b, i), COL_GK // 512)),
                  pl.BlockSpec((c, GLA_WIDTH), lambda b, i, rowf=rowf: (rowf(b, i), COL_GV // GLA_WIDTH)),
                  pl.BlockSpec((c, 128), lambda b, i, rowf=rowf: (rowf(b, i), COL_GG // 128))]
    const = lambda shape: pl.BlockSpec(shape, lambda b, i: (0,) * len(shape))
    pad_w = lambda w, off: jnp.zeros((128, 512), F32).at[off:off + GLA_GATE_RANK].set(w).astype(BF16)
    ins += [_gla_sum_matrix(False), _gla_sum_matrix(True), pad_w(w_gf, 0), pad_w(w_gb, GLA_GATE_RANK), b_gf[None], b_gb[None]]
    specs += [const((GLA_ROWSETS * c, 3 * c))] * 2 + [const((128, 512))] * 2 + [const((1, 512))] * 2
    state_spec = pl.BlockSpec((None, 2, GLA_HEADS, GLA_DK, GLA_DV), lambda b, i: (b, 0, 0, 0, 0))
    if s0 is not None:
        ins.append(s0)
        specs.append(pl.BlockSpec((None, None, 2, GLA_HEADS, GLA_DK, GLA_DV), lambda b, i: (b, layer, 0, 0, 0, 0)))
    return pl.pallas_call(
        functools.partial(_gla_kernel, has_s0=s0 is not None),
        grid=(batch, n), in_specs=specs,
        out_specs=[pl.BlockSpec((c, GLA_WIDTH), lambda b, i: (fwd(b, i), 0)),
                   pl.BlockSpec((c, GLA_WIDTH), lambda b, i: (bwd(b, i), 0)), state_spec],
        out_shape=[jax.ShapeDtypeStruct((batch * t, GLA_WIDTH), F32)] * 2
        + [jax.ShapeDtypeStruct((batch, 2, GLA_HEADS, GLA_DK, GLA_DV), F32)],
        scratch_shapes=[pltpu.VMEM((2, GLA_HEADS, GLA_DK, GLA_DV), F32)],
        compiler_params=pltpu.CompilerParams(dimension_semantics=("arbitrary", "arbitrary")),
        name="gla",
    )(*ins)


GOUT_TM = 512


def _gla_out_kernel(of_ref, ob_ref, gate_ref, g_ref, o_ref, *, layer):
    g = g_ref[pl.ds(layer, 1), :]
    for h in range(GLA_HEADS):
        vs = slice(GLA_DV * h, GLA_DV * h + GLA_DV)
        o = of_ref[:, vs] + ob_ref[:, vs]
        y = (o * lax.rsqrt(jnp.mean(o * o, axis=-1, keepdims=True) + NORM_EPS)) * g
        gate = gate_ref[:, vs]
        o_ref[:, vs] = (y * (gate * jax.nn.sigmoid(gate))).astype(BF16)


def _gla_out(o_f, o_b, proj, g_gla_out, layer):
    m = o_f.shape[0]
    return pl.pallas_call(
        functools.partial(_gla_out_kernel, layer=layer),
        grid=(m // GOUT_TM,),
        in_specs=[pl.BlockSpec((GOUT_TM, GLA_WIDTH), lambda i: (i, 0)),
                  pl.BlockSpec((GOUT_TM, GLA_WIDTH), lambda i: (i, 0)),
                  pl.BlockSpec((GOUT_TM, GLA_WIDTH), lambda i: (i, COL_GOUT // GLA_WIDTH)),
                  pl.BlockSpec((DEPTH, GLA_DV), lambda i: (0, 0))],
        out_specs=pl.BlockSpec((GOUT_TM, GLA_WIDTH), lambda i: (i, 0)),
        out_shape=jax.ShapeDtypeStruct((m, GLA_WIDTH), BF16),
        compiler_params=pltpu.CompilerParams(dimension_semantics=("arbitrary",)),
        name="gla_out",
    )(o_f, o_b, proj, g_gla_out)


MERGE_TM = 1024
MERGE_TN = 512


def _merge_kernel(om_ref, og_ref, os_ref, gm_ref, gg_ref, gs_ref, wm_ref, wg_ref, ws_ref, o_ref):
    def branch(o_r, gate_r, w_r):
        y = jnp.dot(o_r[...], w_r[...].astype(BF16), preferred_element_type=F32)
        return jax.nn.sigmoid(gate_r[...]) * y

    merged = branch(om_ref, gm_ref, wm_ref) + branch(og_ref, gg_ref, wg_ref) + branch(os_ref, gs_ref, ws_ref)
    o_ref[...] = merged.astype(BF16)


def _merge(o_mla, o_gla, o_swa, proj, w_br_mla, w_br_gla, w_br_swa, layer):
    m = o_mla.shape[0]
    nn = D_MODEL // MERGE_TN
    o_spec = pl.BlockSpec((MERGE_TM, 1024), lambda i, j: (i, 0))
    gate_spec = lambda col: pl.BlockSpec((MERGE_TM, MERGE_TN), lambda i, j: (i, col // MERGE_TN + j))
    w_spec = pl.BlockSpec((None, 1024, MERGE_TN), lambda i, j: (layer, 0, j))
    est = 2 * (3 * MERGE_TM * 1024 * 2 + 3 * MERGE_TM * MERGE_TN * 4 + 3 * 1024 * MERGE_TN * 4 + MERGE_TM * MERGE_TN * 2)
    return pl.pallas_call(
        _merge_kernel,
        grid=(m // MERGE_TM, nn),
        in_specs=[o_spec, o_spec, o_spec, gate_spec(COL_GATE_MLA), gate_spec(COL_GATE_GLA), gate_spec(COL_GATE_SWA),
                  w_spec, w_spec, w_spec],
        out_specs=pl.BlockSpec((MERGE_TM, MERGE_TN), lambda i, j: (i, j)),
        out_shape=jax.ShapeDtypeStruct((m, D_MODEL), BF16),
        compiler_params=pltpu.CompilerParams(
            dimension_semantics=("arbitrary", "arbitrary"), vmem_limit_bytes=_vmem_limit(est)),
        name="merge",
    )(o_mla, o_gla, o_swa, proj, proj, proj, w_br_mla, w_br_gla, w_br_swa)


OUT_TM = 1024
OUT_TN = 512


def _out_proj_kernel(m_ref, w_ref, x_ref, gate_ref, o_ref, *, base_row, rows_per_cond):
    row = _cond_row(pl.program_id(0), m_ref.shape[0], base_row, rows_per_cond)
    y = jnp.dot(m_ref[...], w_ref[...].astype(BF16), preferred_element_type=F32)
    o_ref[...] = x_ref[...] + gate_ref[pl.ds(row, 1), :] * y


def _out_proj(merged, x, mods, w_out, layer, base_row, rows_per_cond):
    m = x.shape[0]
    est = 2 * (OUT_TM * D_MODEL * 2 + D_MODEL * OUT_TN * 4 + 2 * OUT_TM * OUT_TN * 4)
    return pl.pallas_call(
        functools.partial(_out_proj_kernel, base_row=base_row, rows_per_cond=rows_per_cond),
        grid=(m // OUT_TM, D_MODEL // OUT_TN),
        in_specs=[pl.BlockSpec((OUT_TM, D_MODEL), lambda i, j: (i, 0)),
                  pl.BlockSpec((None, D_MODEL, OUT_TN), lambda i, j: (layer, 0, j)),
                  pl.BlockSpec((OUT_TM, OUT_TN), lambda i, j: (i, j)),
                  pl.BlockSpec((None, None, 8, OUT_TN), lambda i, j: (layer, 5, 0, j))],
        out_specs=pl.BlockSpec((OUT_TM, OUT_TN), lambda i, j: (i, j)),
        out_shape=jax.ShapeDtypeStruct((m, D_MODEL), F32),
        compiler_params=pltpu.CompilerParams(
            dimension_semantics=("arbitrary", "arbitrary"), vmem_limit_bytes=_vmem_limit(est)),
        name="out_proj",
    )(merged, w_out, x, mods)


def _trunk_layer(x, mods, layer, w, group):
    batch, t, base_row, rows_per_cond, rope, ctx = group
    x = _ffn(x, mods, layer, 0, w['g_norm1'], w['w_ff1_gu'], w['w_ff1_down'], base_row, rows_per_cond)
    ncols = PROJ_COLS_PROMPT if rope is None else PROJ_COLS_SAMPLE
    proj = _in_proj(x, mods, layer, w['g_norm2'], w['w_in_packed'][layer], ncols, base_row, rows_per_cond)
    mla_ctx = swa_ctx = s0 = None
    if ctx is not None:
        mla_ctx, swa_ctx, s0 = ctx[:2], ctx[2:4], ctx[4]
    o_mla, ckv_n = _mla_branch(proj, batch, t, layer, w['mla_packed'][layer], w['w_mla_ukv'], w['g_mla_q'], w['g_mla_kv'],
                               rope, mla_ctx)
    o_swa, k_n, v_raw = _swa_branch(proj, batch, t, layer, w['g_swa_qn'][layer], w['g_swa_kn'][layer], w['swa_sink'],
                                    rope, swa_ctx)
    o_f, o_b, s_fin = _gla(proj, batch, t, layer, w['w_gla_gf'][layer], w['b_gla_gf'][layer],
                           w['w_gla_gb'][layer], w['b_gla_gb'][layer], s0)
    o_gla = _gla_out(o_f, o_b, proj, w['g_gla_out'], layer)
    merged = _merge(o_mla, o_gla, o_swa, proj, w['w_br_mla'], w['w_br_gla'], w['w_br_swa'], layer)
    x = _out_proj(merged, x, mods, w['w_out'], layer, base_row, rows_per_cond)
    x = _ffn(x, mods, layer, 6, w['g_norm3'], w['w_ff2_gu'], w['w_ff2_down'], base_row, rows_per_cond)
    new_ctx = None
    if ctx is None:
        kpe = proj[:, COL_KPE:COL_KPE + MLA_ROPE].reshape(batch, t, MLA_ROPE)
        new_ctx = (ckv_n, kpe, k_n.reshape(batch, t, SWA_KV_HEADS, SWA_HEAD_DIM),
                   v_raw.reshape(batch, t, SWA_KV_HEADS, SWA_HEAD_DIM), s_fin)
    return x, new_ctx


def kernel(x_prompt, x_sample, cache_mla_ckv, cache_mla_kpe, cache_swa_k, cache_swa_v, state_gla,
           c, c_ctx, w_ada, b_ada, g_norm1, g_norm2, g_norm3,
           w_ff1_gu, w_ff1_down, w_ff2_gu, w_ff2_down, w_in,
           g_mla_q, w_mla_uq, g_mla_kv, w_mla_ukv, g_mla_qn, g_mla_kn,
           w_gla_gf, b_gla_gf, w_gla_gb, b_gla_gb, g_gla_out,
           g_swa_qn, g_swa_kn, swa_sink, w_br_mla, w_br_gla, w_br_swa, w_out):
    bp, tp, _ = x_prompt.shape
    bs, ts, _ = x_sample.shape
    assert bs + 1 <= 8, "conditioning rows are packed into one 8-row tile"
    cond8 = jnp.zeros((8, D_MODEL), F32).at[0].set(c_ctx).at[1:1 + bs].set(c)
    mods = _adaln(cond8, w_ada, b_ada)
    w = dict(g_norm1=g_norm1, g_norm2=g_norm2, g_norm3=g_norm3,
             w_ff1_gu=w_ff1_gu, w_ff1_down=w_ff1_down, w_ff2_gu=w_ff2_gu, w_ff2_down=w_ff2_down,
             w_in_packed=[_pack_w_in(w_in[l]) for l in range(DEPTH)],
             mla_packed=[_pack_mla(w_mla_uq[l], g_mla_qn[l], g_mla_kn[l]) for l in range(DEPTH)],
             w_mla_ukv=w_mla_ukv, g_mla_q=g_mla_q, g_mla_kv=g_mla_kv,
             g_swa_qn=g_swa_qn, g_swa_kn=g_swa_kn, swa_sink=swa_sink,
             w_gla_gf=w_gla_gf, b_gla_gf=b_gla_gf, w_gla_gb=w_gla_gb, b_gla_gb=b_gla_gb, g_gla_out=g_gla_out,
             w_br_mla=w_br_mla, w_br_gla=w_br_gla, w_br_swa=w_br_swa, w_out=w_out)
    past = cache_mla_ckv.shape[2]
    ctx = (cache_mla_ckv, jnp.pad(cache_mla_kpe, ((0, 0), (0, 0), (0, 0), (0, 128 - MLA_ROPE))),
           cache_swa_k.reshape(bs, DEPTH, past, SWA_KV_HEADS * SWA_HEAD_DIM),
           cache_swa_v.reshape(bs, DEPTH, past, SWA_KV_HEADS * SWA_HEAD_DIM), state_gla)
    group_p = (bp, tp, 0, bp * tp, None, None)
    group_s = (bs, ts, 1, ts, _rope_tables(ts), ctx)
    y_p = x_prompt.reshape(bp * tp, D_MODEL)
    y_s = x_sample.reshape(bs * ts, D_MODEL)
    new = []
    for l in range(DEPTH):
        y_p, new_ctx = _trunk_layer(y_p, mods, l, w, group_p)
        new.append(new_ctx)
        y_s, _ = _trunk_layer(y_s, mods, l, w, group_s)
    stacked = tuple(jnp.stack([new[l][k] for l in range(DEPTH)], axis=1) for k in range(5))
    return (y_p.reshape(bp, tp, D_MODEL), y_s.reshape(bs, ts, D_MODEL)) + stacked
```

```python
import functools

import numpy as np
import jax
import jax.numpy as jnp
from jax import lax
from jax.experimental import pallas as pl
from jax.experimental.pallas import tpu as pltpu

F32 = jnp.float32
BF16 = jnp.bfloat16

D_MODEL = 2048
DEPTH = 2
GRID_W = 64
ROPE_BASE = 10000.0
NORM_EPS = 1e-6
MLA_HEADS = 8
MLA_Q_LORA = 512
MLA_KV_LORA = 512
MLA_NOPE = 128
MLA_ROPE = 64
MLA_V = 128
MLA_QK = MLA_NOPE + MLA_ROPE
MLA_SCALE = MLA_QK ** -0.5
MLA_WIDTH = MLA_HEADS * MLA_V
MLA_HEAD_PAD = 256
GLA_HEADS = 4
GLA_DK = 128
GLA_DV = 256
GLA_GATE_RANK = 16
GLA_GATE_NORM = 16.0
GLA_WIDTH = GLA_HEADS * GLA_DV
GLA_CHUNK = 128
SWA_HEADS = 16
SWA_KV_HEADS = 4
SWA_HEAD_DIM = 64
SWA_WINDOW = 128
SWA_SCALE = SWA_HEAD_DIM ** -0.5
SWA_WIDTH = SWA_HEADS * SWA_HEAD_DIM
D_FF = 5632
N_MOD = 9
IN_SPLITS = (MLA_Q_LORA, MLA_KV_LORA, MLA_ROPE,
             GLA_HEADS * GLA_DK, GLA_HEADS * GLA_DK, GLA_WIDTH, GLA_WIDTH, GLA_GATE_RANK, GLA_GATE_RANK,
             SWA_WIDTH, SWA_KV_HEADS * SWA_HEAD_DIM, SWA_KV_HEADS * SWA_HEAD_DIM,
             D_MODEL, D_MODEL, D_MODEL)

V7X_LANES = 128
V7X_SUBLANES = 8
V7X_VMEM_BYTES = 64 * 1024 * 1024
MIB = 1024 * 1024

COL_GATE_MLA = 0
COL_GATE_GLA = 2048
COL_GATE_SWA = 4096
COL_CQ = 6144
COL_CKV = 6656
COL_GQ = 7168
COL_GK = 7680
COL_GV = 8192
COL_GOUT = 9216
COL_SQ = 10240
COL_SK = 11264
COL_SV = 11520
COL_KPE = 11776
COL_GG = 11904
PROJ_COLS_PROMPT = 12288
COL_SQ_PERM = 12288
COL_SK_PERM = 13312
COL_KPE_PERM = 13568
PROJ_COLS_SAMPLE = 14336
PROJ_TN = 1024


def _vmem_limit(nbytes):
    return int(min(nbytes + 12 * MIB, V7X_VMEM_BYTES - 6 * MIB))


def _rope_partner():
    i = np.arange(64)
    within = i % 32
    perm = (i // 32) * 32 + (within + 16) % 32
    sign = np.where(within < 16, -1.0, 1.0).astype(np.float32)
    return perm, sign


ADA_TN = 1024


def _adaln_kernel(cond_ref, w_ref, b_ref, o_ref):
    c = cond_ref[...]
    s = (c * jax.nn.sigmoid(c)).astype(BF16)
    o_ref[...] = jnp.dot(s, w_ref[...].astype(BF16), preferred_element_type=F32) + b_ref[...]


def _adaln(cond8, w_ada, b_ada):
    halves = D_MODEL // ADA_TN
    nj = N_MOD * halves
    return pl.pallas_call(
        _adaln_kernel,
        grid=(DEPTH, nj),
        in_specs=[
            pl.BlockSpec((8, D_MODEL), lambda l, j: (0, 0)),
            pl.BlockSpec((None, D_MODEL, ADA_TN), lambda l, j: (l, 0, j)),
            pl.BlockSpec((None, 1, ADA_TN), lambda l, j: (l, 0, j)),
        ],
        out_specs=pl.BlockSpec((None, None, 8, ADA_TN), lambda l, j: (l, j // halves, 0, j % halves)),
        out_shape=jax.ShapeDtypeStruct((DEPTH, N_MOD, 8, D_MODEL), F32),
        compiler_params=pltpu.CompilerParams(
            dimension_semantics=("arbitrary", "arbitrary"),
            vmem_limit_bytes=_vmem_limit(2 * D_MODEL * ADA_TN * 4)),
        name="adaln",
    )(cond8, w_ada, b_ada.reshape(DEPTH, 1, N_MOD * D_MODEL))


NORM_ROWS = 64


def _modulated_norm_to(h_ref, x_ref, g, shift, scale):
    gain = g * (1.0 + scale)

    def body(c, carry):
        r0 = pl.multiple_of(c * NORM_ROWS, NORM_ROWS)
        x = x_ref[pl.ds(r0, NORM_ROWS), :]
        ms = jnp.mean(x * x, axis=-1, keepdims=True)
        h_ref[pl.ds(r0, NORM_ROWS), :] = ((x * lax.rsqrt(ms + NORM_EPS)) * gain + shift).astype(BF16)
        return carry

    lax.fori_loop(0, x_ref.shape[0] // NORM_ROWS, body, 0, unroll=2)


def _cond_row(i, tm, base_row, rows_per_cond):
    return base_row + (i * tm) // rows_per_cond


FFN_TM = 1024
FFN_TF = 256
FFN_TN = 512


def _ffn_kernel(x_ref, shift_ref, scale_ref, gate_ref, g_ref, wg_ref, wu_ref, wd_ref, o_ref, h_ref,
                *, layer, base_row, rows_per_cond):
    i = pl.program_id(0)
    j = pl.program_id(1)
    row = _cond_row(i, x_ref.shape[0], base_row, rows_per_cond)

    @pl.when(j == 0)
    def _():
        _modulated_norm_to(h_ref, x_ref, g_ref[pl.ds(layer, 1), :],
                           shift_ref[pl.ds(row, 1), :], scale_ref[pl.ds(row, 1), :])
        o_ref[...] = jnp.zeros_like(o_ref)

    h = h_ref[...]
    a = jnp.dot(h, wg_ref[...].astype(BF16), preferred_element_type=F32)
    u = jnp.dot(h, wu_ref[...].astype(BF16), preferred_element_type=F32)
    act = (a * jax.nn.sigmoid(a) * u).astype(BF16)
    for n in range(0, D_MODEL, FFN_TN):
        o_ref[:, n:n + FFN_TN] += jnp.dot(act, wd_ref[:, n:n + FFN_TN].astype(BF16), preferred_element_type=F32)

    @pl.when(j == pl.num_programs(1) - 1)
    def _():
        o_ref[...] = x_ref[...] + (0.5 * gate_ref[pl.ds(row, 1), :]) * o_ref[...]


def _ffn(x, mods, layer, first_mod, g_norm, w_gu, w_down, base_row, rows_per_cond):
    m = x.shape[0]
    nf = D_FF // FFN_TF
    mod_spec = lambda k: pl.BlockSpec((None, None, 8, D_MODEL), lambda i, j: (layer, first_mod + k, 0, 0))
    est = (2 * FFN_TM * D_MODEL * 4 * 2 + FFN_TM * D_MODEL * 2
           + 2 * 3 * D_MODEL * FFN_TF * w_gu.dtype.itemsize)
    return pl.pallas_call(
        functools.partial(_ffn_kernel, layer=layer, base_row=base_row, rows_per_cond=rows_per_cond),
        grid=(m // FFN_TM, nf),
        in_specs=[
            pl.BlockSpec((FFN_TM, D_MODEL), lambda i, j: (i, 0)),
            mod_spec(0), mod_spec(1), mod_spec(2),
            pl.BlockSpec((DEPTH, D_MODEL), lambda i, j: (0, 0)),
            pl.BlockSpec((None, D_MODEL, FFN_TF), lambda i, j: (layer, 0, j)),
            pl.BlockSpec((None, D_MODEL, FFN_TF), lambda i, j: (layer, 0, j + nf)),
            pl.BlockSpec((None, FFN_TF, D_MODEL), lambda i, j: (layer, j, 0)),
        ],
        out_specs=pl.BlockSpec((FFN_TM, D_MODEL), lambda i, j: (i, 0)),
        out_shape=jax.ShapeDtypeStruct((m, D_MODEL), F32),
        scratch_shapes=[pltpu.VMEM((FFN_TM, D_MODEL), BF16)],
        compiler_params=pltpu.CompilerParams(
            dimension_semantics=("arbitrary", "arbitrary"), vmem_limit_bytes=_vmem_limit(est)),
        name="ffn",
    )(x, mods, mods, mods, g_norm, w_gu, w_gu, w_down)


PROJ_TM = 1024


IN_COLS = sum(IN_SPLITS)
_SRC = dict(zip(('cq', 'ckv', 'kpe', 'gq', 'gk', 'gv', 'gout', 'ggf', 'ggb', 'sq', 'sk', 'sv', 'gate_mla'),
                np.concatenate([[0], np.cumsum(IN_SPLITS)]).tolist()))
PACK_TR = 128
PACK_MAIN = (IN_COLS // 128) * 128


def _pack_kernel(main_ref, tail_ref, o_ref):
    lane = lax.broadcasted_iota(jnp.int32, (PACK_TR, 128), 1)
    first_half = (lane & 31) < 16
    rolled = {}

    def block(a):
        return tail_ref[...] if a == PACK_MAIN // 128 else main_ref[:, 128 * a:128 * a + 128]

    def rolled_block(a, shift):
        if (a, shift) not in rolled:
            rolled[(a, shift)] = pltpu.roll(block(a), shift, axis=1)
        return rolled[(a, shift)]

    def take128(src):
        a, sh = divmod(src, 128)
        if sh == 0:
            return block(a)
        return jnp.where(lane < 128 - sh, rolled_block(a, 128 - sh), rolled_block(a + 1, 128 - sh))

    def partner(x):
        return jnp.where(first_half, pltpu.roll(x, 112, axis=1), pltpu.roll(x, 16, axis=1))

    def put(dst, val):
        o_ref[:, dst:dst + 128] = val.astype(BF16)

    def copy_run(dst, src, width, f=lambda x: x):
        for b in range(width // 128):
            put(dst + 128 * b, f(take128(src + 128 * b)))

    zero = jnp.zeros((PACK_TR, 128), F32)
    copy_run(COL_GATE_MLA, _SRC['gate_mla'], 3 * D_MODEL)
    copy_run(COL_CQ, _SRC['cq'], MLA_Q_LORA + MLA_KV_LORA)
    copy_run(COL_GQ, _SRC['gq'], 2 * GLA_HEADS * GLA_DK + 2 * GLA_WIDTH)
    copy_run(COL_SQ, _SRC['sq'], SWA_WIDTH + 2 * SWA_KV_HEADS * SWA_HEAD_DIM)
    kpe = jnp.where(lane < MLA_ROPE, take128(_SRC['kpe']), zero)
    put(COL_KPE, kpe)
    put(COL_GG, jnp.where(lane < 2 * GLA_GATE_RANK, take128(_SRC['ggf']), zero))
    copy_run(COL_SQ_PERM, _SRC['sq'], SWA_WIDTH, partner)
    copy_run(COL_SK_PERM, _SRC['sk'], SWA_KV_HEADS * SWA_HEAD_DIM, partner)
    put(COL_KPE_PERM, jnp.where(lane < MLA_ROPE, partner(kpe), zero))
    for dst in list(range(COL_GG + 128, PROJ_COLS_PROMPT, 128)) + list(range(COL_KPE_PERM + 128, PROJ_COLS_SAMPLE, 128)):
        put(dst, zero)


def _pack_w_in(w_in):
    return pl.pallas_call(
        _pack_kernel,
        grid=(DEPTH, D_MODEL // PACK_TR),
        in_specs=[pl.BlockSpec((None, PACK_TR, PACK_MAIN), lambda l, i: (l, i, 0)),
                  pl.BlockSpec((None, PACK_TR, 128), lambda l, i: (l, i, PACK_MAIN // 128))],
        out_specs=pl.BlockSpec((None, PACK_TR, PROJ_COLS_SAMPLE), lambda l, i: (l, i, 0)),
        out_shape=jax.ShapeDtypeStruct((DEPTH, D_MODEL, PROJ_COLS_SAMPLE), BF16),
        compiler_params=pltpu.CompilerParams(
            dimension_semantics=("arbitrary", "arbitrary"),
            vmem_limit_bytes=_vmem_limit(2 * PACK_TR * (PACK_MAIN * 4 + PROJ_COLS_SAMPLE * 2))),
        name="pack_w_in",
    )(w_in, w_in)


def _proj_kernel(x_ref, shift_ref, scale_ref, g_ref, w_ref, o_ref, h_ref, *, layer, base_row, rows_per_cond):
    i = pl.program_id(0)
    j = pl.program_id(1)
    row = _cond_row(i, x_ref.shape[0], base_row, rows_per_cond)

    @pl.when(j == 0)
    def _():
        _modulated_norm_to(h_ref, x_ref, g_ref[pl.ds(layer, 1), :],
                           shift_ref[pl.ds(row, 1), :], scale_ref[pl.ds(row, 1), :])

    o_ref[...] = jnp.dot(h_ref[...], w_ref[...], preferred_element_type=F32)


def _in_proj(x, mods, layer, g_norm2, w_packed, ncols, base_row, rows_per_cond):
    m = x.shape[0]
    mod_spec = lambda k: pl.BlockSpec((None, None, 8, D_MODEL), lambda i, j: (layer, 3 + k, 0, 0))
    est = 2 * PROJ_TM * D_MODEL * 4 + PROJ_TM * D_MODEL * 2 + 2 * D_MODEL * PROJ_TN * 2 + 2 * PROJ_TM * PROJ_TN * 4
    return pl.pallas_call(
        functools.partial(_proj_kernel, layer=layer, base_row=base_row, rows_per_cond=rows_per_cond),
        grid=(m // PROJ_TM, ncols // PROJ_TN),
        in_specs=[
            pl.BlockSpec((PROJ_TM, D_MODEL), lambda i, j: (i, 0)),
            mod_spec(0), mod_spec(1),
            pl.BlockSpec((DEPTH, D_MODEL), lambda i, j: (0, 0)),
            pl.BlockSpec((None, D_MODEL, PROJ_TN), lambda i, j: (layer, 0, j)),
        ],
        out_specs=pl.BlockSpec((PROJ_TM, PROJ_TN), lambda i, j: (i, j)),
        out_shape=jax.ShapeDtypeStruct((m, ncols), F32),
        scratch_shapes=[pltpu.VMEM((PROJ_TM, D_MODEL), BF16)],
        compiler_params=pltpu.CompilerParams(
            dimension_semantics=("arbitrary", "arbitrary"), vmem_limit_bytes=_vmem_limit(est)),
        name="in_proj",
    )(x, mods, mods, g_norm2, w_packed)


def _rope_tables(t):
    pos = jnp.arange(t)
    inv_freq = ROPE_BASE ** (-jnp.arange(16, dtype=F32) / 16)

    def cs(p):
        ang = p.astype(F32)[:, None] * inv_freq[None, :]
        return jnp.concatenate([jnp.cos(ang)] * 2, axis=1), jnp.concatenate([jnp.sin(ang)] * 2, axis=1)

    cr, sr = cs(pos // GRID_W)
    cc, sc = cs(pos % GRID_W)
    _, sign = _rope_partner()
    return jnp.concatenate([cr, cc], axis=1), jnp.concatenate([sr, sc], axis=1) * sign[None, :]


MLA_TR = 256
MLA_TQ = 256
NT_DIMS = (((1,), (1,)), ((), ()))
LOG2_E = 1.4426950408889634


def _pack_mla(w_uq_l, g_qn_l, g_kn_l):
    perm, _ = _rope_partner()
    w = w_uq_l.reshape(MLA_Q_LORA, MLA_HEADS, MLA_QK)
    nope, rope_w = w[:, :, :MLA_NOPE], w[:, :, MLA_NOPE:]
    z64 = jnp.zeros((MLA_Q_LORA, MLA_HEADS, 64), w.dtype)
    wq = jnp.concatenate([nope, rope_w, z64], axis=-1).reshape(MLA_Q_LORA, -1).astype(BF16)
    wqp = jnp.concatenate([rope_w[:, :, perm], z64], axis=-1).reshape(MLA_Q_LORA, -1).astype(BF16)
    v64 = jnp.zeros((64,), F32)
    gq_full = jnp.concatenate([g_qn_l, v64])[None]
    gq_perm = jnp.concatenate([g_qn_l[MLA_NOPE:][perm], v64])[None]
    gk_n = g_kn_l[:MLA_NOPE][None]
    gk_r = jnp.concatenate([g_kn_l[MLA_NOPE:], v64])[None]
    gk_rp = jnp.concatenate([g_kn_l[MLA_NOPE:][perm], v64])[None]
    return wq, wqp, gq_full, gq_perm, gk_n, gk_r, gk_rp


def _mla_kv_kernel(*refs, layer, normalize, rope, emit_ckv):
    it = iter(refs)
    ckv_ref, kpe_ref = next(it), next(it)
    if rope:
        kpp_ref, c_ref, s_ref = next(it), next(it), next(it)
    w_ref, gkv_ref, gn_ref, gr_ref, grp_ref = next(it), next(it), next(it), next(it), next(it)
    k_ref, v_ref = next(it), next(it)
    ckv = ckv_ref[...]
    if normalize:
        ckv = (ckv * lax.rsqrt(jnp.mean(ckv * ckv, axis=-1, keepdims=True) + NORM_EPS)) * gkv_ref[pl.ds(layer, 1), :]
    if emit_ckv:
        next(it)[...] = ckv
    kv = jnp.dot(ckv.astype(BF16), w_ref[...].astype(BF16), preferred_element_type=F32)
    kpe = kpe_ref[...]
    ss_pe = jnp.sum(kpe * kpe, axis=-1, keepdims=True)
    kr = kpe * gr_ref[...]
    if rope:
        kr = kr * c_ref[...] + (kpp_ref[...] * grp_ref[...]) * s_ref[...]
    for h in range(MLA_HEADS):
        kn = kv[:, 256 * h:256 * h + 128]
        r = lax.rsqrt((jnp.sum(kn * kn, axis=-1, keepdims=True) + ss_pe) * (1.0 / MLA_QK) + NORM_EPS)
        k_ref[:, 256 * h:256 * h + 128] = ((kn * r) * gn_ref[...]).astype(BF16)
        k_ref[:, 256 * h + 128:256 * h + 256] = (kr * r).astype(BF16)
        v_ref[:, 128 * h:128 * h + 128] = kv[:, 256 * h + 128:256 * h + 256].astype(BF16)


def _mla_kv(ckv_src, ckv_spec, kpe_src, kpe_spec, rope_ins, w_ukv, g_kv, gk_n, gk_r, gk_rp,
            batch, rows, layer, normalize, emit_ckv):
    rope = rope_ins is not None
    nt = rows // MLA_TR
    ins = [ckv_src, kpe_src]
    specs = [ckv_spec, kpe_spec]
    if rope:
        kpp_src, kpp_spec, ctab, stab = rope_ins
        ins += [kpp_src, ctab, stab]
        specs += [kpp_spec, pl.BlockSpec((MLA_TR, 128), lambda b, t: (t, 0)), pl.BlockSpec((MLA_TR, 128), lambda b, t: (t, 0))]
    ins += [w_ukv, g_kv, gk_n, gk_r, gk_rp]
    specs += [pl.BlockSpec((None, MLA_KV_LORA, 2048), lambda b, t: (layer, 0, 0)),
              pl.BlockSpec((DEPTH, MLA_KV_LORA), lambda b, t: (0, 0))] + [pl.BlockSpec((1, 128), lambda b, t: (0, 0))] * 3
    out_shape = [jax.ShapeDtypeStruct((batch, rows, MLA_HEADS * MLA_HEAD_PAD), BF16),
                 jax.ShapeDtypeStruct((batch, rows, MLA_WIDTH), BF16)]
    out_specs = [pl.BlockSpec((None, MLA_TR, MLA_HEADS * MLA_HEAD_PAD), lambda b, t: (b, t, 0)),
                 pl.BlockSpec((None, MLA_TR, MLA_WIDTH), lambda b, t: (b, t, 0))]
    if emit_ckv:
        out_shape.append(jax.ShapeDtypeStruct((batch, rows, MLA_KV_LORA), F32))
        out_specs.append(pl.BlockSpec((None, MLA_TR, MLA_KV_LORA), lambda b, t: (b, t, 0)))
    est = 2 * MLA_KV_LORA * 2048 * 4 + 4 * MLA_TR * 2048 * 4
    return pl.pallas_call(
        functools.partial(_mla_kv_kernel, layer=layer, normalize=normalize, rope=rope, emit_ckv=emit_ckv),
        grid=(batch, nt), in_specs=specs, out_specs=out_specs, out_shape=out_shape,
        compiler_params=pltpu.CompilerParams(
            dimension_semantics=("arbitrary", "arbitrary"), vmem_limit_bytes=_vmem_limit(est)),
        name="mla_kv",
    )(*ins)


def _mla_attn_kernel(*refs, layer, rope, has_ctx):
    it = iter(refs)
    cq_ref, wq_ref, gq_ref, gfull_ref = next(it), next(it), next(it), next(it)
    if rope:
        wqp_ref, gperm_ref, c_ref, s_ref = next(it), next(it), next(it), next(it)
    k_ref, v_ref = next(it), next(it)
    if has_ctx:
        kc_ref, vc_ref = next(it), next(it)
    o_ref = next(it)
    cq = cq_ref[...]
    ql = ((cq * lax.rsqrt(jnp.mean(cq * cq, axis=-1, keepdims=True) + NORM_EPS)) * gq_ref[pl.ds(layer, 1), :]).astype(BF16)
    q_raw = jnp.dot(ql, wq_ref[...], preferred_element_type=F32)
    if rope:
        q_perm = jnp.dot(ql, wqp_ref[...], preferred_element_type=F32)
    for h in range(MLA_HEADS):
        sl = slice(256 * h, 256 * h + 256)
        vs = slice(128 * h, 128 * h + 128)
        q = q_raw[:, sl]
        r = lax.rsqrt(jnp.sum(q * q, axis=-1, keepdims=True) * (1.0 / MLA_QK) + NORM_EPS)
        if rope:
            q_rot = (q[:, 128:] * gfull_ref[:, 128:]) * c_ref[...] + (q_perm[:, vs] * gperm_ref[...]) * s_ref[...]
            qh = jnp.concatenate([q[:, :128] * gfull_ref[:, :128], q_rot], axis=1)
        else:
            qh = q * gfull_ref[...]
        qh = ((qh * r) * (MLA_SCALE * LOG2_E)).astype(BF16)
        s = lax.dot_general(qh, k_ref[:, sl], NT_DIMS, preferred_element_type=F32)
        m = jnp.max(s, axis=-1, keepdims=True)
        if has_ctx:
            sc = lax.dot_general(qh, kc_ref[:, sl], NT_DIMS, preferred_element_type=F32)
            m = jnp.maximum(m, jnp.max(sc, axis=-1, keepdims=True))
        e = jnp.exp2(s - m)
        den = jnp.sum(e, axis=-1, keepdims=True)
        o = jnp.dot(e.astype(BF16), v_ref[:, vs], preferred_element_type=F32)
        if has_ctx:
            ec = jnp.exp2(sc - m)
            den = den + jnp.sum(ec, axis=-1, keepdims=True)
            o = o + jnp.dot(ec.astype(BF16), vc_ref[:, vs], preferred_element_type=F32)
        o_ref[:, vs] = (o * (1.0 / den)).astype(BF16)


def _mla_attn(proj, wq, g_q, gq_full, rope_ins, k, v, ctx_kv, batch, t, layer):
    rope = rope_ins is not None
    has_ctx = ctx_kv is not None
    nq = t // MLA_TQ
    const = lambda shape: pl.BlockSpec(shape, lambda b, i: (0,) * len(shape))
    ins = [proj, wq, g_q, gq_full]
    specs = [pl.BlockSpec((MLA_TQ, MLA_Q_LORA), lambda b, i: (b * nq + i, COL_CQ // MLA_Q_LORA)),
             const((MLA_Q_LORA, 2048)), const((DEPTH, MLA_Q_LORA)), const((1, 256))]
    if rope:
        wqp, gq_perm, ctab, stab = rope_ins
        ins += [wqp, gq_perm, ctab, stab]
        specs += [const((MLA_Q_LORA, 1024)), const((1, 128)),
                  pl.BlockSpec((MLA_TQ, 128), lambda b, i: (i, 0)), pl.BlockSpec((MLA_TQ, 128), lambda b, i: (i, 0))]
    ins += [k, v]
    specs += [pl.BlockSpec((None, t, 2048), lambda b, i: (b, 0, 0)), pl.BlockSpec((None, t, MLA_WIDTH), lambda b, i: (b, 0, 0))]
    est = 2 * (t * 2048 * 2 + t * MLA_WIDTH * 2) + 6 * MLA_TQ * t * 4
    if has_ctx:
        kc, vc = ctx_kv
        lc = kc.shape[1]
        ins += [kc, vc]
        specs += [pl.BlockSpec((None, lc, 2048), lambda b, i: (b, 0, 0)), pl.BlockSpec((None, lc, MLA_WIDTH), lambda b, i: (b, 0, 0))]
        est += 2 * lc * 3072 * 2
    return pl.pallas_call(
        functools.partial(_mla_attn_kernel, layer=layer, rope=rope, has_ctx=has_ctx),
        grid=(batch, nq), in_specs=specs,
        out_specs=pl.BlockSpec((MLA_TQ, MLA_WIDTH), lambda b, i: (b * nq + i, 0)),
        out_shape=jax.ShapeDtypeStruct((batch * t, MLA_WIDTH), BF16),
        compiler_params=pltpu.CompilerParams(
            dimension_semantics=("arbitrary", "arbitrary"), vmem_limit_bytes=_vmem_limit(est)),
        name="mla_attn",
    )(*ins)


def _mla_branch(proj, batch, t, layer, mla_w, w_ukv, g_q, g_kv, rope, ctx):
    wq, wqp, gq_full, gq_perm, gk_n, gk_r, gk_rp = mla_w
    nt = t // MLA_TR
    ckv_spec = pl.BlockSpec((MLA_TR, MLA_KV_LORA), lambda b, i: (b * nt + i, COL_CKV // MLA_KV_LORA))
    kpe_spec = pl.BlockSpec((MLA_TR, 128), lambda b, i: (b * nt + i, COL_KPE // 128))
    if rope is None:
        k, v, ckv_n = _mla_kv(proj, ckv_spec, proj, kpe_spec, None, w_ukv, g_kv, gk_n, gk_r, gk_rp,
                              batch, t, layer, normalize=True, emit_ckv=True)
        return _mla_attn(proj, wq, g_q, gq_full, None, k, v, None, batch, t, layer), ckv_n
    c64, s64 = rope
    z64 = jnp.zeros_like(c64)
    ck, sk = jnp.concatenate([c64, z64], axis=1), jnp.concatenate([s64, z64], axis=1)
    kpp_spec = pl.BlockSpec((MLA_TR, 128), lambda b, i: (b * nt + i, COL_KPE_PERM // 128))
    k, v = _mla_kv(proj, ckv_spec, proj, kpe_spec, (proj, kpp_spec, ck, sk), w_ukv, g_kv, gk_n, gk_r, gk_rp,
                   batch, t, layer, normalize=True, emit_ckv=False)
    ctx_ckv, ctx_kpe = ctx
    lc = ctx_ckv.shape[2]
    kc, vc = _mla_kv(ctx_ckv, pl.BlockSpec((None, None, MLA_TR, MLA_KV_LORA), lambda b, i: (b, layer, i, 0)),
                     ctx_kpe, pl.BlockSpec((None, None, MLA_TR, 128), lambda b, i: (b, layer, i, 0)),
                     None, w_ukv, g_kv, gk_n, gk_r, gk_rp, batch, lc, layer, normalize=False, emit_ckv=False)
    return _mla_attn(proj, wq, g_q, gq_full, (wqp, gq_perm, ck, sk), k, v, (kc, vc), batch, t, layer), None


SWA_TR = 128
SWA_NEG = -1e30


def _pair_sum_matrix():
    g = (np.arange(128)[:, None] // 64 == np.arange(128)[None, :] // 64).astype(np.float32)
    return jnp.asarray(np.concatenate([g, g], axis=0), BF16)


def _group_rms_scale(x, pair_ref):
    sq = x * x
    hi = sq.astype(BF16)
    lo = (sq - hi.astype(F32)).astype(BF16)
    ss = jnp.dot(jnp.concatenate([hi, lo], axis=1), pair_ref[...], preferred_element_type=F32)
    return lax.rsqrt(ss * (1.0 / SWA_HEAD_DIM) + NORM_EPS)


def _swa_kv_kernel(*refs, normalize, rope, emit, pad_blocks, layer):
    it = iter(refs)
    k_ref, v_ref = next(it), next(it)
    if rope:
        kp_ref, c_ref, s_ref = next(it), next(it), next(it)
    if normalize:
        g_ref, gp_ref, pair_ref = next(it), next(it), next(it)
    ko_ref, vo_ref = next(it), next(it)
    if emit:
        kn_ref, vn_ref = next(it), next(it)
    lane = lax.broadcasted_iota(jnp.int32, (SWA_TR, 128), 1)
    low = lane < 64

    def halves(x, c, o_ref):
        sw = pltpu.roll(x, 64, axis=1)
        zero = jnp.zeros_like(x)
        o_ref[4 * c + 0] = jnp.where(low, x, zero).astype(BF16)
        o_ref[4 * c + 1] = jnp.where(low, zero, sw).astype(BF16)
        o_ref[4 * c + 2] = jnp.where(low, sw, zero).astype(BF16)
        o_ref[4 * c + 3] = jnp.where(low, zero, x).astype(BF16)

    def compute():
        for c in range(2):
            sl = slice(128 * c, 128 * c + 128)
            kb = k_ref[:, sl]
            if normalize:
                r = _group_rms_scale(kb, pair_ref)
                kn = (kb * r) * g_ref[...]
                if emit:
                    kn_ref[:, sl] = kn
                if rope:
                    kn = kn * c_ref[...] + ((kp_ref[:, sl] * r) * gp_ref[...]) * s_ref[...]
            else:
                kn = kb
            halves(kn, c, ko_ref)
            vb = v_ref[:, sl]
            if emit:
                vn_ref[:, sl] = vb
            halves(vb, c, vo_ref)

    if pad_blocks:
        t = pl.program_id(1)
        is_pad = jnp.logical_or(t == 0, t == pl.num_programs(1) - 1)

        @pl.when(is_pad)
        def _():
            ko_ref[...] = jnp.zeros_like(ko_ref)
            vo_ref[...] = jnp.zeros_like(vo_ref)

        pl.when(jnp.logical_not(is_pad))(compute)
    else:
        compute()


def _swa_kv(k_src, k_spec, v_src, v_spec, rope_ins, norm_ins, batch, rows, layer, emit, pad_blocks):
    rope = rope_ins is not None
    normalize = norm_ins is not None
    nt = rows // SWA_TR + (2 if pad_blocks else 0)
    ins, specs = [k_src, v_src], [k_spec, v_spec]
    if rope:
        kp_src, kp_spec, ctab, stab = rope_ins
        tab_row = (lambda t: jnp.clip(t - 1, 0, nt - 3)) if pad_blocks else (lambda t: t)
        tab_spec = pl.BlockSpec((SWA_TR, 128), lambda b, t: (tab_row(t), 0))
        ins += [kp_src, ctab, stab]
        specs += [kp_spec, tab_spec, tab_spec]
    if normalize:
        ins += list(norm_ins)
        specs += [pl.BlockSpec((1, 128), lambda b, t: (0, 0)), pl.BlockSpec((1, 128), lambda b, t: (0, 0)),
                  pl.BlockSpec((256, 128), lambda b, t: (0, 0))]
    out_shape = [jax.ShapeDtypeStruct((batch, 8, nt * SWA_TR, 128), BF16)] * 2
    out_specs = [pl.BlockSpec((None, 8, SWA_TR, 128), lambda b, t: (b, 0, t, 0))] * 2
    if emit:
        out_shape += [jax.ShapeDtypeStruct((batch, rows, 256), F32)] * 2
        out_specs += [pl.BlockSpec((None, SWA_TR, 256), lambda b, t: (b, t, 0))] * 2
    return pl.pallas_call(
        functools.partial(_swa_kv_kernel, normalize=normalize, rope=rope, emit=emit, pad_blocks=pad_blocks, layer=layer),
        grid=(batch, nt), in_specs=specs, out_specs=out_specs, out_shape=out_shape,
        compiler_params=pltpu.CompilerParams(dimension_semantics=("arbitrary", "arbitrary")),
        name="swa_kv",
    )(*ins)


def _swa_attn_kernel(*refs, rope, windowed, has_ctx, t_total, layer):
    it = iter(refs)
    sink_ref, q_ref = next(it), next(it)
    if rope:
        qp_ref, c_ref, s_ref = next(it), next(it), next(it)
    g_ref, gp_ref, pair_ref = next(it), next(it), next(it)
    k_ref, v_ref = next(it), next(it)
    if has_ctx:
        kc_ref, vc_ref = next(it), next(it)
    o_ref = next(it)
    n = pl.program_id(1)
    if windowed:
        start = pl.multiple_of(n * SWA_TR, SWA_TR)
        win = pl.ds(start, 3 * SWA_TR)
        r_i = lax.broadcasted_iota(jnp.int32, (2 * SWA_TR, 3 * SWA_TR), 0) & (SWA_TR - 1)
        c_i = lax.broadcasted_iota(jnp.int32, (2 * SWA_TR, 3 * SWA_TR), 1)
        kpos = (n - 1) * SWA_TR + c_i
        diff = SWA_TR + r_i - c_i
        valid2 = (kpos >= 0) & (kpos < t_total) & (diff <= SWA_WINDOW) & (diff >= -SWA_WINDOW)
    else:
        win = slice(None)
    def normed_queries(cb):
        sl = slice(128 * cb, 128 * cb + 128)
        qb = q_ref[:, sl]
        r = _group_rms_scale(qb, pair_ref)
        qn = (qb * r) * g_ref[...]
        if rope:
            qn = qn * c_ref[...] + ((qp_ref[:, sl] * r) * gp_ref[...]) * s_ref[...]
        return (qn * (SWA_SCALE * LOG2_E)).astype(BF16)

    low_lanes = lax.broadcasted_iota(jnp.int32, (2 * SWA_TR, 128), 1) < SWA_HEAD_DIM

    lk = 3 * SWA_TR if windowed else k_ref.shape[1]
    lc = kc_ref.shape[1] if has_ctx else 0
    row_blk = lax.broadcasted_iota(jnp.int32, (2 * SWA_TR, 1), 0) // SWA_TR
    for j in range(SWA_KV_HEADS):
        q2 = jnp.concatenate([normed_queries(2 * j), normed_queries(2 * j + 1)], axis=0)
        k_parts, v_parts = [], []
        for e in range(2):
            k_parts.append(k_ref[2 * j + e, win, :])
            v_parts.append(v_ref[2 * j + e, win, :])
            if has_ctx:
                k_parts.append(kc_ref[2 * j + e])
                v_parts.append(vc_ref[2 * j + e])
        s = lax.dot_general(q2, jnp.concatenate(k_parts, axis=0), NT_DIMS, preferred_element_type=F32)
        p_parts, inv = [], []
        for e in range(2):
            off = e * (lk + lc)
            sink = jnp.where(row_blk == 0, sink_ref[layer, 4 * j + e], sink_ref[layer, 4 * j + 2 + e]) * LOG2_E
            s_loc = s[:, off:off + lk]
            if windowed:
                s_loc = jnp.where(valid2, s_loc, SWA_NEG)
            m = jnp.maximum(jnp.max(s_loc, axis=-1, keepdims=True), sink)
            if has_ctx:
                s_ctx = s[:, off + lk:off + lk + lc]
                m = jnp.maximum(m, jnp.max(s_ctx, axis=-1, keepdims=True))
            p = jnp.exp2(s_loc - m)
            den = jnp.sum(p, axis=-1, keepdims=True) + jnp.exp2(sink - m)
            p_parts.append(p.astype(BF16))
            if has_ctx:
                pc = jnp.exp2(s_ctx - m)
                den = den + jnp.sum(pc, axis=-1, keepdims=True)
                p_parts.append(pc.astype(BF16))
            inv.append(1.0 / den)
        o = jnp.dot(jnp.concatenate(p_parts, axis=1), jnp.concatenate(v_parts, axis=0), preferred_element_type=F32)
        o = o * jnp.where(low_lanes, inv[0], inv[1])
        o_ref[:, 256 * j:256 * j + 128] = o[:SWA_TR].astype(BF16)
        o_ref[:, 256 * j + 128:256 * j + 256] = o[SWA_TR:].astype(BF16)


def _swa_attn(sink, proj, rope_ins, norm_ins, k, v, ctx_kv, batch, t, layer):
    rope = rope_ins is not None
    has_ctx = ctx_kv is not None
    nq = t // SWA_TR
    ins = [sink, proj]
    specs = [pl.BlockSpec(memory_space=pltpu.SMEM),
             pl.BlockSpec((SWA_TR, SWA_WIDTH), lambda b, i: (b * nq + i, COL_SQ // SWA_WIDTH))]
    if rope:
        ctab, stab = rope_ins
        tab_spec = pl.BlockSpec((SWA_TR, 128), lambda b, i: (i, 0))
        ins += [proj, ctab, stab]
        specs += [pl.BlockSpec((SWA_TR, SWA_WIDTH), lambda b, i: (b * nq + i, COL_SQ_PERM // SWA_WIDTH)), tab_spec, tab_spec]
    ins += list(norm_ins)
    specs += [pl.BlockSpec((1, 128), lambda b, i: (0, 0)), pl.BlockSpec((1, 128), lambda b, i: (0, 0)),
              pl.BlockSpec((256, 128), lambda b, i: (0, 0))]
    lk = k.shape[2]
    ins += [k, v]
    specs += [pl.BlockSpec((None, 8, lk, 128), lambda b, i: (b, 0, 0, 0))] * 2
    if has_ctx:
        lc = ctx_kv[0].shape[2]
        ins += list(ctx_kv)
        specs += [pl.BlockSpec((None, 8, lc, 128), lambda b, i: (b, 0, 0, 0))] * 2
    return pl.pallas_call(
        functools.partial(_swa_attn_kernel, rope=rope, windowed=has_ctx, has_ctx=has_ctx, t_total=t, layer=layer),
        grid=(batch, nq), in_specs=specs,
        out_specs=pl.BlockSpec((SWA_TR, SWA_WIDTH), lambda b, i: (b * nq + i, 0)),
        out_shape=jax.ShapeDtypeStruct((batch * t, SWA_WIDTH), BF16),
        compiler_params=pltpu.CompilerParams(dimension_semantics=("arbitrary", "arbitrary")),
        name="swa_attn",
    )(*ins)


def _swa_branch(proj, batch, t, layer, g_qn_l, g_kn_l, sink, rope, ctx):
    perm, _ = _rope_partner()
    pair = _pair_sum_matrix()
    tile2 = lambda g: jnp.concatenate([g, g])[None]
    nt = t // SWA_TR
    if rope is None:
        k_spec = pl.BlockSpec((SWA_TR, 256), lambda b, i: (b * nt + i, COL_SK // 256))
        v_spec = pl.BlockSpec((SWA_TR, 256), lambda b, i: (b * nt + i, COL_SV // 256))
        k, v, k_n, v_raw = _swa_kv(proj, k_spec, proj, v_spec, None, (tile2(g_kn_l), tile2(g_kn_l[perm]), pair),
                                   batch, t, layer, emit=True, pad_blocks=False)
        o = _swa_attn(sink, proj, None, (tile2(g_qn_l), tile2(g_qn_l[perm]), pair), k, v, None, batch, t, layer)
        return o, k_n, v_raw
    c64, s64 = rope
    ctab, stab = jnp.concatenate([c64, c64], axis=1), jnp.concatenate([s64, s64], axis=1)
    row = lambda b, i: b * nt + jnp.clip(i - 1, 0, nt - 1)
    k_spec = pl.BlockSpec((SWA_TR, 256), lambda b, i: (row(b, i), COL_SK // 256))
    v_spec = pl.BlockSpec((SWA_TR, 256), lambda b, i: (row(b, i), COL_SV // 256))
    kp_spec = pl.BlockSpec((SWA_TR, 256), lambda b, i: (row(b, i), COL_SK_PERM // 256))
    k, v = _swa_kv(proj, k_spec, proj, v_spec, (proj, kp_spec, ctab, stab), (tile2(g_kn_l), tile2(g_kn_l[perm]), pair),
                   batch, t, layer, emit=False, pad_blocks=True)
    ctx_k, ctx_v = ctx
    lc = ctx_k.shape[2]
    c_spec = pl.BlockSpec((None, None, SWA_TR, 256), lambda b, i: (b, layer, i, 0))
    kc, vc = _swa_kv(ctx_k, c_spec, ctx_v, c_spec, None, None, batch, lc, layer, emit=False, pad_blocks=False)
    o = _swa_attn(sink, proj, (ctab, stab), (tile2(g_qn_l), tile2(g_qn_l[perm]), pair), k, v, (kc, vc), batch, t, layer)
    return o, None, None


GLA_LEVELS = (64, 32, 16, 8, 4, 2, 1)
GLA_ROWSETS = 2 + len(GLA_LEVELS)


def _gla_sum_matrix(backward):
    c = GLA_CHUNK
    t = np.arange(c)[:, None]
    j = np.arange(c)[None, :]
    sets = [j <= t, j > t]
    for g in GLA_LEVELS:
        e = (t // (2 * g)) * 2 * g + g - 1
        upper = (t // g) % 2 == 1
        sets.append(np.where(upper, (j > e) & (j <= t), (j > t) & (j <= e)))
    n = np.concatenate(sets, axis=0).astype(np.float32)
    if backward:
        n = n.reshape(GLA_ROWSETS, c, c)[:, ::-1, ::-1].reshape(GLA_ROWSETS * c, c)
    return jnp.asarray(np.concatenate([n, n, n], axis=1), BF16)


def _gla_kernel(*refs, has_s0):
    it = iter(refs)
    srcs = [tuple(next(it) for _ in range(4)) for _ in range(2)]
    nmat = (next(it), next(it))
    wg = (next(it), next(it))
    bg = (next(it), next(it))
    s0_ref = next(it) if has_s0 else None
    o_refs = (next(it), next(it))
    sfin_ref = next(it)
    s_ref = next(it)
    i = pl.program_id(1)
    c = GLA_CHUNK

    @pl.when(i == 0)
    def _():
        s_ref[...] = s0_ref[...] if has_s0 else jnp.zeros_like(s_ref)

    row = lax.broadcasted_iota(jnp.int32, (c, c), 0)
    col = lax.broadcasted_iota(jnp.int32, (c, c), 1)
    diag = row == col
    log2 = lambda g: int(g).bit_length() - 1
    odd_half = [((row >> log2(g)) & 1) == 1 for g in GLA_LEVELS]
    same_block = [(row >> (log2(g) + 1)) == (col >> (log2(g) + 1)) for g in GLA_LEVELS]
    for d in range(2):
        q_ref, k_ref, v_ref, gg_ref = srcs[d]
        z = jnp.dot(gg_ref[...].astype(BF16), wg[d][...], preferred_element_type=F32) + bg[d][...]
        la = (jnp.minimum(z, 0.0) - jnp.log1p(jnp.exp(-jnp.abs(z)))) * (1.0 / GLA_GATE_NORM)
        hi = la.astype(BF16)
        r1 = la - hi.astype(F32)
        mid = r1.astype(BF16)
        lo = (r1 - mid.astype(F32)).astype(BF16)
        ex_all = jnp.dot(nmat[d][...], jnp.concatenate([hi, mid, lo], axis=0), preferred_element_type=F32)
        for h in range(GLA_HEADS):
            sl = slice(GLA_DK * h, GLA_DK * h + GLA_DK)
            vs = slice(GLA_DV * h, GLA_DV * h + GLA_DV)
            ex = ex_all[:, sl]
            q = q_ref[:, sl] * (GLA_DK ** -0.5)
            k = k_ref[:, sl]
            v = v_ref[:, vs].astype(BF16)
            eb = jnp.exp(ex[0:c])
            ek = jnp.exp(ex[c:2 * c])
            s_old = s_ref[d, h]
            o = jnp.dot((q * eb).astype(BF16), s_old.astype(BF16), preferred_element_type=F32)
            a = jnp.where(diag, jnp.sum(q * k, axis=-1, keepdims=True), 0.0)
            for li, g in enumerate(GLA_LEVELS):
                eg = jnp.exp(ex[(2 + li) * c:(3 + li) * c])
                qe, ke = q * eg, k * eg
                qg = (jnp.where(odd_half[li], qe, 0.0) if d == 0 else jnp.where(odd_half[li], 0.0, qe)).astype(BF16)
                kg = (jnp.where(odd_half[li], 0.0, ke) if d == 0 else jnp.where(odd_half[li], ke, 0.0)).astype(BF16)
                ag = lax.dot_general(qg, kg, NT_DIMS, preferred_element_type=F32)
                if 2 * g < c:
                    ag = jnp.where(same_block[li], ag, 0.0)
                a = a + ag
            o = o + jnp.dot(a.astype(BF16), v, preferred_element_type=F32)
            o_refs[d][:, vs] = o
            a_col = eb.T[:, c - 1:c] if d == 0 else eb.T[:, 0:1]
            kt = (k * ek).T.astype(BF16)
            s_ref[d, h] = a_col * s_old + jnp.dot(kt, v, preferred_element_type=F32)

    @pl.when(i == pl.num_programs(1) - 1)
    def _():
        sfin_ref[...] = s_ref[...]


def _gla(proj, batch, t, layer, w_gf, b_gf, w_gb, b_gb, s0):
    c = GLA_CHUNK
    n = t // c
    fwd = lambda b, i: b * n + i
    bwd = lambda b, i: b * n + (n - 1 - i)
    ins, specs = [], []
    for rowf in (fwd, bwd):
        ins += [proj] * 4
        specs += [pl.BlockSpec((c, 512), lambda b, i, rowf=rowf: (rowf(b, i), COL_GQ // 512)),
                  pl.BlockSpec((c, 512), lambda b, i, rowf=rowf: (rowf(b, i), COL_GK // 512)),
                  pl.BlockSpec((c, GLA_WIDTH), lambda b, i, rowf=rowf: (rowf(b, i), COL_GV // GLA_WIDTH)),
                  pl.BlockSpec((c, 128), lambda b, i, rowf=rowf: (rowf(b, i), COL_GG // 128))]
    const = lambda shape: pl.BlockSpec(shape, lambda b, i: (0,) * len(shape))
    pad_w = lambda w, off: jnp.zeros((128, 512), F32).at[off:off + GLA_GATE_RANK].set(w).astype(BF16)
    ins += [_gla_sum_matrix(False), _gla_sum_matrix(True), pad_w(w_gf, 0), pad_w(w_gb, GLA_GATE_RANK), b_gf[None], b_gb[None]]
    specs += [const((GLA_ROWSETS * c, 3 * c))] * 2 + [const((128, 512))] * 2 + [const((1, 512))] * 2
    state_spec = pl.BlockSpec((None, 2, GLA_HEADS, GLA_DK, GLA_DV), lambda b, i: (b, 0, 0, 0, 0))
    if s0 is not None:
        ins.append(s0)
        specs.append(pl.BlockSpec((None, None, 2, GLA_HEADS, GLA_DK, GLA_DV), lambda b, i: (b, layer, 0, 0, 0, 0)))
    return pl.pallas_call(
        functools.partial(_gla_kernel, has_s0=s0 is not None),
        grid=(batch, n), in_specs=specs,
        out_specs=[pl.BlockSpec((c, GLA_WIDTH), lambda b, i: (fwd(b, i), 0)),
                   pl.BlockSpec((c, GLA_WIDTH), lambda b, i: (bwd(b, i), 0)), state_spec],
        out_shape=[jax.ShapeDtypeStruct((batch * t, GLA_WIDTH), F32)] * 2
        + [jax.ShapeDtypeStruct((batch, 2, GLA_HEADS, GLA_DK, GLA_DV), F32)],
        scratch_shapes=[pltpu.VMEM((2, GLA_HEADS, GLA_DK, GLA_DV), F32)],
        compiler_params=pltpu.CompilerParams(dimension_semantics=("arbitrary", "arbitrary")),
        name="gla",
    )(*ins)


GOUT_TM = 512


def _gla_out_kernel(of_ref, ob_ref, gate_ref, g_ref, o_ref, *, layer):
    g = g_ref[pl.ds(layer, 1), :]
    for h in range(GLA_HEADS):
        vs = slice(GLA_DV * h, GLA_DV * h + GLA_DV)
        o = of_ref[:, vs] + ob_ref[:, vs]
        y = (o * lax.rsqrt(jnp.mean(o * o, axis=-1, keepdims=True) + NORM_EPS)) * g
        gate = gate_ref[:, vs]
        o_ref[:, vs] = (y * (gate * jax.nn.sigmoid(gate))).astype(BF16)


def _gla_out(o_f, o_b, proj, g_gla_out, layer):
    m = o_f.shape[0]
    return pl.pallas_call(
        functools.partial(_gla_out_kernel, layer=layer),
        grid=(m // GOUT_TM,),
        in_specs=[pl.BlockSpec((GOUT_TM, GLA_WIDTH), lambda i: (i, 0)),
                  pl.BlockSpec((GOUT_TM, GLA_WIDTH), lambda i: (i, 0)),
                  pl.BlockSpec((GOUT_TM, GLA_WIDTH), lambda i: (i, COL_GOUT // GLA_WIDTH)),
                  pl.BlockSpec((DEPTH, GLA_DV), lambda i: (0, 0))],
        out_specs=pl.BlockSpec((GOUT_TM, GLA_WIDTH), lambda i: (i, 0)),
        out_shape=jax.ShapeDtypeStruct((m, GLA_WIDTH), BF16),
        compiler_params=pltpu.CompilerParams(dimension_semantics=("arbitrary",)),
        name="gla_out",
    )(o_f, o_b, proj, g_gla_out)


MERGE_TM = 1024
MERGE_TN = 512


def _merge_kernel(om_ref, og_ref, os_ref, gm_ref, gg_ref, gs_ref, wm_ref, wg_ref, ws_ref, o_ref):
    def branch(o_r, gate_r, w_r):
        y = jnp.dot(o_r[...], w_r[...].astype(BF16), preferred_element_type=F32)
        return jax.nn.sigmoid(gate_r[...]) * y

    merged = branch(om_ref, gm_ref, wm_ref) + branch(og_ref, gg_ref, wg_ref) + branch(os_ref, gs_ref, ws_ref)
    o_ref[...] = merged.astype(BF16)


def _merge(o_mla, o_gla, o_swa, proj, w_br_mla, w_br_gla, w_br_swa, layer):
    m = o_mla.shape[0]
    nn = D_MODEL // MERGE_TN
    o_spec = pl.BlockSpec((MERGE_TM, 1024), lambda i, j: (i, 0))
    gate_spec = lambda col: pl.BlockSpec((MERGE_TM, MERGE_TN), lambda i, j: (i, col // MERGE_TN + j))
    w_spec = pl.BlockSpec((None, 1024, MERGE_TN), lambda i, j: (layer, 0, j))
    est = 2 * (3 * MERGE_TM * 1024 * 2 + 3 * MERGE_TM * MERGE_TN * 4 + 3 * 1024 * MERGE_TN * 4 + MERGE_TM * MERGE_TN * 2)
    return pl.pallas_call(
        _merge_kernel,
        grid=(m // MERGE_TM, nn),
        in_specs=[o_spec, o_spec, o_spec, gate_spec(COL_GATE_MLA), gate_spec(COL_GATE_GLA), gate_spec(COL_GATE_SWA),
                  w_spec, w_spec, w_spec],
        out_specs=pl.BlockSpec((MERGE_TM, MERGE_TN), lambda i, j: (i, j)),
        out_shape=jax.ShapeDtypeStruct((m, D_MODEL), BF16),
        compiler_params=pltpu.CompilerParams(
            dimension_semantics=("arbitrary", "arbitrary"), vmem_limit_bytes=_vmem_limit(est)),
        name="merge",
    )(o_mla, o_gla, o_swa, proj, proj, proj, w_br_mla, w_br_gla, w_br_swa)


OUT_TM = 1024
OUT_TN = 512


def _out_proj_kernel(m_ref, w_ref, x_ref, gate_ref, o_ref, *, base_row, rows_per_cond):
    row = _cond_row(pl.program_id(0), m_ref.shape[0], base_row, rows_per_cond)
    y = jnp.dot(m_ref[...], w_ref[...].astype(BF16), preferred_element_type=F32)
    o_ref[...] = x_ref[...] + gate_ref[pl.ds(row, 1), :] * y


def _out_proj(merged, x, mods, w_out, layer, base_row, rows_per_cond):
    m = x.shape[0]
    est = 2 * (OUT_TM * D_MODEL * 2 + D_MODEL * OUT_TN * 4 + 2 * OUT_TM * OUT_TN * 4)
    return pl.pallas_call(
        functools.partial(_out_proj_kernel, base_row=base_row, rows_per_cond=rows_per_cond),
        grid=(m // OUT_TM, D_MODEL // OUT_TN),
        in_specs=[pl.BlockSpec((OUT_TM, D_MODEL), lambda i, j: (i, 0)),
                  pl.BlockSpec((None, D_MODEL, OUT_TN), lambda i, j: (layer, 0, j)),
                  pl.BlockSpec((OUT_TM, OUT_TN), lambda i, j: (i, j)),
                  pl.BlockSpec((None, None, 8, OUT_TN), lambda i, j: (layer, 5, 0, j))],
        out_specs=pl.BlockSpec((OUT_TM, OUT_TN), lambda i, j: (i, j)),
        out_shape=jax.ShapeDtypeStruct((m, D_MODEL), F32),
        compiler_params=pltpu.CompilerParams(
            dimension_semantics=("arbitrary", "arbitrary"), vmem_limit_bytes=_vmem_limit(est)),
        name="out_proj",
    )(merged, w_out, x, mods)


def _trunk_layer(x, mods, layer, w, group):
    batch, t, base_row, rows_per_cond, rope, ctx = group
    x = _ffn(x, mods, layer, 0, w['g_norm1'], w['w_ff1_gu'], w['w_ff1_down'], base_row, rows_per_cond)
    ncols = PROJ_COLS_PROMPT if rope is None else PROJ_COLS_SAMPLE
    proj = _in_proj(x, mods, layer, w['g_norm2'], w['w_in_packed'], ncols, base_row, rows_per_cond)
    mla_ctx = swa_ctx = s0 = None
    if ctx is not None:
        mla_ctx, swa_ctx, s0 = ctx[:2], ctx[2:4], ctx[4]
    o_mla, ckv_n = _mla_branch(proj, batch, t, layer, w['mla_packed'][layer], w['w_mla_ukv'], w['g_mla_q'], w['g_mla_kv'],
                               rope, mla_ctx)
    o_swa, k_n, v_raw = _swa_branch(proj, batch, t, layer, w['g_swa_qn'][layer], w['g_swa_kn'][layer], w['swa_sink'],
                                    rope, swa_ctx)
    o_f, o_b, s_fin = _gla(proj, batch, t, layer, w['w_gla_gf'][layer], w['b_gla_gf'][layer],
                           w['w_gla_gb'][layer], w['b_gla_gb'][layer], s0)
    o_gla = _gla_out(o_f, o_b, proj, w['g_gla_out'], layer)
    merged = _merge(o_mla, o_gla, o_swa, proj, w['w_br_mla'], w['w_br_gla'], w['w_br_swa'], layer)
    x = _out_proj(merged, x, mods, w['w_out'], layer, base_row, rows_per_cond)
    x = _ffn(x, mods, layer, 6, w['g_norm3'], w['w_ff2_gu'], w['w_ff2_down'], base_row, rows_per_cond)
    new_ctx = None
    if ctx is None:
        kpe = proj[:, COL_KPE:COL_KPE + MLA_ROPE].reshape(batch, t, MLA_ROPE)
        new_ctx = (ckv_n, kpe, k_n.reshape(batch, t, SWA_KV_HEADS, SWA_HEAD_DIM),
                   v_raw.reshape(batch, t, SWA_KV_HEADS, SWA_HEAD_DIM), s_fin)
    return x, new_ctx


def kernel(x_prompt, x_sample, cache_mla_ckv, cache_mla_kpe, cache_swa_k, cache_swa_v, state_gla,
           c, c_ctx, w_ada, b_ada, g_norm1, g_norm2, g_norm3,
           w_ff1_gu, w_ff1_down, w_ff2_gu, w_ff2_down, w_in,
           g_mla_q, w_mla_uq, g_mla_kv, w_mla_ukv, g_mla_qn, g_mla_kn,
           w_gla_gf, b_gla_gf, w_gla_gb, b_gla_gb, g_gla_out,
           g_swa_qn, g_swa_kn, swa_sink, w_br_mla, w_br_gla, w_br_swa, w_out):
    bp, tp, _ = x_prompt.shape
    bs, ts, _ = x_sample.shape
    assert bs + 1 <= 8, "conditioning rows are packed into one 8-row tile"
    cond8 = jnp.zeros((8, D_MODEL), F32).at[0].set(c_ctx).at[1:1 + bs].set(c)
    mods = _adaln(cond8, w_ada, b_ada)
    w = dict(g_norm1=g_norm1, g_norm2=g_norm2, g_norm3=g_norm3,
             w_ff1_gu=w_ff1_gu, w_ff1_down=w_ff1_down, w_ff2_gu=w_ff2_gu, w_ff2_down=w_ff2_down,
             w_in_packed=_pack_w_in(w_in),
             mla_packed=[_pack_mla(w_mla_uq[l], g_mla_qn[l], g_mla_kn[l]) for l in range(DEPTH)],
             w_mla_ukv=w_mla_ukv, g_mla_q=g_mla_q, g_mla_kv=g_mla_kv,
             g_swa_qn=g_swa_qn, g_swa_kn=g_swa_kn, swa_sink=swa_sink,
             w_gla_gf=w_gla_gf, b_gla_gf=b_gla_gf, w_gla_gb=w_gla_gb, b_gla_gb=b_gla_gb, g_gla_out=g_gla_out,
             w_br_mla=w_br_mla, w_br_gla=w_br_gla, w_br_swa=w_br_swa, w_out=w_out)
    past = cache_mla_ckv.shape[2]
    ctx = (cache_mla_ckv, jnp.pad(cache_mla_kpe, ((0, 0), (0, 0), (0, 0), (0, 128 - MLA_ROPE))),
           cache_swa_k.reshape(bs, DEPTH, past, SWA_KV_HEADS * SWA_HEAD_DIM),
           cache_swa_v.reshape(bs, DEPTH, past, SWA_KV_HEADS * SWA_HEAD_DIM), state_gla)
    group_p = (bp, tp, 0, bp * tp, None, None)
    group_s = (bs, ts, 1, ts, _rope_tables(ts), ctx)
    y_p = x_prompt.reshape(bp * tp, D_MODEL)
    y_s = x_sample.reshape(bs * ts, D_MODEL)
    new = []
    for l in range(DEPTH):
        y_p, new_ctx = _trunk_layer(y_p, mods, l, w, group_p)
        new.append(new_ctx)
        y_s, _ = _trunk_layer(y_s, mods, l, w, group_s)
    stacked = tuple(jnp.stack([new[l][k] for l in range(DEPTH)], axis=1) for k in range(5))
    return (y_p.reshape(bp, tp, D_MODEL), y_s.reshape(bs, ts, D_MODEL)) + stacked
```

```python
import functools

import numpy as np
import jax
import jax.numpy as jnp
from jax import lax
from jax.experimental import pallas as pl
from jax.experimental.pallas import tpu as pltpu

F32 = jnp.float32
BF16 = jnp.bfloat16

D_MODEL = 2048
DEPTH = 2
GRID_W = 64
ROPE_BASE = 10000.0
NORM_EPS = 1e-6
MLA_HEADS = 8
MLA_Q_LORA = 512
MLA_KV_LORA = 512
MLA_NOPE = 128
MLA_ROPE = 64
MLA_V = 128
MLA_QK = MLA_NOPE + MLA_ROPE
MLA_SCALE = MLA_QK ** -0.5
MLA_WIDTH = MLA_HEADS * MLA_V
MLA_HEAD_PAD = 256
GLA_HEADS = 4
GLA_DK = 128
GLA_DV = 256
GLA_GATE_RANK = 16
GLA_GATE_NORM = 16.0
GLA_WIDTH = GLA_HEADS * GLA_DV
GLA_CHUNK = 128
SWA_HEADS = 16
SWA_KV_HEADS = 4
SWA_HEAD_DIM = 64
SWA_WINDOW = 128
SWA_SCALE = SWA_HEAD_DIM ** -0.5
SWA_WIDTH = SWA_HEADS * SWA_HEAD_DIM
D_FF = 5632
N_MOD = 9
IN_SPLITS = (MLA_Q_LORA, MLA_KV_LORA, MLA_ROPE,
             GLA_HEADS * GLA_DK, GLA_HEADS * GLA_DK, GLA_WIDTH, GLA_WIDTH, GLA_GATE_RANK, GLA_GATE_RANK,
             SWA_WIDTH, SWA_KV_HEADS * SWA_HEAD_DIM, SWA_KV_HEADS * SWA_HEAD_DIM,
             D_MODEL, D_MODEL, D_MODEL)

V7X_LANES = 128
V7X_SUBLANES = 8
V7X_VMEM_BYTES = 64 * 1024 * 1024
MIB = 1024 * 1024

NT_DIMS = (((1,), (1,)), ((), ()))
LOG2_E = 1.4426950408889634

COL_GATE_MLA = 0
COL_GATE_GLA = 2048
COL_GATE_SWA = 4096
COL_CQ = 6144
COL_CKV = 6656
COL_GQ = 7168
COL_GK = 7680
COL_GV = 8192
COL_GOUT = 9216
COL_SQ = 10240
COL_SK = 11264
COL_SV = 11520
COL_KPE = 11776
COL_GG = 11904
PROJ_COLS_PROMPT = 12288
COL_SQ_PERM = 12288
COL_SK_PERM = 13312
COL_KPE_PERM = 13568
PROJ_COLS_SAMPLE = 14336
PROJ_TN = 1024


def _vmem_limit(nbytes):
    return int(min(nbytes + 12 * MIB, V7X_VMEM_BYTES - 6 * MIB))


def _rope_partner():
    i = np.arange(64)
    within = i % 32
    perm = (i // 32) * 32 + (within + 16) % 32
    sign = np.where(within < 16, -1.0, 1.0).astype(np.float32)
    return perm, sign


ADA_TN = 1024


def _adaln_kernel(cond_ref, w_ref, b_ref, o_ref):
    c = cond_ref[...]
    s = (c * jax.nn.sigmoid(c)).astype(BF16)
    o_ref[...] = jnp.dot(s, w_ref[...].astype(BF16), preferred_element_type=F32) + b_ref[...]


def _adaln(cond8, w_ada, b_ada):
    halves = D_MODEL // ADA_TN
    nj = N_MOD * halves
    return pl.pallas_call(
        _adaln_kernel,
        grid=(DEPTH, nj),
        in_specs=[
            pl.BlockSpec((8, D_MODEL), lambda l, j: (0, 0)),
            pl.BlockSpec((None, D_MODEL, ADA_TN), lambda l, j: (l, 0, j)),
            pl.BlockSpec((None, 1, ADA_TN), lambda l, j: (l, 0, j)),
        ],
        out_specs=pl.BlockSpec((None, None, 8, ADA_TN), lambda l, j: (l, j // halves, 0, j % halves)),
        out_shape=jax.ShapeDtypeStruct((DEPTH, N_MOD, 8, D_MODEL), F32),
        compiler_params=pltpu.CompilerParams(
            dimension_semantics=("arbitrary", "arbitrary"),
            vmem_limit_bytes=_vmem_limit(2 * D_MODEL * ADA_TN * 4)),
        name="adaln",
    )(cond8, w_ada, b_ada.reshape(DEPTH, 1, N_MOD * D_MODEL))


NORM_ROWS = 64


def _modulated_norm_to(h_ref, x_ref, g, shift, scale):
    gain = g * (1.0 + scale)

    def body(c, carry):
        r0 = pl.multiple_of(c * NORM_ROWS, NORM_ROWS)
        x = x_ref[pl.ds(r0, NORM_ROWS), :]
        ms = jnp.mean(x * x, axis=-1, keepdims=True)
        h_ref[pl.ds(r0, NORM_ROWS), :] = ((x * lax.rsqrt(ms + NORM_EPS)) * gain + shift).astype(BF16)
        return carry

    lax.fori_loop(0, x_ref.shape[0] // NORM_ROWS, body, 0, unroll=2)


def _cond_row(i, tm, base_row, rows_per_cond):
    return base_row + (i * tm) // rows_per_cond


FFN_TM = 1024
FFN_TF = 256
FFN_TN = 512


def _ffn_kernel(x_ref, shift_ref, scale_ref, gate_ref, g_ref, wg_ref, wu_ref, wd_ref, o_ref, h_ref,
                *, layer, base_row, rows_per_cond):
    i = pl.program_id(0)
    j = pl.program_id(1)
    row = _cond_row(i, x_ref.shape[0], base_row, rows_per_cond)

    @pl.when(j == 0)
    def _():
        _modulated_norm_to(h_ref, x_ref, g_ref[pl.ds(layer, 1), :],
                           shift_ref[pl.ds(row, 1), :], scale_ref[pl.ds(row, 1), :])
        o_ref[...] = jnp.zeros_like(o_ref)

    h = h_ref[...]
    a = jnp.dot(h, wg_ref[...].astype(BF16), preferred_element_type=F32)
    u = jnp.dot(h, wu_ref[...].astype(BF16), preferred_element_type=F32)
    act = (a * jax.nn.sigmoid(a) * u).astype(BF16)
    for n in range(0, D_MODEL, FFN_TN):
        o_ref[:, n:n + FFN_TN] += jnp.dot(act, wd_ref[:, n:n + FFN_TN].astype(BF16), preferred_element_type=F32)

    @pl.when(j == pl.num_programs(1) - 1)
    def _():
        o_ref[...] = x_ref[...] + (0.5 * gate_ref[pl.ds(row, 1), :]) * o_ref[...]


def _ffn(x, mods, layer, first_mod, g_norm, w_gu, w_down, base_row, rows_per_cond):
    m = x.shape[0]
    nf = D_FF // FFN_TF
    mod_spec = lambda k: pl.BlockSpec((None, None, 8, D_MODEL), lambda i, j: (layer, first_mod + k, 0, 0))
    est = (2 * FFN_TM * D_MODEL * 4 * 2 + FFN_TM * D_MODEL * 2
           + 2 * 3 * D_MODEL * FFN_TF * w_gu.dtype.itemsize)
    return pl.pallas_call(
        functools.partial(_ffn_kernel, layer=layer, base_row=base_row, rows_per_cond=rows_per_cond),
        grid=(m // FFN_TM, nf),
        in_specs=[
            pl.BlockSpec((FFN_TM, D_MODEL), lambda i, j: (i, 0)),
            mod_spec(0), mod_spec(1), mod_spec(2),
            pl.BlockSpec((DEPTH, D_MODEL), lambda i, j: (0, 0)),
            pl.BlockSpec((None, D_MODEL, FFN_TF), lambda i, j: (layer, 0, j)),
            pl.BlockSpec((None, D_MODEL, FFN_TF), lambda i, j: (layer, 0, j + nf)),
            pl.BlockSpec((None, FFN_TF, D_MODEL), lambda i, j: (layer, j, 0)),
        ],
        out_specs=pl.BlockSpec((FFN_TM, D_MODEL), lambda i, j: (i, 0)),
        out_shape=jax.ShapeDtypeStruct((m, D_MODEL), F32),
        scratch_shapes=[pltpu.VMEM((FFN_TM, D_MODEL), BF16)],
        compiler_params=pltpu.CompilerParams(
            dimension_semantics=("arbitrary", "arbitrary"), vmem_limit_bytes=_vmem_limit(est)),
        name="ffn",
    )(x, mods, mods, mods, g_norm, w_gu, w_gu, w_down)


PROJ_TM = 1024


IN_COLS = sum(IN_SPLITS)
_SRC = dict(zip(('cq', 'ckv', 'kpe', 'gq', 'gk', 'gv', 'gout', 'ggf', 'ggb', 'sq', 'sk', 'sv', 'gate_mla'),
                np.concatenate([[0], np.cumsum(IN_SPLITS)]).tolist()))
PACK_ROWS = 512
PACK_TILES = PROJ_COLS_SAMPLE // PACK_ROWS


def _pack_plan():
    kinds, a, b = [], [], []
    def add(kind, src_a, src_b=0):
        kinds.append(kind); a.append(src_a); b.append(src_b)
    for dst, src, width in ((COL_GATE_MLA, _SRC['gate_mla'], 3 * D_MODEL), (COL_CQ, _SRC['cq'], 1024),
                            (COL_GQ, _SRC['gq'], 3072), (COL_SQ, _SRC['sq'], 1536)):
        assert dst == len(kinds) * PACK_ROWS and width % PACK_ROWS == 0
        for t in range(width // PACK_ROWS):
            add(0, src + t * PACK_ROWS)
    assert COL_KPE == len(kinds) * PACK_ROWS
    add(2, _SRC['kpe'], _SRC['ggf'])
    assert COL_SQ_PERM == len(kinds) * PACK_ROWS
    add(1, _SRC['sq']); add(1, _SRC['sq'] + PACK_ROWS)
    assert COL_SK_PERM == len(kinds) * PACK_ROWS
    add(3, _SRC['sk'], _SRC['kpe'])
    add(4, 0)
    assert len(kinds) == PACK_TILES
    return np.asarray([kinds, a, b], np.int32)


def _pack_kernel(plan_ref, a3_ref, b3_ref, o_ref):
    j = pl.program_id(1)
    kind = plan_ref[0, j]
    a_ref, b_ref = a3_ref.at[0], b3_ref.at[0]

    def swap_partners(src_ref, rows, dst0):
        for r in range(0, rows, 32):
            o_ref[dst0 + r:dst0 + r + 16, :] = src_ref[r + 16:r + 32, :].astype(BF16)
            o_ref[dst0 + r + 16:dst0 + r + 32, :] = src_ref[r:r + 16, :].astype(BF16)

    @pl.when(kind == 0)
    def _():
        o_ref[...] = a_ref[...].astype(BF16)

    @pl.when(kind == 1)
    def _():
        swap_partners(a_ref, PACK_ROWS, 0)

    @pl.when(kind == 2)
    def _():
        o_ref[...] = jnp.zeros_like(o_ref)
        o_ref[0:MLA_ROPE, :] = a_ref[0:MLA_ROPE, :].astype(BF16)
        o_ref[128:128 + 2 * GLA_GATE_RANK, :] = b_ref[0:2 * GLA_GATE_RANK, :].astype(BF16)

    @pl.when(kind == 3)
    def _():
        o_ref[...] = jnp.zeros_like(o_ref)
        swap_partners(a_ref, SWA_KV_HEADS * SWA_HEAD_DIM, 0)
        swap_partners(b_ref, MLA_ROPE, SWA_KV_HEADS * SWA_HEAD_DIM)

    @pl.when(kind == 4)
    def _():
        o_ref[...] = jnp.zeros_like(o_ref)


def _pack_w_in(w_in):
    w_t = jnp.swapaxes(w_in, 1, 2)
    window = lambda which: pl.BlockSpec((pl.Element(1), pl.Element(PACK_ROWS), pl.Element(D_MODEL)),
                                        lambda l, j, plan: (l, pl.multiple_of(plan[which, j], 32), 0))
    return pl.pallas_call(
        _pack_kernel,
        grid_spec=pltpu.PrefetchScalarGridSpec(
            num_scalar_prefetch=1, grid=(DEPTH, PACK_TILES),
            in_specs=[window(1), window(2)],
            out_specs=pl.BlockSpec((None, PACK_ROWS, D_MODEL), lambda l, j, plan: (l, j, 0))),
        out_shape=jax.ShapeDtypeStruct((DEPTH, PROJ_COLS_SAMPLE, D_MODEL), BF16),
        compiler_params=pltpu.CompilerParams(
            dimension_semantics=("arbitrary", "arbitrary"),
            vmem_limit_bytes=_vmem_limit(2 * PACK_ROWS * D_MODEL * (4 + 4 + 2))),
        name="pack_w_in",
    )(jnp.asarray(_pack_plan()), w_t, w_t)


def _proj_kernel(x_ref, shift_ref, scale_ref, g_ref, w_ref, o_ref, h_ref, *, layer, base_row, rows_per_cond):
    i = pl.program_id(0)
    j = pl.program_id(1)
    row = _cond_row(i, x_ref.shape[0], base_row, rows_per_cond)

    @pl.when(j == 0)
    def _():
        _modulated_norm_to(h_ref, x_ref, g_ref[pl.ds(layer, 1), :],
                           shift_ref[pl.ds(row, 1), :], scale_ref[pl.ds(row, 1), :])

    o_ref[...] = lax.dot_general(h_ref[...], w_ref[...], NT_DIMS, preferred_element_type=F32)


def _in_proj(x, mods, layer, g_norm2, w_packed, ncols, base_row, rows_per_cond):
    m = x.shape[0]
    mod_spec = lambda k: pl.BlockSpec((None, None, 8, D_MODEL), lambda i, j: (layer, 3 + k, 0, 0))
    est = 2 * PROJ_TM * D_MODEL * 4 + PROJ_TM * D_MODEL * 2 + 2 * D_MODEL * PROJ_TN * 2 + 2 * PROJ_TM * PROJ_TN * 4
    return pl.pallas_call(
        functools.partial(_proj_kernel, layer=layer, base_row=base_row, rows_per_cond=rows_per_cond),
        grid=(m // PROJ_TM, ncols // PROJ_TN),
        in_specs=[
            pl.BlockSpec((PROJ_TM, D_MODEL), lambda i, j: (i, 0)),
            mod_spec(0), mod_spec(1),
            pl.BlockSpec((DEPTH, D_MODEL), lambda i, j: (0, 0)),
            pl.BlockSpec((None, PROJ_TN, D_MODEL), lambda i, j: (layer, j, 0)),
        ],
        out_specs=pl.BlockSpec((PROJ_TM, PROJ_TN), lambda i, j: (i, j)),
        out_shape=jax.ShapeDtypeStruct((m, ncols), F32),
        scratch_shapes=[pltpu.VMEM((PROJ_TM, D_MODEL), BF16)],
        compiler_params=pltpu.CompilerParams(
            dimension_semantics=("arbitrary", "arbitrary"), vmem_limit_bytes=_vmem_limit(est)),
        name="in_proj",
    )(x, mods, mods, g_norm2, w_packed)


def _rope_tables(t):
    pos = jnp.arange(t)
    inv_freq = ROPE_BASE ** (-jnp.arange(16, dtype=F32) / 16)

    def cs(p):
        ang = p.astype(F32)[:, None] * inv_freq[None, :]
        return jnp.concatenate([jnp.cos(ang)] * 2, axis=1), jnp.concatenate([jnp.sin(ang)] * 2, axis=1)

    cr, sr = cs(pos // GRID_W)
    cc, sc = cs(pos % GRID_W)
    _, sign = _rope_partner()
    return jnp.concatenate([cr, cc], axis=1), jnp.concatenate([sr, sc], axis=1) * sign[None, :]


MLA_TR = 256
MLA_TQ = 256


def _pack_mla(w_uq_l, g_qn_l, g_kn_l):
    perm, _ = _rope_partner()
    w = w_uq_l.reshape(MLA_Q_LORA, MLA_HEADS, MLA_QK)
    nope, rope_w = w[:, :, :MLA_NOPE], w[:, :, MLA_NOPE:]
    z64 = jnp.zeros((MLA_Q_LORA, MLA_HEADS, 64), w.dtype)
    wq = jnp.concatenate([nope, rope_w, z64], axis=-1).reshape(MLA_Q_LORA, -1).astype(BF16)
    wqp = jnp.concatenate([rope_w[:, :, perm], z64], axis=-1).reshape(MLA_Q_LORA, -1).astype(BF16)
    v64 = jnp.zeros((64,), F32)
    gq_full = jnp.concatenate([g_qn_l, v64])[None]
    gq_perm = jnp.concatenate([g_qn_l[MLA_NOPE:][perm], v64])[None]
    gk_n = g_kn_l[:MLA_NOPE][None]
    gk_r = jnp.concatenate([g_kn_l[MLA_NOPE:], v64])[None]
    gk_rp = jnp.concatenate([g_kn_l[MLA_NOPE:][perm], v64])[None]
    return wq, wqp, gq_full, gq_perm, gk_n, gk_r, gk_rp


def _mla_kv_kernel(*refs, layer, normalize, rope, emit_ckv):
    it = iter(refs)
    ckv_ref, kpe_ref = next(it), next(it)
    if rope:
        kpp_ref, c_ref, s_ref = next(it), next(it), next(it)
    w_ref, gkv_ref, gn_ref, gr_ref, grp_ref = next(it), next(it), next(it), next(it), next(it)
    k_ref, v_ref = next(it), next(it)
    ckv = ckv_ref[...]
    if normalize:
        ckv = (ckv * lax.rsqrt(jnp.mean(ckv * ckv, axis=-1, keepdims=True) + NORM_EPS)) * gkv_ref[pl.ds(layer, 1), :]
    if emit_ckv:
        next(it)[...] = ckv
    kv = jnp.dot(ckv.astype(BF16), w_ref[...].astype(BF16), preferred_element_type=F32)
    kpe = kpe_ref[...]
    ss_pe = jnp.sum(kpe * kpe, axis=-1, keepdims=True)
    kr = kpe * gr_ref[...]
    if rope:
        kr = kr * c_ref[...] + (kpp_ref[...] * grp_ref[...]) * s_ref[...]
    for h in range(MLA_HEADS):
        kn = kv[:, 256 * h:256 * h + 128]
        r = lax.rsqrt((jnp.sum(kn * kn, axis=-1, keepdims=True) + ss_pe) * (1.0 / MLA_QK) + NORM_EPS)
        k_ref[:, 256 * h:256 * h + 128] = ((kn * r) * gn_ref[...]).astype(BF16)
        k_ref[:, 256 * h + 128:256 * h + 256] = (kr * r).astype(BF16)
        v_ref[:, 128 * h:128 * h + 128] = kv[:, 256 * h + 128:256 * h + 256].astype(BF16)


def _mla_kv(ckv_src, ckv_spec, kpe_src, kpe_spec, rope_ins, w_ukv, g_kv, gk_n, gk_r, gk_rp,
            batch, rows, layer, normalize, emit_ckv):
    rope = rope_ins is not None
    nt = rows // MLA_TR
    ins = [ckv_src, kpe_src]
    specs = [ckv_spec, kpe_spec]
    if rope:
        kpp_src, kpp_spec, ctab, stab = rope_ins
        ins += [kpp_src, ctab, stab]
        specs += [kpp_spec, pl.BlockSpec((MLA_TR, 128), lambda b, t: (t, 0)), pl.BlockSpec((MLA_TR, 128), lambda b, t: (t, 0))]
    ins += [w_ukv, g_kv, gk_n, gk_r, gk_rp]
    specs += [pl.BlockSpec((None, MLA_KV_LORA, 2048), lambda b, t: (layer, 0, 0)),
              pl.BlockSpec((DEPTH, MLA_KV_LORA), lambda b, t: (0, 0))] + [pl.BlockSpec((1, 128), lambda b, t: (0, 0))] * 3
    out_shape = [jax.ShapeDtypeStruct((batch, rows, MLA_HEADS * MLA_HEAD_PAD), BF16),
                 jax.ShapeDtypeStruct((batch, rows, MLA_WIDTH), BF16)]
    out_specs = [pl.BlockSpec((None, MLA_TR, MLA_HEADS * MLA_HEAD_PAD), lambda b, t: (b, t, 0)),
                 pl.BlockSpec((None, MLA_TR, MLA_WIDTH), lambda b, t: (b, t, 0))]
    if emit_ckv:
        out_shape.append(jax.ShapeDtypeStruct((batch, rows, MLA_KV_LORA), F32))
        out_specs.append(pl.BlockSpec((None, MLA_TR, MLA_KV_LORA), lambda b, t: (b, t, 0)))
    est = 2 * MLA_KV_LORA * 2048 * 4 + 4 * MLA_TR * 2048 * 4
    return pl.pallas_call(
        functools.partial(_mla_kv_kernel, layer=layer, normalize=normalize, rope=rope, emit_ckv=emit_ckv),
        grid=(batch, nt), in_specs=specs, out_specs=out_specs, out_shape=out_shape,
        compiler_params=pltpu.CompilerParams(
            dimension_semantics=("arbitrary", "arbitrary"), vmem_limit_bytes=_vmem_limit(est)),
        name="mla_kv",
    )(*ins)


def _mla_attn_kernel(*refs, layer, rope, has_ctx):
    it = iter(refs)
    cq_ref, wq_ref, gq_ref, gfull_ref = next(it), next(it), next(it), next(it)
    if rope:
        wqp_ref, gperm_ref, c_ref, s_ref = next(it), next(it), next(it), next(it)
    k_ref, v_ref = next(it), next(it)
    if has_ctx:
        kc_ref, vc_ref = next(it), next(it)
    o_ref = next(it)
    cq = cq_ref[...]
    ql = ((cq * lax.rsqrt(jnp.mean(cq * cq, axis=-1, keepdims=True) + NORM_EPS)) * gq_ref[pl.ds(layer, 1), :]).astype(BF16)
    q_raw = jnp.dot(ql, wq_ref[...], preferred_element_type=F32)
    if rope:
        q_perm = jnp.dot(ql, wqp_ref[...], preferred_element_type=F32)
    for h in range(MLA_HEADS):
        sl = slice(256 * h, 256 * h + 256)
        vs = slice(128 * h, 128 * h + 128)
        q = q_raw[:, sl]
        r = lax.rsqrt(jnp.sum(q * q, axis=-1, keepdims=True) * (1.0 / MLA_QK) + NORM_EPS)
        if rope:
            q_rot = (q[:, 128:] * gfull_ref[:, 128:]) * c_ref[...] + (q_perm[:, vs] * gperm_ref[...]) * s_ref[...]
            qh = jnp.concatenate([q[:, :128] * gfull_ref[:, :128], q_rot], axis=1)
        else:
            qh = q * gfull_ref[...]
        qh = ((qh * r) * (MLA_SCALE * LOG2_E)).astype(BF16)
        s = lax.dot_general(qh, k_ref[:, sl], NT_DIMS, preferred_element_type=F32)
        m = jnp.max(s, axis=-1, keepdims=True)
        if has_ctx:
            sc = lax.dot_general(qh, kc_ref[:, sl], NT_DIMS, preferred_element_type=F32)
            m = jnp.maximum(m, jnp.max(sc, axis=-1, keepdims=True))
        e = jnp.exp2(s - m)
        den = jnp.sum(e, axis=-1, keepdims=True)
        o = jnp.dot(e.astype(BF16), v_ref[:, vs], preferred_element_type=F32)
        if has_ctx:
            ec = jnp.exp2(sc - m)
            den = den + jnp.sum(ec, axis=-1, keepdims=True)
            o = o + jnp.dot(ec.astype(BF16), vc_ref[:, vs], preferred_element_type=F32)
        o_ref[:, vs] = (o * (1.0 / den)).astype(BF16)


def _mla_attn(proj, wq, g_q, gq_full, rope_ins, k, v, ctx_kv, batch, t, layer):
    rope = rope_ins is not None
    has_ctx = ctx_kv is not None
    nq = t // MLA_TQ
    const = lambda shape: pl.BlockSpec(shape, lambda b, i: (0,) * len(shape))
    ins = [proj, wq, g_q, gq_full]
    specs = [pl.BlockSpec((MLA_TQ, MLA_Q_LORA), lambda b, i: (b * nq + i, COL_CQ // MLA_Q_LORA)),
             const((MLA_Q_LORA, 2048)), const((DEPTH, MLA_Q_LORA)), const((1, 256))]
    if rope:
        wqp, gq_perm, ctab, stab = rope_ins
        ins += [wqp, gq_perm, ctab, stab]
        specs += [const((MLA_Q_LORA, 1024)), const((1, 128)),
                  pl.BlockSpec((MLA_TQ, 128), lambda b, i: (i, 0)), pl.BlockSpec((MLA_TQ, 128), lambda b, i: (i, 0))]
    ins += [k, v]
    specs += [pl.BlockSpec((None, t, 2048), lambda b, i: (b, 0, 0)), pl.BlockSpec((None, t, MLA_WIDTH), lambda b, i: (b, 0, 0))]
    est = 2 * (t * 2048 * 2 + t * MLA_WIDTH * 2) + 6 * MLA_TQ * t * 4
    if has_ctx:
        kc, vc = ctx_kv
        lc = kc.shape[1]
        ins += [kc, vc]
        specs += [pl.BlockSpec((None, lc, 2048), lambda b, i: (b, 0, 0)), pl.BlockSpec((None, lc, MLA_WIDTH), lambda b, i: (b, 0, 0))]
        est += 2 * lc * 3072 * 2
    return pl.pallas_call(
        functools.partial(_mla_attn_kernel, layer=layer, rope=rope, has_ctx=has_ctx),
        grid=(batch, nq), in_specs=specs,
        out_specs=pl.BlockSpec((MLA_TQ, MLA_WIDTH), lambda b, i: (b * nq + i, 0)),
        out_shape=jax.ShapeDtypeStruct((batch * t, MLA_WIDTH), BF16),
        compiler_params=pltpu.CompilerParams(
            dimension_semantics=("arbitrary", "arbitrary"), vmem_limit_bytes=_vmem_limit(est)),
        name="mla_attn",
    )(*ins)


def _mla_branch(proj, batch, t, layer, mla_w, w_ukv, g_q, g_kv, rope, ctx):
    wq, wqp, gq_full, gq_perm, gk_n, gk_r, gk_rp = mla_w
    nt = t // MLA_TR
    ckv_spec = pl.BlockSpec((MLA_TR, MLA_KV_LORA), lambda b, i: (b * nt + i, COL_CKV // MLA_KV_LORA))
    kpe_spec = pl.BlockSpec((MLA_TR, 128), lambda b, i: (b * nt + i, COL_KPE // 128))
    if rope is None:
        k, v, ckv_n = _mla_kv(proj, ckv_spec, proj, kpe_spec, None, w_ukv, g_kv, gk_n, gk_r, gk_rp,
                              batch, t, layer, normalize=True, emit_ckv=True)
        return _mla_attn(proj, wq, g_q, gq_full, None, k, v, None, batch, t, layer), ckv_n
    c64, s64 = rope
    z64 = jnp.zeros_like(c64)
    ck, sk = jnp.concatenate([c64, z64], axis=1), jnp.concatenate([s64, z64], axis=1)
    kpp_spec = pl.BlockSpec((MLA_TR, 128), lambda b, i: (b * nt + i, COL_KPE_PERM // 128))
    k, v = _mla_kv(proj, ckv_spec, proj, kpe_spec, (proj, kpp_spec, ck, sk), w_ukv, g_kv, gk_n, gk_r, gk_rp,
                   batch, t, layer, normalize=True, emit_ckv=False)
    ctx_ckv, ctx_kpe = ctx
    lc = ctx_ckv.shape[2]
    kc, vc = _mla_kv(ctx_ckv, pl.BlockSpec((None, None, MLA_TR, MLA_KV_LORA), lambda b, i: (b, layer, i, 0)),
                     ctx_kpe, pl.BlockSpec((None, None, MLA_TR, 128), lambda b, i: (b, layer, i, 0)),
                     None, w_ukv, g_kv, gk_n, gk_r, gk_rp, batch, lc, layer, normalize=False, emit_ckv=False)
    return _mla_attn(proj, wq, g_q, gq_full, (wqp, gq_perm, ck, sk), k, v, (kc, vc), batch, t, layer), None


SWA_TR = 128
SWA_NEG = -1e30


def _pair_sum_matrix():
    g = (np.arange(128)[:, None] // 64 == np.arange(128)[None, :] // 64).astype(np.float32)
    return jnp.asarray(np.concatenate([g, g], axis=0), BF16)


def _group_rms_scale(x, pair_ref):
    sq = x * x
    hi = sq.astype(BF16)
    lo = (sq - hi.astype(F32)).astype(BF16)
    ss = jnp.dot(jnp.concatenate([hi, lo], axis=1), pair_ref[...], preferred_element_type=F32)
    return lax.rsqrt(ss * (1.0 / SWA_HEAD_DIM) + NORM_EPS)


def _swa_kv_kernel(*refs, normalize, rope, emit, pad_blocks, layer):
    it = iter(refs)
    k_ref, v_ref = next(it), next(it)
    if rope:
        kp_ref, c_ref, s_ref = next(it), next(it), next(it)
    if normalize:
        g_ref, gp_ref, pair_ref = next(it), next(it), next(it)
    ko_ref, vo_ref = next(it), next(it)
    if emit:
        kn_ref, vn_ref = next(it), next(it)
    lane = lax.broadcasted_iota(jnp.int32, (SWA_TR, 128), 1)
    low = lane < 64

    def halves(x, c, o_ref):
        sw = pltpu.roll(x, 64, axis=1)
        zero = jnp.zeros_like(x)
        o_ref[4 * c + 0] = jnp.where(low, x, zero).astype(BF16)
        o_ref[4 * c + 1] = jnp.where(low, zero, sw).astype(BF16)
        o_ref[4 * c + 2] = jnp.where(low, sw, zero).astype(BF16)
        o_ref[4 * c + 3] = jnp.where(low, zero, x).astype(BF16)

    def compute():
        for c in range(2):
            sl = slice(128 * c, 128 * c + 128)
            kb = k_ref[:, sl]
            if normalize:
                r = _group_rms_scale(kb, pair_ref)
                kn = (kb * r) * g_ref[...]
                if emit:
                    kn_ref[:, sl] = kn
                if rope:
                    kn = kn * c_ref[...] + ((kp_ref[:, sl] * r) * gp_ref[...]) * s_ref[...]
            else:
                kn = kb
            halves(kn, c, ko_ref)
            vb = v_ref[:, sl]
            if emit:
                vn_ref[:, sl] = vb
            halves(vb, c, vo_ref)

    if pad_blocks:
        t = pl.program_id(1)
        is_pad = jnp.logical_or(t == 0, t == pl.num_programs(1) - 1)

        @pl.when(is_pad)
        def _():
            ko_ref[...] = jnp.zeros_like(ko_ref)
            vo_ref[...] = jnp.zeros_like(vo_ref)

        pl.when(jnp.logical_not(is_pad))(compute)
    else:
        compute()


def _swa_kv(k_src, k_spec, v_src, v_spec, rope_ins, norm_ins, batch, rows, layer, emit, pad_blocks):
    rope = rope_ins is not None
    normalize = norm_ins is not None
    nt = rows // SWA_TR + (2 if pad_blocks else 0)
    ins, specs = [k_src, v_src], [k_spec, v_spec]
    if rope:
        kp_src, kp_spec, ctab, stab = rope_ins
        tab_row = (lambda t: jnp.clip(t - 1, 0, nt - 3)) if pad_blocks else (lambda t: t)
        tab_spec = pl.BlockSpec((SWA_TR, 128), lambda b, t: (tab_row(t), 0))
        ins += [kp_src, ctab, stab]
        specs += [kp_spec, tab_spec, tab_spec]
    if normalize:
        ins += list(norm_ins)
        specs += [pl.BlockSpec((1, 128), lambda b, t: (0, 0)), pl.BlockSpec((1, 128), lambda b, t: (0, 0)),
                  pl.BlockSpec((256, 128), lambda b, t: (0, 0))]
    out_shape = [jax.ShapeDtypeStruct((batch, 8, nt * SWA_TR, 128), BF16)] * 2
    out_specs = [pl.BlockSpec((None, 8, SWA_TR, 128), lambda b, t: (b, 0, t, 0))] * 2
    if emit:
        out_shape += [jax.ShapeDtypeStruct((batch, rows, 256), F32)] * 2
        out_specs += [pl.BlockSpec((None, SWA_TR, 256), lambda b, t: (b, t, 0))] * 2
    return pl.pallas_call(
        functools.partial(_swa_kv_kernel, normalize=normalize, rope=rope, emit=emit, pad_blocks=pad_blocks, layer=layer),
        grid=(batch, nt), in_specs=specs, out_specs=out_specs, out_shape=out_shape,
        compiler_params=pltpu.CompilerParams(dimension_semantics=("arbitrary", "arbitrary")),
        name="swa_kv",
    )(*ins)


def _swa_attn_kernel(*refs, rope, windowed, has_ctx, t_total, layer):
    it = iter(refs)
    sink_ref, q_ref = next(it), next(it)
    if rope:
        qp_ref, c_ref, s_ref = next(it), next(it), next(it)
    g_ref, gp_ref, pair_ref = next(it), next(it), next(it)
    k_ref, v_ref = next(it), next(it)
    if has_ctx:
        kc_ref, vc_ref = next(it), next(it)
    o_ref = next(it)
    n = pl.program_id(1)
    if windowed:
        start = pl.multiple_of(n * SWA_TR, SWA_TR)
        win = pl.ds(start, 3 * SWA_TR)
        r_i = lax.broadcasted_iota(jnp.int32, (2 * SWA_TR, 3 * SWA_TR), 0) & (SWA_TR - 1)
        c_i = lax.broadcasted_iota(jnp.int32, (2 * SWA_TR, 3 * SWA_TR), 1)
        kpos = (n - 1) * SWA_TR + c_i
        diff = SWA_TR + r_i - c_i
        valid2 = (kpos >= 0) & (kpos < t_total) & (diff <= SWA_WINDOW) & (diff >= -SWA_WINDOW)
    else:
        win = slice(None)
    def normed_queries(cb):
        sl = slice(128 * cb, 128 * cb + 128)
        qb = q_ref[:, sl]
        r = _group_rms_scale(qb, pair_ref)
        qn = (qb * r) * g_ref[...]
        if rope:
            qn = qn * c_ref[...] + ((qp_ref[:, sl] * r) * gp_ref[...]) * s_ref[...]
        return (qn * (SWA_SCALE * LOG2_E)).astype(BF16)

    tq = q_ref.shape[0]
    low_lanes = lax.broadcasted_iota(jnp.int32, (2 * tq, 128), 1) < SWA_HEAD_DIM

    lk = 3 * SWA_TR if windowed else k_ref.shape[1]
    lc = kc_ref.shape[1] if has_ctx else 0
    row_blk = lax.broadcasted_iota(jnp.int32, (2 * tq, 1), 0) // tq
    for j in range(SWA_KV_HEADS):
        q2 = jnp.concatenate([normed_queries(2 * j), normed_queries(2 * j + 1)], axis=0)
        k_parts, v_parts = [], []
        for e in range(2):
            k_parts.append(k_ref[2 * j + e, win, :])
            v_parts.append(v_ref[2 * j + e, win, :])
            if has_ctx:
                k_parts.append(kc_ref[2 * j + e])
                v_parts.append(vc_ref[2 * j + e])
        s = lax.dot_general(q2, jnp.concatenate(k_parts, axis=0), NT_DIMS, preferred_element_type=F32)
        p_parts, inv = [], []
        for e in range(2):
            off = e * (lk + lc)
            sink = jnp.where(row_blk == 0, sink_ref[layer, 4 * j + e], sink_ref[layer, 4 * j + 2 + e]) * LOG2_E
            s_loc = s[:, off:off + lk]
            if windowed:
                s_loc = jnp.where(valid2, s_loc, SWA_NEG)
            m = jnp.maximum(jnp.max(s_loc, axis=-1, keepdims=True), sink)
            if has_ctx:
                s_ctx = s[:, off + lk:off + lk + lc]
                m = jnp.maximum(m, jnp.max(s_ctx, axis=-1, keepdims=True))
            p = jnp.exp2(s_loc - m)
            den = jnp.sum(p, axis=-1, keepdims=True) + jnp.exp2(sink - m)
            p_parts.append(p.astype(BF16))
            if has_ctx:
                pc = jnp.exp2(s_ctx - m)
                den = den + jnp.sum(pc, axis=-1, keepdims=True)
                p_parts.append(pc.astype(BF16))
            inv.append(1.0 / den)
        o = jnp.dot(jnp.concatenate(p_parts, axis=1), jnp.concatenate(v_parts, axis=0), preferred_element_type=F32)
        o = o * jnp.where(low_lanes, inv[0], inv[1])
        o_ref[:, 256 * j:256 * j + 128] = o[:tq].astype(BF16)
        o_ref[:, 256 * j + 128:256 * j + 256] = o[tq:].astype(BF16)


def _swa_attn(sink, proj, rope_ins, norm_ins, k, v, ctx_kv, batch, t, layer):
    rope = rope_ins is not None
    has_ctx = ctx_kv is not None
    tq = SWA_TR if has_ctx else 2 * SWA_TR
    nq = t // tq
    ins = [sink, proj]
    specs = [pl.BlockSpec(memory_space=pltpu.SMEM),
             pl.BlockSpec((tq, SWA_WIDTH), lambda b, i: (b * nq + i, COL_SQ // SWA_WIDTH))]
    if rope:
        ctab, stab = rope_ins
        tab_spec = pl.BlockSpec((tq, 128), lambda b, i: (i, 0))
        ins += [proj, ctab, stab]
        specs += [pl.BlockSpec((tq, SWA_WIDTH), lambda b, i: (b * nq + i, COL_SQ_PERM // SWA_WIDTH)), tab_spec, tab_spec]
    ins += list(norm_ins)
    specs += [pl.BlockSpec((1, 128), lambda b, i: (0, 0)), pl.BlockSpec((1, 128), lambda b, i: (0, 0)),
              pl.BlockSpec((256, 128), lambda b, i: (0, 0))]
    lk = k.shape[2]
    ins += [k, v]
    specs += [pl.BlockSpec((None, 8, lk, 128), lambda b, i: (b, 0, 0, 0))] * 2
    if has_ctx:
        lc = ctx_kv[0].shape[2]
        ins += list(ctx_kv)
        specs += [pl.BlockSpec((None, 8, lc, 128), lambda b, i: (b, 0, 0, 0))] * 2
    return pl.pallas_call(
        functools.partial(_swa_attn_kernel, rope=rope, windowed=has_ctx, has_ctx=has_ctx, t_total=t, layer=layer),
        grid=(batch, nq), in_specs=specs,
        out_specs=pl.BlockSpec((tq, SWA_WIDTH), lambda b, i: (b * nq + i, 0)),
        out_shape=jax.ShapeDtypeStruct((batch * t, SWA_WIDTH), BF16),
        compiler_params=pltpu.CompilerParams(dimension_semantics=("arbitrary", "arbitrary")),
        name="swa_attn",
    )(*ins)


def _swa_branch(proj, batch, t, layer, g_qn_l, g_kn_l, sink, rope, ctx):
    perm, _ = _rope_partner()
    pair = _pair_sum_matrix()
    tile2 = lambda g: jnp.concatenate([g, g])[None]
    nt = t // SWA_TR
    if rope is None:
        k_spec = pl.BlockSpec((SWA_TR, 256), lambda b, i: (b * nt + i, COL_SK // 256))
        v_spec = pl.BlockSpec((SWA_TR, 256), lambda b, i: (b * nt + i, COL_SV // 256))
        k, v, k_n, v_raw = _swa_kv(proj, k_spec, proj, v_spec, None, (tile2(g_kn_l), tile2(g_kn_l[perm]), pair),
                                   batch, t, layer, emit=True, pad_blocks=False)
        o = _swa_attn(sink, proj, None, (tile2(g_qn_l), tile2(g_qn_l[perm]), pair), k, v, None, batch, t, layer)
        return o, k_n, v_raw
    c64, s64 = rope
    ctab, stab = jnp.concatenate([c64, c64], axis=1), jnp.concatenate([s64, s64], axis=1)
    row = lambda b, i: b * nt + jnp.clip(i - 1, 0, nt - 1)
    k_spec = pl.BlockSpec((SWA_TR, 256), lambda b, i: (row(b, i), COL_SK // 256))
    v_spec = pl.BlockSpec((SWA_TR, 256), lambda b, i: (row(b, i), COL_SV // 256))
    kp_spec = pl.BlockSpec((SWA_TR, 256), lambda b, i: (row(b, i), COL_SK_PERM // 256))
    k, v = _swa_kv(proj, k_spec, proj, v_spec, (proj, kp_spec, ctab, stab), (tile2(g_kn_l), tile2(g_kn_l[perm]), pair),
                   batch, t, layer, emit=False, pad_blocks=True)
    ctx_k, ctx_v = ctx
    lc = ctx_k.shape[2]
    c_spec = pl.BlockSpec((None, None, SWA_TR, 256), lambda b, i: (b, layer, i, 0))
    kc, vc = _swa_kv(ctx_k, c_spec, ctx_v, c_spec, None, None, batch, lc, layer, emit=False, pad_blocks=False)
    o = _swa_attn(sink, proj, (ctab, stab), (tile2(g_qn_l), tile2(g_qn_l[perm]), pair), k, v, (kc, vc), batch, t, layer)
    return o, None, None


GLA_LEVELS = (64, 32, 16, 8, 4, 2, 1)
GLA_ROWSETS = 2 + len(GLA_LEVELS)


def _gla_sum_matrix(backward):
    c = GLA_CHUNK
    t = np.arange(c)[:, None]
    j = np.arange(c)[None, :]
    sets = [j <= t, j > t]
    for g in GLA_LEVELS:
        e = (t // (2 * g)) * 2 * g + g - 1
        upper = (t // g) % 2 == 1
        sets.append(np.where(upper, (j > e) & (j <= t), (j > t) & (j <= e)))
    n = np.concatenate(sets, axis=0).astype(np.float32)
    if backward:
        n = n.reshape(GLA_ROWSETS, c, c)[:, ::-1, ::-1].reshape(GLA_ROWSETS * c, c)
    return jnp.asarray(np.concatenate([n, n, n], axis=1), BF16)


def _gla_kernel(*refs, has_s0):
    it = iter(refs)
    srcs = [tuple(next(it) for _ in range(4)) for _ in range(2)]
    nmat = (next(it), next(it))
    wg = (next(it), next(it))
    bg = (next(it), next(it))
    s0_ref = next(it) if has_s0 else None
    o_refs = (next(it), next(it))
    sfin_ref = next(it)
    s_ref = next(it)
    i = pl.program_id(1)
    c = GLA_CHUNK

    @pl.when(i == 0)
    def _():
        s_ref[...] = s0_ref[...] if has_s0 else jnp.zeros_like(s_ref)

    row = lax.broadcasted_iota(jnp.int32, (c, c), 0)
    col = lax.broadcasted_iota(jnp.int32, (c, c), 1)
    diag = row == col
    log2 = lambda g: int(g).bit_length() - 1
    odd_half = [((row >> log2(g)) & 1) == 1 for g in GLA_LEVELS]
    same_block = [(row >> (log2(g) + 1)) == (col >> (log2(g) + 1)) for g in GLA_LEVELS]
    for d in range(2):
        q_ref, k_ref, v_ref, gg_ref = srcs[d]
        z = jnp.dot(gg_ref[...].astype(BF16), wg[d][...], preferred_element_type=F32) + bg[d][...]
        la = (jnp.minimum(z, 0.0) - jnp.log1p(jnp.exp(-jnp.abs(z)))) * (1.0 / GLA_GATE_NORM)
        hi = la.astype(BF16)
        r1 = la - hi.astype(F32)
        mid = r1.astype(BF16)
        lo = (r1 - mid.astype(F32)).astype(BF16)
        ex_all = jnp.dot(nmat[d][...], jnp.concatenate([hi, mid, lo], axis=0), preferred_element_type=F32)
        for h in range(GLA_HEADS):
            sl = slice(GLA_DK * h, GLA_DK * h + GLA_DK)
            vs = slice(GLA_DV * h, GLA_DV * h + GLA_DV)
            ex = ex_all[:, sl]
            q = q_ref[:, sl] * (GLA_DK ** -0.5)
            k = k_ref[:, sl]
            v = v_ref[:, vs].astype(BF16)
            eb = jnp.exp(ex[0:c])
            ek = jnp.exp(ex[c:2 * c])
            s_old = s_ref[d, h]
            o = jnp.dot((q * eb).astype(BF16), s_old.astype(BF16), preferred_element_type=F32)
            a = jnp.where(diag, jnp.sum(q * k, axis=-1, keepdims=True), 0.0)
            for li, g in enumerate(GLA_LEVELS):
                eg = jnp.exp(ex[(2 + li) * c:(3 + li) * c])
                qe, ke = q * eg, k * eg
                qg = (jnp.where(odd_half[li], qe, 0.0) if d == 0 else jnp.where(odd_half[li], 0.0, qe)).astype(BF16)
                kg = (jnp.where(odd_half[li], 0.0, ke) if d == 0 else jnp.where(odd_half[li], ke, 0.0)).astype(BF16)
                ag = lax.dot_general(qg, kg, NT_DIMS, preferred_element_type=F32)
                if 2 * g < c:
                    ag = jnp.where(same_block[li], ag, 0.0)
                a = a + ag
            o = o + jnp.dot(a.astype(BF16), v, preferred_element_type=F32)
            o_refs[d][:, vs] = o
            a_col = eb.T[:, c - 1:c] if d == 0 else eb.T[:, 0:1]
            kt = (k * ek).T.astype(BF16)
            s_ref[d, h] = a_col * s_old + jnp.dot(kt, v, preferred_element_type=F32)

    @pl.when(i == pl.num_programs(1) - 1)
    def _():
        sfin_ref[...] = s_ref[...]


def _gla(proj, batch, t, layer, w_gf, b_gf, w_gb, b_gb, s0):
    c = GLA_CHUNK
    n = t // c
    fwd = lambda b, i: b * n + i
    bwd = lambda b, i: b * n + (n - 1 - i)
    ins, specs = [], []
    for rowf in (fwd, bwd):
        ins += [proj] * 4
        specs += [pl.BlockSpec((c, 512), lambda b, i, rowf=rowf: (rowf(b, i), COL_GQ // 512)),
                  pl.BlockSpec((c, 512), lambda b, i, rowf=rowf: (rowf(b, i), COL_GK // 512)),
                  pl.BlockSpec((c, GLA_WIDTH), lambda b, i, rowf=rowf: (rowf(b, i), COL_GV // GLA_WIDTH)),
                  pl.BlockSpec((c, 128), lambda b, i, rowf=rowf: (rowf(b, i), COL_GG // 128))]
    const = lambda shape: pl.BlockSpec(shape, lambda b, i: (0,) * len(shape))
    pad_w = lambda w, off: jnp.zeros((128, 512), F32).at[off:off + GLA_GATE_RANK].set(w).astype(BF16)
    ins += [_gla_sum_matrix(False), _gla_sum_matrix(True), pad_w(w_gf, 0), pad_w(w_gb, GLA_GATE_RANK), b_gf[None], b_gb[None]]
    specs += [const((GLA_ROWSETS * c, 3 * c))] * 2 + [const((128, 512))] * 2 + [const((1, 512))] * 2
    state_spec = pl.BlockSpec((None, 2, GLA_HEADS, GLA_DK, GLA_DV), lambda b, i: (b, 0, 0, 0, 0))
    if s0 is not None:
        ins.append(s0)
        specs.append(pl.BlockSpec((None, None, 2, GLA_HEADS, GLA_DK, GLA_DV), lambda b, i: (b, layer, 0, 0, 0, 0)))
    return pl.pallas_call(
        functools.partial(_gla_kernel, has_s0=s0 is not None),
        grid=(batch, n), in_specs=specs,
        out_specs=[pl.BlockSpec((c, GLA_WIDTH), lambda b, i: (fwd(b, i), 0)),
                   pl.BlockSpec((c, GLA_WIDTH), lambda b, i: (bwd(b, i), 0)), state_spec],
        out_shape=[jax.ShapeDtypeStruct((batch * t, GLA_WIDTH), F32)] * 2
        + [jax.ShapeDtypeStruct((batch, 2, GLA_HEADS, GLA_DK, GLA_DV), F32)],
        scratch_shapes=[pltpu.VMEM((2, GLA_HEADS, GLA_DK, GLA_DV), F32)],
        compiler_params=pltpu.CompilerParams(dimension_semantics=("arbitrary", "arbitrary")),
        name="gla",
    )(*ins)


GOUT_TM = 512


def _gla_out_kernel(of_ref, ob_ref, gate_ref, g_ref, o_ref, *, layer):
    g = g_ref[pl.ds(layer, 1), :]
    for h in range(GLA_HEADS):
        vs = slice(GLA_DV * h, GLA_DV * h + GLA_DV)
        o = of_ref[:, vs] + ob_ref[:, vs]
        y = (o * lax.rsqrt(jnp.mean(o * o, axis=-1, keepdims=True) + NORM_EPS)) * g
        gate = gate_ref[:, vs]
        o_ref[:, vs] = (y * (gate * jax.nn.sigmoid(gate))).astype(BF16)


def _gla_out(o_f, o_b, proj, g_gla_out, layer):
    m = o_f.shape[0]
    return pl.pallas_call(
        functools.partial(_gla_out_kernel, layer=layer),
        grid=(m // GOUT_TM,),
        in_specs=[pl.BlockSpec((GOUT_TM, GLA_WIDTH), lambda i: (i, 0)),
                  pl.BlockSpec((GOUT_TM, GLA_WIDTH), lambda i: (i, 0)),
                  pl.BlockSpec((GOUT_TM, GLA_WIDTH), lambda i: (i, COL_GOUT // GLA_WIDTH)),
                  pl.BlockSpec((DEPTH, GLA_DV), lambda i: (0, 0))],
        out_specs=pl.BlockSpec((GOUT_TM, GLA_WIDTH), lambda i: (i, 0)),
        out_shape=jax.ShapeDtypeStruct((m, GLA_WIDTH), BF16),
        compiler_params=pltpu.CompilerParams(dimension_semantics=("arbitrary",)),
        name="gla_out",
    )(o_f, o_b, proj, g_gla_out)


MERGE_TM = 1024
MERGE_TN = 512


def _merge_kernel(om_ref, og_ref, os_ref, gm_ref, gg_ref, gs_ref, wm_ref, wg_ref, ws_ref, o_ref):
    def branch(o_r, gate_r, w_r):
        y = jnp.dot(o_r[...], w_r[...].astype(BF16), preferred_element_type=F32)
        return jax.nn.sigmoid(gate_r[...]) * y

    merged = branch(om_ref, gm_ref, wm_ref) + branch(og_ref, gg_ref, wg_ref) + branch(os_ref, gs_ref, ws_ref)
    o_ref[...] = merged.astype(BF16)


def _merge(o_mla, o_gla, o_swa, proj, w_br_mla, w_br_gla, w_br_swa, layer):
    m = o_mla.shape[0]
    nn = D_MODEL // MERGE_TN
    o_spec = pl.BlockSpec((MERGE_TM, 1024), lambda i, j: (i, 0))
    gate_spec = lambda col: pl.BlockSpec((MERGE_TM, MERGE_TN), lambda i, j: (i, col // MERGE_TN + j))
    w_spec = pl.BlockSpec((None, 1024, MERGE_TN), lambda i, j: (layer, 0, j))
    est = 2 * (3 * MERGE_TM * 1024 * 2 + 3 * MERGE_TM * MERGE_TN * 4 + 3 * 1024 * MERGE_TN * 4 + MERGE_TM * MERGE_TN * 2)
    return pl.pallas_call(
        _merge_kernel,
        grid=(m // MERGE_TM, nn),
        in_specs=[o_spec, o_spec, o_spec, gate_spec(COL_GATE_MLA), gate_spec(COL_GATE_GLA), gate_spec(COL_GATE_SWA),
                  w_spec, w_spec, w_spec],
        out_specs=pl.BlockSpec((MERGE_TM, MERGE_TN), lambda i, j: (i, j)),
        out_shape=jax.ShapeDtypeStruct((m, D_MODEL), BF16),
        compiler_params=pltpu.CompilerParams(
            dimension_semantics=("arbitrary", "arbitrary"), vmem_limit_bytes=_vmem_limit(est)),
        name="merge",
    )(o_mla, o_gla, o_swa, proj, proj, proj, w_br_mla, w_br_gla, w_br_swa)


OUT_TM = 1024
OUT_TN = 512


def _out_proj_kernel(m_ref, w_ref, x_ref, gate_ref, o_ref, *, base_row, rows_per_cond):
    row = _cond_row(pl.program_id(0), m_ref.shape[0], base_row, rows_per_cond)
    y = jnp.dot(m_ref[...], w_ref[...].astype(BF16), preferred_element_type=F32)
    o_ref[...] = x_ref[...] + gate_ref[pl.ds(row, 1), :] * y


def _out_proj(merged, x, mods, w_out, layer, base_row, rows_per_cond):
    m = x.shape[0]
    est = 2 * (OUT_TM * D_MODEL * 2 + D_MODEL * OUT_TN * 4 + 2 * OUT_TM * OUT_TN * 4)
    return pl.pallas_call(
        functools.partial(_out_proj_kernel, base_row=base_row, rows_per_cond=rows_per_cond),
        grid=(m // OUT_TM, D_MODEL // OUT_TN),
        in_specs=[pl.BlockSpec((OUT_TM, D_MODEL), lambda i, j: (i, 0)),
                  pl.BlockSpec((None, D_MODEL, OUT_TN), lambda i, j: (layer, 0, j)),
                  pl.BlockSpec((OUT_TM, OUT_TN), lambda i, j: (i, j)),
                  pl.BlockSpec((None, None, 8, OUT_TN), lambda i, j: (layer, 5, 0, j))],
        out_specs=pl.BlockSpec((OUT_TM, OUT_TN), lambda i, j: (i, j)),
        out_shape=jax.ShapeDtypeStruct((m, D_MODEL), F32),
        compiler_params=pltpu.CompilerParams(
            dimension_semantics=("arbitrary", "arbitrary"), vmem_limit_bytes=_vmem_limit(est)),
        name="out_proj",
    )(merged, w_out, x, mods)


def _trunk_layer(x, mods, layer, w, group):
    batch, t, base_row, rows_per_cond, rope, ctx = group
    x = _ffn(x, mods, layer, 0, w['g_norm1'], w['w_ff1_gu'], w['w_ff1_down'], base_row, rows_per_cond)
    ncols = PROJ_COLS_PROMPT if rope is None else PROJ_COLS_SAMPLE
    proj = _in_proj(x, mods, layer, w['g_norm2'], w['w_in_packed'], ncols, base_row, rows_per_cond)
    mla_ctx = swa_ctx = s0 = None
    if ctx is not None:
        mla_ctx, swa_ctx, s0 = ctx[:2], ctx[2:4], ctx[4]
    o_mla, ckv_n = _mla_branch(proj, batch, t, layer, w['mla_packed'][layer], w['w_mla_ukv'], w['g_mla_q'], w['g_mla_kv'],
                               rope, mla_ctx)
    o_swa, k_n, v_raw = _swa_branch(proj, batch, t, layer, w['g_swa_qn'][layer], w['g_swa_kn'][layer], w['swa_sink'],
                                    rope, swa_ctx)
    o_f, o_b, s_fin = _gla(proj, batch, t, layer, w['w_gla_gf'][layer], w['b_gla_gf'][layer],
                           w['w_gla_gb'][layer], w['b_gla_gb'][layer], s0)
    o_gla = _gla_out(o_f, o_b, proj, w['g_gla_out'], layer)
    merged = _merge(o_mla, o_gla, o_swa, proj, w['w_br_mla'], w['w_br_gla'], w['w_br_swa'], layer)
    x = _out_proj(merged, x, mods, w['w_out'], layer, base_row, rows_per_cond)
    x = _ffn(x, mods, layer, 6, w['g_norm3'], w['w_ff2_gu'], w['w_ff2_down'], base_row, rows_per_cond)
    new_ctx = None
    if ctx is None:
        kpe = proj[:, COL_KPE:COL_KPE + MLA_ROPE].reshape(batch, t, MLA_ROPE)
        new_ctx = (ckv_n, kpe, k_n.reshape(batch, t, SWA_KV_HEADS, SWA_HEAD_DIM),
                   v_raw.reshape(batch, t, SWA_KV_HEADS, SWA_HEAD_DIM), s_fin)
    return x, new_ctx


def kernel(x_prompt, x_sample, cache_mla_ckv, cache_mla_kpe, cache_swa_k, cache_swa_v, state_gla,
           c, c_ctx, w_ada, b_ada, g_norm1, g_norm2, g_norm3,
           w_ff1_gu, w_ff1_down, w_ff2_gu, w_ff2_down, w_in,
           g_mla_q, w_mla_uq, g_mla_kv, w_mla_ukv, g_mla_qn, g_mla_kn,
           w_gla_gf, b_gla_gf, w_gla_gb, b_gla_gb, g_gla_out,
           g_swa_qn, g_swa_kn, swa_sink, w_br_mla, w_br_gla, w_br_swa, w_out):
    bp, tp, _ = x_prompt.shape
    bs, ts, _ = x_sample.shape
    assert bs + 1 <= 8, "conditioning rows are packed into one 8-row tile"
    cond8 = jnp.zeros((8, D_MODEL), F32).at[0].set(c_ctx).at[1:1 + bs].set(c)
    mods = _adaln(cond8, w_ada, b_ada)
    w = dict(g_norm1=g_norm1, g_norm2=g_norm2, g_norm3=g_norm3,
             w_ff1_gu=w_ff1_gu, w_ff1_down=w_ff1_down, w_ff2_gu=w_ff2_gu, w_ff2_down=w_ff2_down,
             w_in_packed=_pack_w_in(w_in),
             mla_packed=[_pack_mla(w_mla_uq[l], g_mla_qn[l], g_mla_kn[l]) for l in range(DEPTH)],
             w_mla_ukv=w_mla_ukv, g_mla_q=g_mla_q, g_mla_kv=g_mla_kv,
             g_swa_qn=g_swa_qn, g_swa_kn=g_swa_kn, swa_sink=swa_sink,
             w_gla_gf=w_gla_gf, b_gla_gf=b_gla_gf, w_gla_gb=w_gla_gb, b_gla_gb=b_gla_gb, g_gla_out=g_gla_out,
             w_br_mla=w_br_mla.astype(BF16), w_br_gla=w_br_gla.astype(BF16), w_br_swa=w_br_swa.astype(BF16),
             w_out=w_out.astype(BF16))
    past = cache_mla_ckv.shape[2]
    ctx = (cache_mla_ckv, jnp.pad(cache_mla_kpe, ((0, 0), (0, 0), (0, 0), (0, 128 - MLA_ROPE))),
           cache_swa_k.reshape(bs, DEPTH, past, SWA_KV_HEADS * SWA_HEAD_DIM),
           cache_swa_v.reshape(bs, DEPTH, past, SWA_KV_HEADS * SWA_HEAD_DIM), state_gla)
    group_p = (bp, tp, 0, bp * tp, None, None)
    group_s = (bs, ts, 1, ts, _rope_tables(ts), ctx)
    y_p = x_prompt.reshape(bp * tp, D_MODEL)
    y_s = x_sample.reshape(bs * ts, D_MODEL)
    new = []
    for l in range(DEPTH):
        y_p, new_ctx = _trunk_layer(y_p, mods, l, w, group_p)
        new.append(new_ctx)
        y_s, _ = _trunk_layer(y_s, mods, l, w, group_s)
    stacked = tuple(jnp.stack([new[l][k] for l in range(DEPTH)], axis=1) for k in range(5))
    return (y_p.reshape(bp, tp, D_MODEL), y_s.reshape(bs, ts, D_MODEL)) + stacked
```

```python
import functools

import numpy as np
import jax
import jax.numpy as jnp
from jax import lax
from jax.experimental import pallas as pl
from jax.experimental.pallas import tpu as pltpu

F32 = jnp.float32
BF16 = jnp.bfloat16

D_MODEL = 2048
DEPTH = 2
GRID_W = 64
ROPE_BASE = 10000.0
NORM_EPS = 1e-6
MLA_HEADS = 8
MLA_Q_LORA = 512
MLA_KV_LORA = 512
MLA_NOPE = 128
MLA_ROPE = 64
MLA_V = 128
MLA_QK = MLA_NOPE + MLA_ROPE
MLA_SCALE = MLA_QK ** -0.5
MLA_WIDTH = MLA_HEADS * MLA_V
MLA_HEAD_PAD = 256
GLA_HEADS = 4
GLA_DK = 128
GLA_DV = 256
GLA_GATE_RANK = 16
GLA_GATE_NORM = 16.0
GLA_WIDTH = GLA_HEADS * GLA_DV
GLA_CHUNK = 128
SWA_HEADS = 16
SWA_KV_HEADS = 4
SWA_HEAD_DIM = 64
SWA_WINDOW = 128
SWA_SCALE = SWA_HEAD_DIM ** -0.5
SWA_WIDTH = SWA_HEADS * SWA_HEAD_DIM
D_FF = 5632
N_MOD = 9
IN_SPLITS = (MLA_Q_LORA, MLA_KV_LORA, MLA_ROPE,
             GLA_HEADS * GLA_DK, GLA_HEADS * GLA_DK, GLA_WIDTH, GLA_WIDTH, GLA_GATE_RANK, GLA_GATE_RANK,
             SWA_WIDTH, SWA_KV_HEADS * SWA_HEAD_DIM, SWA_KV_HEADS * SWA_HEAD_DIM,
             D_MODEL, D_MODEL, D_MODEL)

V7X_LANES = 128
V7X_SUBLANES = 8
V7X_VMEM_BYTES = 64 * 1024 * 1024
MIB = 1024 * 1024

NT_DIMS = (((1,), (1,)), ((), ()))
LOG2_E = 1.4426950408889634

COL_GATE_MLA = 0
COL_GATE_GLA = 2048
COL_GATE_SWA = 4096
COL_CQ = 6144
COL_CKV = 6656
COL_GQ = 7168
COL_GK = 7680
COL_GV = 8192
COL_GOUT = 9216
COL_SQ = 10240
COL_SK = 11264
COL_SV = 11520
COL_KPE = 11776
COL_GG = 11904
PROJ_COLS_PROMPT = 12288
COL_SQ_PERM = 12288
COL_SK_PERM = 13312
COL_KPE_PERM = 13568
PROJ_COLS_SAMPLE = 14336
PROJ_TN = 1024


def _vmem_limit(nbytes):
    return int(min(nbytes + 12 * MIB, V7X_VMEM_BYTES - 6 * MIB))


def _rope_partner():
    i = np.arange(64)
    within = i % 32
    perm = (i // 32) * 32 + (within + 16) % 32
    sign = np.where(within < 16, -1.0, 1.0).astype(np.float32)
    return perm, sign


ADA_TN = 1024


def _adaln_kernel(cond_ref, w_ref, b_ref, o_ref):
    c = cond_ref[...]
    s = (c * jax.nn.sigmoid(c)).astype(BF16)
    o_ref[...] = jnp.dot(s, w_ref[...].astype(BF16), preferred_element_type=F32) + b_ref[...]


def _adaln(cond8, w_ada, b_ada):
    halves = D_MODEL // ADA_TN
    nj = N_MOD * halves
    return pl.pallas_call(
        _adaln_kernel,
        grid=(DEPTH, nj),
        in_specs=[
            pl.BlockSpec((8, D_MODEL), lambda l, j: (0, 0)),
            pl.BlockSpec((None, D_MODEL, ADA_TN), lambda l, j: (l, 0, j)),
            pl.BlockSpec((None, 1, ADA_TN), lambda l, j: (l, 0, j)),
        ],
        out_specs=pl.BlockSpec((None, None, 8, ADA_TN), lambda l, j: (l, j // halves, 0, j % halves)),
        out_shape=jax.ShapeDtypeStruct((DEPTH, N_MOD, 8, D_MODEL), F32),
        compiler_params=pltpu.CompilerParams(
            dimension_semantics=("arbitrary", "arbitrary"),
            vmem_limit_bytes=_vmem_limit(2 * D_MODEL * ADA_TN * 4)),
        name="adaln",
    )(cond8, w_ada, b_ada.reshape(DEPTH, 1, N_MOD * D_MODEL))


NORM_ROWS = 64


def _modulated_norm_to(h_ref, x_ref, g, shift, scale):
    gain = g * (1.0 + scale)

    def body(c, carry):
        r0 = pl.multiple_of(c * NORM_ROWS, NORM_ROWS)
        x = x_ref[pl.ds(r0, NORM_ROWS), :]
        ms = jnp.mean(x * x, axis=-1, keepdims=True)
        h_ref[pl.ds(r0, NORM_ROWS), :] = ((x * lax.rsqrt(ms + NORM_EPS)) * gain + shift).astype(BF16)
        return carry

    lax.fori_loop(0, x_ref.shape[0] // NORM_ROWS, body, 0, unroll=2)


def _cond_row(i, tm, base_row, rows_per_cond):
    return base_row + (i * tm) // rows_per_cond


FFN_TM = 1024
FFN_TF = 256
FFN_TN = 512


def _ffn_kernel(x_ref, shift_ref, scale_ref, gate_ref, g_ref, wg_ref, wu_ref, wd_ref, o_ref, h_ref,
                *, layer, base_row, rows_per_cond):
    i = pl.program_id(0)
    j = pl.program_id(1)
    row = _cond_row(i, x_ref.shape[0], base_row, rows_per_cond)

    @pl.when(j == 0)
    def _():
        _modulated_norm_to(h_ref, x_ref, g_ref[pl.ds(layer, 1), :],
                           shift_ref[pl.ds(row, 1), :], scale_ref[pl.ds(row, 1), :])
        o_ref[...] = jnp.zeros_like(o_ref)

    h = h_ref[...]
    a = jnp.dot(h, wg_ref[...].astype(BF16), preferred_element_type=F32)
    u = jnp.dot(h, wu_ref[...].astype(BF16), preferred_element_type=F32)
    act = (a * jax.nn.sigmoid(a) * u).astype(BF16)
    for n in range(0, D_MODEL, FFN_TN):
        o_ref[:, n:n + FFN_TN] += jnp.dot(act, wd_ref[:, n:n + FFN_TN].astype(BF16), preferred_element_type=F32)

    @pl.when(j == pl.num_programs(1) - 1)
    def _():
        o_ref[...] = x_ref[...] + (0.5 * gate_ref[pl.ds(row, 1), :]) * o_ref[...]


def _ffn(x, mods, layer, first_mod, g_norm, w_gu, w_down, base_row, rows_per_cond):
    m = x.shape[0]
    nf = D_FF // FFN_TF
    mod_spec = lambda k: pl.BlockSpec((None, None, 8, D_MODEL), lambda i, j: (layer, first_mod + k, 0, 0))
    est = (2 * FFN_TM * D_MODEL * 4 * 2 + FFN_TM * D_MODEL * 2
           + 2 * 3 * D_MODEL * FFN_TF * w_gu.dtype.itemsize)
    return pl.pallas_call(
        functools.partial(_ffn_kernel, layer=layer, base_row=base_row, rows_per_cond=rows_per_cond),
        grid=(m // FFN_TM, nf),
        in_specs=[
            pl.BlockSpec((FFN_TM, D_MODEL), lambda i, j: (i, 0)),
            mod_spec(0), mod_spec(1), mod_spec(2),
            pl.BlockSpec((DEPTH, D_MODEL), lambda i, j: (0, 0)),
            pl.BlockSpec((None, D_MODEL, FFN_TF), lambda i, j: (layer, 0, j)),
            pl.BlockSpec((None, D_MODEL, FFN_TF), lambda i, j: (layer, 0, j + nf)),
            pl.BlockSpec((None, FFN_TF, D_MODEL), lambda i, j: (layer, j, 0)),
        ],
        out_specs=pl.BlockSpec((FFN_TM, D_MODEL), lambda i, j: (i, 0)),
        out_shape=jax.ShapeDtypeStruct((m, D_MODEL), F32),
        scratch_shapes=[pltpu.VMEM((FFN_TM, D_MODEL), BF16)],
        compiler_params=pltpu.CompilerParams(
            dimension_semantics=("arbitrary", "arbitrary"), vmem_limit_bytes=_vmem_limit(est)),
        name="ffn",
    )(x, mods, mods, mods, g_norm, w_gu, w_gu, w_down)


PROJ_TM = 1024


IN_COLS = sum(IN_SPLITS)
_SRC = dict(zip(('cq', 'ckv', 'kpe', 'gq', 'gk', 'gv', 'gout', 'ggf', 'ggb', 'sq', 'sk', 'sv', 'gate_mla'),
                np.concatenate([[0], np.cumsum(IN_SPLITS)]).tolist()))
PACK_ROWS = 512
PACK_TILES = PROJ_COLS_SAMPLE // PACK_ROWS


def _pack_plan():
    kinds, a, b = [], [], []
    def add(kind, src_a, src_b=0):
        kinds.append(kind); a.append(src_a); b.append(src_b)
    for dst, src, width in ((COL_GATE_MLA, _SRC['gate_mla'], 3 * D_MODEL), (COL_CQ, _SRC['cq'], 1024),
                            (COL_GQ, _SRC['gq'], 3072), (COL_SQ, _SRC['sq'], 1536)):
        assert dst == len(kinds) * PACK_ROWS and width % PACK_ROWS == 0
        for t in range(width // PACK_ROWS):
            add(0, src + t * PACK_ROWS)
    assert COL_KPE == len(kinds) * PACK_ROWS
    add(2, _SRC['kpe'], _SRC['ggf'])
    assert COL_SQ_PERM == len(kinds) * PACK_ROWS
    add(1, _SRC['sq']); add(1, _SRC['sq'] + PACK_ROWS)
    assert COL_SK_PERM == len(kinds) * PACK_ROWS
    add(3, _SRC['sk'], _SRC['kpe'])
    add(4, 0)
    assert len(kinds) == PACK_TILES
    return np.asarray([kinds, a, b], np.int32)


def _pack_kernel(plan_ref, a3_ref, b3_ref, o_ref):
    j = pl.program_id(1)
    kind = plan_ref[0, j]
    a_ref, b_ref = a3_ref.at[0], b3_ref.at[0]

    def swap_partners(src_ref, rows, dst0):
        for r in range(0, rows, 32):
            o_ref[dst0 + r:dst0 + r + 16, :] = src_ref[r + 16:r + 32, :].astype(BF16)
            o_ref[dst0 + r + 16:dst0 + r + 32, :] = src_ref[r:r + 16, :].astype(BF16)

    @pl.when(kind == 0)
    def _():
        o_ref[...] = a_ref[...].astype(BF16)

    @pl.when(kind == 1)
    def _():
        swap_partners(a_ref, PACK_ROWS, 0)

    @pl.when(kind == 2)
    def _():
        o_ref[...] = jnp.zeros_like(o_ref)
        o_ref[0:MLA_ROPE, :] = a_ref[0:MLA_ROPE, :].astype(BF16)
        o_ref[128:128 + 2 * GLA_GATE_RANK, :] = b_ref[0:2 * GLA_GATE_RANK, :].astype(BF16)

    @pl.when(kind == 3)
    def _():
        o_ref[...] = jnp.zeros_like(o_ref)
        swap_partners(a_ref, SWA_KV_HEADS * SWA_HEAD_DIM, 0)
        swap_partners(b_ref, MLA_ROPE, SWA_KV_HEADS * SWA_HEAD_DIM)

    @pl.when(kind == 4)
    def _():
        o_ref[...] = jnp.zeros_like(o_ref)


def _pack_w_in(w_in):
    w_t = jnp.swapaxes(w_in, 1, 2)
    window = lambda which: pl.BlockSpec((pl.Element(1), pl.Element(PACK_ROWS), pl.Element(D_MODEL)),
                                        lambda l, j, plan: (l, pl.multiple_of(plan[which, j], 32), 0))
    return pl.pallas_call(
        _pack_kernel,
        grid_spec=pltpu.PrefetchScalarGridSpec(
            num_scalar_prefetch=1, grid=(DEPTH, PACK_TILES),
            in_specs=[window(1), window(2)],
            out_specs=pl.BlockSpec((None, PACK_ROWS, D_MODEL), lambda l, j, plan: (l, j, 0))),
        out_shape=jax.ShapeDtypeStruct((DEPTH, PROJ_COLS_SAMPLE, D_MODEL), BF16),
        compiler_params=pltpu.CompilerParams(
            dimension_semantics=("arbitrary", "arbitrary"),
            vmem_limit_bytes=_vmem_limit(2 * PACK_ROWS * D_MODEL * (4 + 4 + 2))),
        name="pack_w_in",
    )(jnp.asarray(_pack_plan()), w_t, w_t)


def _proj_kernel(x_ref, shift_ref, scale_ref, g_ref, w_ref, o_ref, h_ref, *, layer, base_row, rows_per_cond):
    i = pl.program_id(0)
    j = pl.program_id(1)
    row = _cond_row(i, x_ref.shape[0], base_row, rows_per_cond)

    @pl.when(j == 0)
    def _():
        _modulated_norm_to(h_ref, x_ref, g_ref[pl.ds(layer, 1), :],
                           shift_ref[pl.ds(row, 1), :], scale_ref[pl.ds(row, 1), :])

    o_ref[...] = lax.dot_general(h_ref[...], w_ref[...], NT_DIMS, preferred_element_type=F32)


def _in_proj(x, mods, layer, g_norm2, w_packed, ncols, base_row, rows_per_cond):
    m = x.shape[0]
    mod_spec = lambda k: pl.BlockSpec((None, None, 8, D_MODEL), lambda i, j: (layer, 3 + k, 0, 0))
    est = 2 * PROJ_TM * D_MODEL * 4 + PROJ_TM * D_MODEL * 2 + 2 * D_MODEL * PROJ_TN * 2 + 2 * PROJ_TM * PROJ_TN * 4
    return pl.pallas_call(
        functools.partial(_proj_kernel, layer=layer, base_row=base_row, rows_per_cond=rows_per_cond),
        grid=(m // PROJ_TM, ncols // PROJ_TN),
        in_specs=[
            pl.BlockSpec((PROJ_TM, D_MODEL), lambda i, j: (i, 0)),
            mod_spec(0), mod_spec(1),
            pl.BlockSpec((DEPTH, D_MODEL), lambda i, j: (0, 0)),
            pl.BlockSpec((None, PROJ_TN, D_MODEL), lambda i, j: (layer, j, 0)),
        ],
        out_specs=pl.BlockSpec((PROJ_TM, PROJ_TN), lambda i, j: (i, j)),
        out_shape=jax.ShapeDtypeStruct((m, ncols), F32),
        scratch_shapes=[pltpu.VMEM((PROJ_TM, D_MODEL), BF16)],
        compiler_params=pltpu.CompilerParams(
            dimension_semantics=("arbitrary", "arbitrary"), vmem_limit_bytes=_vmem_limit(est)),
        name="in_proj",
    )(x, mods, mods, g_norm2, w_packed)


def _rope_tables(t):
    pos = jnp.arange(t)
    inv_freq = ROPE_BASE ** (-jnp.arange(16, dtype=F32) / 16)

    def cs(p):
        ang = p.astype(F32)[:, None] * inv_freq[None, :]
        return jnp.concatenate([jnp.cos(ang)] * 2, axis=1), jnp.concatenate([jnp.sin(ang)] * 2, axis=1)

    cr, sr = cs(pos // GRID_W)
    cc, sc = cs(pos % GRID_W)
    _, sign = _rope_partner()
    return jnp.concatenate([cr, cc], axis=1), jnp.concatenate([sr, sc], axis=1) * sign[None, :]


MLA_TR = 256
MLA_TQ = 256


def _pack_mla(w_uq_l, g_qn_l, g_kn_l):
    perm, _ = _rope_partner()
    w = w_uq_l.reshape(MLA_Q_LORA, MLA_HEADS, MLA_QK)
    nope, rope_w = w[:, :, :MLA_NOPE], w[:, :, MLA_NOPE:]
    z64 = jnp.zeros((MLA_Q_LORA, MLA_HEADS, 64), w.dtype)
    wq = jnp.concatenate([nope, rope_w, z64], axis=-1).reshape(MLA_Q_LORA, -1).astype(BF16)
    wqp = jnp.concatenate([rope_w[:, :, perm], z64], axis=-1).reshape(MLA_Q_LORA, -1).astype(BF16)
    v64 = jnp.zeros((64,), F32)
    gq_full = jnp.concatenate([g_qn_l, v64])[None]
    gq_perm = jnp.concatenate([g_qn_l[MLA_NOPE:][perm], v64])[None]
    gk_n = g_kn_l[:MLA_NOPE][None]
    gk_r = jnp.concatenate([g_kn_l[MLA_NOPE:], v64])[None]
    gk_rp = jnp.concatenate([g_kn_l[MLA_NOPE:][perm], v64])[None]
    return wq, wqp, gq_full, gq_perm, gk_n, gk_r, gk_rp


def _mla_kv_kernel(*refs, layer, normalize, rope, emit_ckv):
    it = iter(refs)
    ckv_ref, kpe_ref = next(it), next(it)
    if rope:
        kpp_ref, c_ref, s_ref = next(it), next(it), next(it)
    w_ref, gkv_ref, gn_ref, gr_ref, grp_ref = next(it), next(it), next(it), next(it), next(it)
    k_ref, v_ref = next(it), next(it)
    ckv = ckv_ref[...]
    if normalize:
        ckv = (ckv * lax.rsqrt(jnp.mean(ckv * ckv, axis=-1, keepdims=True) + NORM_EPS)) * gkv_ref[pl.ds(layer, 1), :]
    if emit_ckv:
        next(it)[...] = ckv
    kv = jnp.dot(ckv.astype(BF16), w_ref[...].astype(BF16), preferred_element_type=F32)
    kpe = kpe_ref[...]
    ss_pe = jnp.sum(kpe * kpe, axis=-1, keepdims=True)
    kr = kpe * gr_ref[...]
    if rope:
        kr = kr * c_ref[...] + (kpp_ref[...] * grp_ref[...]) * s_ref[...]
    for h in range(MLA_HEADS):
        kn = kv[:, 256 * h:256 * h + 128]
        r = lax.rsqrt((jnp.sum(kn * kn, axis=-1, keepdims=True) + ss_pe) * (1.0 / MLA_QK) + NORM_EPS)
        k_ref[:, 256 * h:256 * h + 128] = ((kn * r) * gn_ref[...]).astype(BF16)
        k_ref[:, 256 * h + 128:256 * h + 256] = (kr * r).astype(BF16)
        v_ref[:, 128 * h:128 * h + 128] = kv[:, 256 * h + 128:256 * h + 256].astype(BF16)


def _mla_kv(ckv_src, ckv_spec, kpe_src, kpe_spec, rope_ins, w_ukv, g_kv, gk_n, gk_r, gk_rp,
            batch, rows, layer, normalize, emit_ckv):
    rope = rope_ins is not None
    nt = rows // MLA_TR
    ins = [ckv_src, kpe_src]
    specs = [ckv_spec, kpe_spec]
    if rope:
        kpp_src, kpp_spec, ctab, stab = rope_ins
        ins += [kpp_src, ctab, stab]
        specs += [kpp_spec, pl.BlockSpec((MLA_TR, 128), lambda b, t: (t, 0)), pl.BlockSpec((MLA_TR, 128), lambda b, t: (t, 0))]
    ins += [w_ukv, g_kv, gk_n, gk_r, gk_rp]
    specs += [pl.BlockSpec((None, MLA_KV_LORA, 2048), lambda b, t: (layer, 0, 0)),
              pl.BlockSpec((DEPTH, MLA_KV_LORA), lambda b, t: (0, 0))] + [pl.BlockSpec((1, 128), lambda b, t: (0, 0))] * 3
    out_shape = [jax.ShapeDtypeStruct((batch, rows, MLA_HEADS * MLA_HEAD_PAD), BF16),
                 jax.ShapeDtypeStruct((batch, rows, MLA_WIDTH), BF16)]
    out_specs = [pl.BlockSpec((None, MLA_TR, MLA_HEADS * MLA_HEAD_PAD), lambda b, t: (b, t, 0)),
                 pl.BlockSpec((None, MLA_TR, MLA_WIDTH), lambda b, t: (b, t, 0))]
    if emit_ckv:
        out_shape.append(jax.ShapeDtypeStruct((batch, rows, MLA_KV_LORA), F32))
        out_specs.append(pl.BlockSpec((None, MLA_TR, MLA_KV_LORA), lambda b, t: (b, t, 0)))
    est = 2 * MLA_KV_LORA * 2048 * 4 + 4 * MLA_TR * 2048 * 4
    return pl.pallas_call(
        functools.partial(_mla_kv_kernel, layer=layer, normalize=normalize, rope=rope, emit_ckv=emit_ckv),
        grid=(batch, nt), in_specs=specs, out_specs=out_specs, out_shape=out_shape,
        compiler_params=pltpu.CompilerParams(
            dimension_semantics=("arbitrary", "arbitrary"), vmem_limit_bytes=_vmem_limit(est)),
        name="mla_kv",
    )(*ins)


def _mla_attn_kernel(*refs, layer, rope, has_ctx):
    it = iter(refs)
    cq_ref, wq_ref, gq_ref, gfull_ref = next(it), next(it), next(it), next(it)
    if rope:
        wqp_ref, gperm_ref, c_ref, s_ref = next(it), next(it), next(it), next(it)
    k_ref, v_ref = next(it), next(it)
    if has_ctx:
        kc_ref, vc_ref = next(it), next(it)
    o_ref = next(it)
    cq = cq_ref[...]
    ql = ((cq * lax.rsqrt(jnp.mean(cq * cq, axis=-1, keepdims=True) + NORM_EPS)) * gq_ref[pl.ds(layer, 1), :]).astype(BF16)
    q_raw = jnp.dot(ql, wq_ref[...], preferred_element_type=F32)
    if rope:
        q_perm = jnp.dot(ql, wqp_ref[...], preferred_element_type=F32)
    for h in range(MLA_HEADS):
        sl = slice(256 * h, 256 * h + 256)
        vs = slice(128 * h, 128 * h + 128)
        q = q_raw[:, sl]
        r = lax.rsqrt(jnp.sum(q * q, axis=-1, keepdims=True) * (1.0 / MLA_QK) + NORM_EPS)
        if rope:
            q_rot = (q[:, 128:] * gfull_ref[:, 128:]) * c_ref[...] + (q_perm[:, vs] * gperm_ref[...]) * s_ref[...]
            qh = jnp.concatenate([q[:, :128] * gfull_ref[:, :128], q_rot], axis=1)
        else:
            qh = q * gfull_ref[...]
        qh = ((qh * r) * (MLA_SCALE * LOG2_E)).astype(BF16)
        s = lax.dot_general(qh, k_ref[:, sl], NT_DIMS, preferred_element_type=F32)
        m = jnp.max(s, axis=-1, keepdims=True)
        if has_ctx:
            sc = lax.dot_general(qh, kc_ref[:, sl], NT_DIMS, preferred_element_type=F32)
            m = jnp.maximum(m, jnp.max(sc, axis=-1, keepdims=True))
        e = jnp.exp2(s - m)
        den = jnp.sum(e, axis=-1, keepdims=True)
        o = jnp.dot(e.astype(BF16), v_ref[:, vs], preferred_element_type=F32)
        if has_ctx:
            ec = jnp.exp2(sc - m)
            den = den + jnp.sum(ec, axis=-1, keepdims=True)
            o = o + jnp.dot(ec.astype(BF16), vc_ref[:, vs], preferred_element_type=F32)
        o_ref[:, vs] = (o * (1.0 / den)).astype(BF16)


def _mla_attn(proj, wq, g_q, gq_full, rope_ins, k, v, ctx_kv, batch, t, layer):
    rope = rope_ins is not None
    has_ctx = ctx_kv is not None
    nq = t // MLA_TQ
    const = lambda shape: pl.BlockSpec(shape, lambda b, i: (0,) * len(shape))
    ins = [proj, wq, g_q, gq_full]
    specs = [pl.BlockSpec((MLA_TQ, MLA_Q_LORA), lambda b, i: (b * nq + i, COL_CQ // MLA_Q_LORA)),
             const((MLA_Q_LORA, 2048)), const((DEPTH, MLA_Q_LORA)), const((1, 256))]
    if rope:
        wqp, gq_perm, ctab, stab = rope_ins
        ins += [wqp, gq_perm, ctab, stab]
        specs += [const((MLA_Q_LORA, 1024)), const((1, 128)),
                  pl.BlockSpec((MLA_TQ, 128), lambda b, i: (i, 0)), pl.BlockSpec((MLA_TQ, 128), lambda b, i: (i, 0))]
    ins += [k, v]
    specs += [pl.BlockSpec((None, t, 2048), lambda b, i: (b, 0, 0)), pl.BlockSpec((None, t, MLA_WIDTH), lambda b, i: (b, 0, 0))]
    est = 2 * (t * 2048 * 2 + t * MLA_WIDTH * 2) + 6 * MLA_TQ * t * 4
    if has_ctx:
        kc, vc = ctx_kv
        lc = kc.shape[1]
        ins += [kc, vc]
        specs += [pl.BlockSpec((None, lc, 2048), lambda b, i: (b, 0, 0)), pl.BlockSpec((None, lc, MLA_WIDTH), lambda b, i: (b, 0, 0))]
        est += 2 * lc * 3072 * 2
    return pl.pallas_call(
        functools.partial(_mla_attn_kernel, layer=layer, rope=rope, has_ctx=has_ctx),
        grid=(batch, nq), in_specs=specs,
        out_specs=pl.BlockSpec((MLA_TQ, MLA_WIDTH), lambda b, i: (b * nq + i, 0)),
        out_shape=jax.ShapeDtypeStruct((batch * t, MLA_WIDTH), BF16),
        compiler_params=pltpu.CompilerParams(
            dimension_semantics=("arbitrary", "arbitrary"), vmem_limit_bytes=_vmem_limit(est)),
        name="mla_attn",
    )(*ins)


def _mla_branch(proj, batch, t, layer, mla_w, w_ukv, g_q, g_kv, rope, ctx):
    wq, wqp, gq_full, gq_perm, gk_n, gk_r, gk_rp = mla_w
    nt = t // MLA_TR
    ckv_spec = pl.BlockSpec((MLA_TR, MLA_KV_LORA), lambda b, i: (b * nt + i, COL_CKV // MLA_KV_LORA))
    kpe_spec = pl.BlockSpec((MLA_TR, 128), lambda b, i: (b * nt + i, COL_KPE // 128))
    if rope is None:
        k, v, ckv_n = _mla_kv(proj, ckv_spec, proj, kpe_spec, None, w_ukv, g_kv, gk_n, gk_r, gk_rp,
                              batch, t, layer, normalize=True, emit_ckv=True)
        return _mla_attn(proj, wq, g_q, gq_full, None, k, v, None, batch, t, layer), ckv_n
    c64, s64 = rope
    z64 = jnp.zeros_like(c64)
    ck, sk = jnp.concatenate([c64, z64], axis=1), jnp.concatenate([s64, z64], axis=1)
    kpp_spec = pl.BlockSpec((MLA_TR, 128), lambda b, i: (b * nt + i, COL_KPE_PERM // 128))
    k, v = _mla_kv(proj, ckv_spec, proj, kpe_spec, (proj, kpp_spec, ck, sk), w_ukv, g_kv, gk_n, gk_r, gk_rp,
                   batch, t, layer, normalize=True, emit_ckv=False)
    ctx_ckv, ctx_kpe = ctx
    lc = ctx_ckv.shape[2]
    kc, vc = _mla_kv(ctx_ckv, pl.BlockSpec((None, None, MLA_TR, MLA_KV_LORA), lambda b, i: (b, layer, i, 0)),
                     ctx_kpe, pl.BlockSpec((None, None, MLA_TR, 128), lambda b, i: (b, layer, i, 0)),
                     None, w_ukv, g_kv, gk_n, gk_r, gk_rp, batch, lc, layer, normalize=False, emit_ckv=False)
    return _mla_attn(proj, wq, g_q, gq_full, (wqp, gq_perm, ck, sk), k, v, (kc, vc), batch, t, layer), None


SWA_TR = 128
SWA_NEG = -1e30


def _pair_sum_matrix():
    g = (np.arange(128)[:, None] // 64 == np.arange(128)[None, :] // 64).astype(np.float32)
    return jnp.asarray(np.concatenate([g, g], axis=0), BF16)


def _group_rms_scale(x, pair_ref):
    sq = x * x
    hi = sq.astype(BF16)
    lo = (sq - hi.astype(F32)).astype(BF16)
    ss = jnp.dot(jnp.concatenate([hi, lo], axis=1), pair_ref[...], preferred_element_type=F32)
    return lax.rsqrt(ss * (1.0 / SWA_HEAD_DIM) + NORM_EPS)


def _swa_kv_kernel(*refs, normalize, rope, emit, pad_blocks, layer):
    it = iter(refs)
    k_ref, v_ref = next(it), next(it)
    if rope:
        kp_ref, c_ref, s_ref = next(it), next(it), next(it)
    if normalize:
        g_ref, gp_ref, pair_ref = next(it), next(it), next(it)
    ko_ref, vo_ref = next(it), next(it)
    if emit:
        kn_ref, vn_ref = next(it), next(it)
    lane = lax.broadcasted_iota(jnp.int32, (SWA_TR, 128), 1)
    low = lane < 64

    def halves(x, c, o_ref):
        sw = pltpu.roll(x, 64, axis=1)
        zero = jnp.zeros_like(x)
        o_ref[4 * c + 0] = jnp.where(low, x, zero).astype(BF16)
        o_ref[4 * c + 1] = jnp.where(low, zero, sw).astype(BF16)
        o_ref[4 * c + 2] = jnp.where(low, sw, zero).astype(BF16)
        o_ref[4 * c + 3] = jnp.where(low, zero, x).astype(BF16)

    def compute():
        for c in range(2):
            sl = slice(128 * c, 128 * c + 128)
            kb = k_ref[:, sl]
            if normalize:
                r = _group_rms_scale(kb, pair_ref)
                kn = (kb * r) * g_ref[...]
                if emit:
                    kn_ref[:, sl] = kn
                if rope:
                    kn = kn * c_ref[...] + ((kp_ref[:, sl] * r) * gp_ref[...]) * s_ref[...]
            else:
                kn = kb
            halves(kn, c, ko_ref)
            vb = v_ref[:, sl]
            if emit:
                vn_ref[:, sl] = vb
            halves(vb, c, vo_ref)

    if pad_blocks:
        t = pl.program_id(1)
        is_pad = jnp.logical_or(t == 0, t == pl.num_programs(1) - 1)

        @pl.when(is_pad)
        def _():
            ko_ref[...] = jnp.zeros_like(ko_ref)
            vo_ref[...] = jnp.zeros_like(vo_ref)

        pl.when(jnp.logical_not(is_pad))(compute)
    else:
        compute()


def _swa_kv(k_src, k_spec, v_src, v_spec, rope_ins, norm_ins, batch, rows, layer, emit, pad_blocks):
    rope = rope_ins is not None
    normalize = norm_ins is not None
    nt = rows // SWA_TR + (2 if pad_blocks else 0)
    ins, specs = [k_src, v_src], [k_spec, v_spec]
    if rope:
        kp_src, kp_spec, ctab, stab = rope_ins
        tab_row = (lambda t: jnp.clip(t - 1, 0, nt - 3)) if pad_blocks else (lambda t: t)
        tab_spec = pl.BlockSpec((SWA_TR, 128), lambda b, t: (tab_row(t), 0))
        ins += [kp_src, ctab, stab]
        specs += [kp_spec, tab_spec, tab_spec]
    if normalize:
        ins += list(norm_ins)
        specs += [pl.BlockSpec((1, 128), lambda b, t: (0, 0)), pl.BlockSpec((1, 128), lambda b, t: (0, 0)),
                  pl.BlockSpec((256, 128), lambda b, t: (0, 0))]
    out_shape = [jax.ShapeDtypeStruct((batch, 8, nt * SWA_TR, 128), BF16)] * 2
    out_specs = [pl.BlockSpec((None, 8, SWA_TR, 128), lambda b, t: (b, 0, t, 0))] * 2
    if emit:
        out_shape += [jax.ShapeDtypeStruct((batch, rows, 256), F32)] * 2
        out_specs += [pl.BlockSpec((None, SWA_TR, 256), lambda b, t: (b, t, 0))] * 2
    return pl.pallas_call(
        functools.partial(_swa_kv_kernel, normalize=normalize, rope=rope, emit=emit, pad_blocks=pad_blocks, layer=layer),
        grid=(batch, nt), in_specs=specs, out_specs=out_specs, out_shape=out_shape,
        compiler_params=pltpu.CompilerParams(dimension_semantics=("arbitrary", "arbitrary")),
        name="swa_kv",
    )(*ins)


def _swa_attn_kernel(*refs, rope, windowed, has_ctx, t_total, layer):
    it = iter(refs)
    sink_ref, q_ref = next(it), next(it)
    if rope:
        qp_ref, c_ref, s_ref = next(it), next(it), next(it)
    g_ref, gp_ref, pair_ref = next(it), next(it), next(it)
    k_ref, v_ref = next(it), next(it)
    if has_ctx:
        kc_ref, vc_ref = next(it), next(it)
    o_ref = next(it)
    n = pl.program_id(1)
    if windowed:
        start = pl.multiple_of(n * SWA_TR, SWA_TR)
        win = pl.ds(start, 3 * SWA_TR)
        r_i = lax.broadcasted_iota(jnp.int32, (2 * SWA_TR, 3 * SWA_TR), 0) & (SWA_TR - 1)
        c_i = lax.broadcasted_iota(jnp.int32, (2 * SWA_TR, 3 * SWA_TR), 1)
        kpos = (n - 1) * SWA_TR + c_i
        diff = SWA_TR + r_i - c_i
        valid2 = (kpos >= 0) & (kpos < t_total) & (diff <= SWA_WINDOW) & (diff >= -SWA_WINDOW)
    else:
        win = slice(None)
    def normed_queries(cb):
        sl = slice(128 * cb, 128 * cb + 128)
        qb = q_ref[:, sl]
        r = _group_rms_scale(qb, pair_ref)
        qn = (qb * r) * g_ref[...]
        if rope:
            qn = qn * c_ref[...] + ((qp_ref[:, sl] * r) * gp_ref[...]) * s_ref[...]
        return (qn * (SWA_SCALE * LOG2_E)).astype(BF16)

    tq = q_ref.shape[0]
    low_lanes = lax.broadcasted_iota(jnp.int32, (2 * tq, 128), 1) < SWA_HEAD_DIM

    lk = 3 * SWA_TR if windowed else k_ref.shape[1]
    lc = kc_ref.shape[1] if has_ctx else 0
    row_blk = lax.broadcasted_iota(jnp.int32, (2 * tq, 1), 0) // tq
    for j in range(SWA_KV_HEADS):
        q2 = jnp.concatenate([normed_queries(2 * j), normed_queries(2 * j + 1)], axis=0)
        k_parts, v_parts = [], []
        for e in range(2):
            k_parts.append(k_ref[2 * j + e, win, :])
            v_parts.append(v_ref[2 * j + e, win, :])
            if has_ctx:
                k_parts.append(kc_ref[2 * j + e])
                v_parts.append(vc_ref[2 * j + e])
        s = lax.dot_general(q2, jnp.concatenate(k_parts, axis=0), NT_DIMS, preferred_element_type=F32)
        p_parts, inv = [], []
        for e in range(2):
            off = e * (lk + lc)
            sink = jnp.where(row_blk == 0, sink_ref[layer, 4 * j + e], sink_ref[layer, 4 * j + 2 + e]) * LOG2_E
            s_loc = s[:, off:off + lk]
            if windowed:
                s_loc = jnp.where(valid2, s_loc, SWA_NEG)
            m = jnp.maximum(jnp.max(s_loc, axis=-1, keepdims=True), sink)
            if has_ctx:
                s_ctx = s[:, off + lk:off + lk + lc]
                m = jnp.maximum(m, jnp.max(s_ctx, axis=-1, keepdims=True))
            p = jnp.exp2(s_loc - m)
            den = jnp.sum(p, axis=-1, keepdims=True) + jnp.exp2(sink - m)
            p_parts.append(p.astype(BF16))
            if has_ctx:
                pc = jnp.exp2(s_ctx - m)
                den = den + jnp.sum(pc, axis=-1, keepdims=True)
                p_parts.append(pc.astype(BF16))
            inv.append(1.0 / den)
        o = jnp.dot(jnp.concatenate(p_parts, axis=1), jnp.concatenate(v_parts, axis=0), preferred_element_type=F32)
        o = o * jnp.where(low_lanes, inv[0], inv[1])
        o_ref[:, 256 * j:256 * j + 128] = o[:tq].astype(BF16)
        o_ref[:, 256 * j + 128:256 * j + 256] = o[tq:].astype(BF16)


def _swa_attn(sink, proj, rope_ins, norm_ins, k, v, ctx_kv, batch, t, layer):
    rope = rope_ins is not None
    has_ctx = ctx_kv is not None
    tq = SWA_TR if has_ctx else 2 * SWA_TR
    nq = t // tq
    ins = [sink, proj]
    specs = [pl.BlockSpec(memory_space=pltpu.SMEM),
             pl.BlockSpec((tq, SWA_WIDTH), lambda b, i: (b * nq + i, COL_SQ // SWA_WIDTH))]
    if rope:
        ctab, stab = rope_ins
        tab_spec = pl.BlockSpec((tq, 128), lambda b, i: (i, 0))
        ins += [proj, ctab, stab]
        specs += [pl.BlockSpec((tq, SWA_WIDTH), lambda b, i: (b * nq + i, COL_SQ_PERM // SWA_WIDTH)), tab_spec, tab_spec]
    ins += list(norm_ins)
    specs += [pl.BlockSpec((1, 128), lambda b, i: (0, 0)), pl.BlockSpec((1, 128), lambda b, i: (0, 0)),
              pl.BlockSpec((256, 128), lambda b, i: (0, 0))]
    lk = k.shape[2]
    ins += [k, v]
    specs += [pl.BlockSpec((None, 8, lk, 128), lambda b, i: (b, 0, 0, 0))] * 2
    if has_ctx:
        lc = ctx_kv[0].shape[2]
        ins += list(ctx_kv)
        specs += [pl.BlockSpec((None, 8, lc, 128), lambda b, i: (b, 0, 0, 0))] * 2
    return pl.pallas_call(
        functools.partial(_swa_attn_kernel, rope=rope, windowed=has_ctx, has_ctx=has_ctx, t_total=t, layer=layer),
        grid=(batch, nq), in_specs=specs,
        out_specs=pl.BlockSpec((tq, SWA_WIDTH), lambda b, i: (b * nq + i, 0)),
        out_shape=jax.ShapeDtypeStruct((batch * t, SWA_WIDTH), BF16),
        compiler_params=pltpu.CompilerParams(dimension_semantics=("arbitrary", "arbitrary")),
        name="swa_attn",
    )(*ins)


def _swa_branch(proj, batch, t, layer, g_qn_l, g_kn_l, sink, rope, ctx):
    perm, _ = _rope_partner()
    pair = _pair_sum_matrix()
    tile2 = lambda g: jnp.concatenate([g, g])[None]
    nt = t // SWA_TR
    if rope is None:
        k_spec = pl.BlockSpec((SWA_TR, 256), lambda b, i: (b * nt + i, COL_SK // 256))
        v_spec = pl.BlockSpec((SWA_TR, 256), lambda b, i: (b * nt + i, COL_SV // 256))
        k, v, k_n, v_raw = _swa_kv(proj, k_spec, proj, v_spec, None, (tile2(g_kn_l), tile2(g_kn_l[perm]), pair),
                                   batch, t, layer, emit=True, pad_blocks=False)
        o = _swa_attn(sink, proj, None, (tile2(g_qn_l), tile2(g_qn_l[perm]), pair), k, v, None, batch, t, layer)
        return o, k_n, v_raw
    c64, s64 = rope
    ctab, stab = jnp.concatenate([c64, c64], axis=1), jnp.concatenate([s64, s64], axis=1)
    row = lambda b, i: b * nt + jnp.clip(i - 1, 0, nt - 1)
    k_spec = pl.BlockSpec((SWA_TR, 256), lambda b, i: (row(b, i), COL_SK // 256))
    v_spec = pl.BlockSpec((SWA_TR, 256), lambda b, i: (row(b, i), COL_SV // 256))
    kp_spec = pl.BlockSpec((SWA_TR, 256), lambda b, i: (row(b, i), COL_SK_PERM // 256))
    k, v = _swa_kv(proj, k_spec, proj, v_spec, (proj, kp_spec, ctab, stab), (tile2(g_kn_l), tile2(g_kn_l[perm]), pair),
                   batch, t, layer, emit=False, pad_blocks=True)
    ctx_k, ctx_v = ctx
    lc = ctx_k.shape[2]
    c_spec = pl.BlockSpec((None, None, SWA_TR, 256), lambda b, i: (b, layer, i, 0))
    kc, vc = _swa_kv(ctx_k, c_spec, ctx_v, c_spec, None, None, batch, lc, layer, emit=False, pad_blocks=False)
    o = _swa_attn(sink, proj, (ctab, stab), (tile2(g_qn_l), tile2(g_qn_l[perm]), pair), k, v, (kc, vc), batch, t, layer)
    return o, None, None


GLA_LEVELS = (64, 32, 16, 8, 4, 2, 1)
GLA_MXU_LEVELS = (2, 1)
GLA_ROWSETS = 1 + len(GLA_MXU_LEVELS)


def _gla_sum_matrix(backward):
    c = GLA_CHUNK
    t = np.arange(c)[:, None]
    j = np.arange(c)[None, :]
    sets = [j <= t]
    for g in GLA_MXU_LEVELS:
        e = (t // (2 * g)) * 2 * g + g - 1
        upper = (t // g) % 2 == 1
        sets.append(np.where(upper, (j > e) & (j <= t), (j > t) & (j <= e)))
    n = np.concatenate(sets, axis=0).astype(np.float32)
    if backward:
        n = n.reshape(GLA_ROWSETS, c, c)[:, ::-1, ::-1].reshape(GLA_ROWSETS * c, c)
    return jnp.asarray(np.concatenate([n, n, n], axis=1), BF16)


def _gla_kernel(*refs, has_s0):
    it = iter(refs)
    srcs = [tuple(next(it) for _ in range(4)) for _ in range(2)]
    nmat = (next(it), next(it))
    wg = (next(it), next(it))
    bg = (next(it), next(it))
    s0_ref = next(it) if has_s0 else None
    o_refs = (next(it), next(it))
    sfin_ref = next(it)
    s_ref = next(it)
    i = pl.program_id(1)
    c = GLA_CHUNK

    @pl.when(i == 0)
    def _():
        s_ref[...] = s0_ref[...] if has_s0 else jnp.zeros_like(s_ref)

    row = lax.broadcasted_iota(jnp.int32, (c, c), 0)
    col = lax.broadcasted_iota(jnp.int32, (c, c), 1)
    diag = row == col
    log2 = lambda g: int(g).bit_length() - 1
    second_half = [((row >> log2(g)) & 1) == 1 for g in GLA_LEVELS]
    same_block = [(row >> (log2(g) + 1)) == (col >> (log2(g) + 1)) for g in GLA_LEVELS]
    for d in range(2):
        q_ref, k_ref, v_ref, gg_ref = srcs[d]
        z = jnp.dot(gg_ref[...].astype(BF16), wg[d][...], preferred_element_type=F32) + bg[d][...]
        la = (jnp.minimum(z, 0.0) - jnp.log1p(jnp.exp(-jnp.abs(z)))) * (1.0 / GLA_GATE_NORM)
        hi = la.astype(BF16)
        r1 = la - hi.astype(F32)
        mid = r1.astype(BF16)
        lo = (r1 - mid.astype(F32)).astype(BF16)
        ex_all = jnp.dot(nmat[d][...], jnp.concatenate([hi, mid, lo], axis=0), preferred_element_type=F32)
        for h in range(GLA_HEADS):
            sl = slice(GLA_DK * h, GLA_DK * h + GLA_DK)
            vs = slice(GLA_DV * h, GLA_DV * h + GLA_DV)
            ex = ex_all[:, sl]
            q = q_ref[:, sl] * (GLA_DK ** -0.5)
            k = k_ref[:, sl]
            v = v_ref[:, vs].astype(BF16)
            b = ex[0:c]
            eb = jnp.exp(b)
            b_exit = b[c - 1:c] if d == 0 else b[0:1]
            ek = jnp.exp(b_exit - b)
            s_old = s_ref[d, h]
            o = jnp.dot((q * eb).astype(BF16), s_old.astype(BF16), preferred_element_type=F32)
            a = jnp.where(diag, jnp.sum(q * k, axis=-1, keepdims=True), 0.0)
            for li, g in enumerate(GLA_LEVELS):
                if g in GLA_MXU_LEVELS:
                    r = 1 + GLA_MXU_LEVELS.index(g)
                    eg = jnp.exp(ex[r * c:(r + 1) * c])
                else:
                    b3 = b.reshape(c // (2 * g), 2 * g, GLA_DK)
                    edge = g - 1 if d == 0 else g
                    b_edge = jnp.broadcast_to(b3[:, edge:edge + 1, :], b3.shape).reshape(c, GLA_DK)
                    eg = jnp.exp(-jnp.abs(b - b_edge))
                qe, ke = q * eg, k * eg
                late = second_half[li]
                qg = (jnp.where(late, qe, 0.0) if d == 0 else jnp.where(late, 0.0, qe)).astype(BF16)
                kg = (jnp.where(late, 0.0, ke) if d == 0 else jnp.where(late, ke, 0.0)).astype(BF16)
                ag = lax.dot_general(qg, kg, NT_DIMS, preferred_element_type=F32)
                if 2 * g < c:
                    ag = jnp.where(same_block[li], ag, 0.0)
                a = a + ag
            o = o + jnp.dot(a.astype(BF16), v, preferred_element_type=F32)
            o_refs[d][:, vs] = o
            a_col = eb.T[:, c - 1:c] if d == 0 else eb.T[:, 0:1]
            kt = (k * ek).T.astype(BF16)
            s_ref[d, h] = a_col * s_old + jnp.dot(kt, v, preferred_element_type=F32)

    @pl.when(i == pl.num_programs(1) - 1)
    def _():
        sfin_ref[...] = s_ref[...]


def _gla(proj, batch, t, layer, w_gf, b_gf, w_gb, b_gb, s0):
    c = GLA_CHUNK
    n = t // c
    fwd = lambda b, i: b * n + i
    bwd = lambda b, i: b * n + (n - 1 - i)
    ins, specs = [], []
    for rowf in (fwd, bwd):
        ins += [proj] * 4
        specs += [pl.BlockSpec((c, 512), lambda b, i, rowf=rowf: (rowf(b, i), COL_GQ // 512)),
                  pl.BlockSpec((c, 512), lambda b, i, rowf=rowf: (rowf(b, i), COL_GK // 512)),
                  pl.BlockSpec((c, GLA_WIDTH), lambda b, i, rowf=rowf: (rowf(b, i), COL_GV // GLA_WIDTH)),
                  pl.BlockSpec((c, 128), lambda b, i, rowf=rowf: (rowf(b, i), COL_GG // 128))]
    const = lambda shape: pl.BlockSpec(shape, lambda b, i: (0,) * len(shape))
    pad_w = lambda w, off: jnp.zeros((128, 512), F32).at[off:off + GLA_GATE_RANK].set(w).astype(BF16)
    ins += [_gla_sum_matrix(False), _gla_sum_matrix(True), pad_w(w_gf, 0), pad_w(w_gb, GLA_GATE_RANK), b_gf[None], b_gb[None]]
    specs += [const((GLA_ROWSETS * c, 3 * c))] * 2 + [const((128, 512))] * 2 + [const((1, 512))] * 2
    state_spec = pl.BlockSpec((None, 2, GLA_HEADS, GLA_DK, GLA_DV), lambda b, i: (b, 0, 0, 0, 0))
    if s0 is not None:
        ins.append(s0)
        specs.append(pl.BlockSpec((None, None, 2, GLA_HEADS, GLA_DK, GLA_DV), lambda b, i: (b, layer, 0, 0, 0, 0)))
    return pl.pallas_call(
        functools.partial(_gla_kernel, has_s0=s0 is not None),
        grid=(batch, n), in_specs=specs,
        out_specs=[pl.BlockSpec((c, GLA_WIDTH), lambda b, i: (fwd(b, i), 0)),
                   pl.BlockSpec((c, GLA_WIDTH), lambda b, i: (bwd(b, i), 0)), state_spec],
        out_shape=[jax.ShapeDtypeStruct((batch * t, GLA_WIDTH), F32)] * 2
        + [jax.ShapeDtypeStruct((batch, 2, GLA_HEADS, GLA_DK, GLA_DV), F32)],
        scratch_shapes=[pltpu.VMEM((2, GLA_HEADS, GLA_DK, GLA_DV), F32)],
        compiler_params=pltpu.CompilerParams(dimension_semantics=("arbitrary", "arbitrary")),
        name="gla",
    )(*ins)


TAIL_TM = 512
TAIL_TN = 512
TAIL_ACC_COLS = 512


def _tail_kernel(om_ref, os_ref, of_ref, ob_ref, gout_ref, gm_ref, gg_ref, gs_ref, wm_ref, wg_ref, ws_ref, wo_ref,
                 x_ref, gate_ref, gn_ref, o_ref, og_ref, *, layer, base_row, rows_per_cond):
    i = pl.program_id(0)
    j = pl.program_id(1)
    row = _cond_row(i, x_ref.shape[0], base_row, rows_per_cond)

    @pl.when(j == 0)
    def _():
        g = gn_ref[pl.ds(layer, 1), :]
        for h in range(GLA_HEADS):
            vs = slice(GLA_DV * h, GLA_DV * h + GLA_DV)
            o = of_ref[:, vs] + ob_ref[:, vs]
            y = (o * lax.rsqrt(jnp.mean(o * o, axis=-1, keepdims=True) + NORM_EPS)) * g
            gate = gout_ref[:, vs]
            og_ref[:, vs] = (y * (gate * jax.nn.sigmoid(gate))).astype(BF16)
        o_ref[...] = jnp.zeros_like(o_ref)

    def branch(o_r, gate_r, w_r):
        return jax.nn.sigmoid(gate_r[...]) * jnp.dot(o_r[...], w_r[...], preferred_element_type=F32)

    merged = (branch(om_ref, gm_ref, wm_ref) + branch(og_ref, gg_ref, wg_ref) + branch(os_ref, gs_ref, ws_ref)).astype(BF16)
    for n in range(0, D_MODEL, TAIL_ACC_COLS):
        o_ref[:, n:n + TAIL_ACC_COLS] += jnp.dot(merged, wo_ref[:, n:n + TAIL_ACC_COLS], preferred_element_type=F32)

    @pl.when(j == pl.num_programs(1) - 1)
    def _():
        o_ref[...] = x_ref[...] + gate_ref[pl.ds(row, 1), :] * o_ref[...]


def _mixer_tail(o_mla, o_swa, o_f, o_b, proj, x, mods, g_gla_out, w_br_mla, w_br_gla, w_br_swa, w_out,
                layer, base_row, rows_per_cond):
    m = x.shape[0]
    row_blk = lambda width, col=0: pl.BlockSpec((TAIL_TM, width), lambda i, j: (i, col // width))
    gate_spec = lambda col: pl.BlockSpec((TAIL_TM, TAIL_TN), lambda i, j: (i, col // TAIL_TN + j))
    w_spec = pl.BlockSpec((None, 1024, TAIL_TN), lambda i, j: (layer, 0, j))
    est = (2 * (2 * TAIL_TM * 1024 * 2 + 3 * TAIL_TM * 1024 * 4 + 3 * TAIL_TM * TAIL_TN * 4 + 3 * 1024 * TAIL_TN * 2
                + TAIL_TN * D_MODEL * 2 + 2 * TAIL_TM * D_MODEL * 4) + TAIL_TM * 1024 * 2)
    return pl.pallas_call(
        functools.partial(_tail_kernel, layer=layer, base_row=base_row, rows_per_cond=rows_per_cond),
        grid=(m // TAIL_TM, D_MODEL // TAIL_TN),
        in_specs=[row_blk(MLA_WIDTH), row_blk(SWA_WIDTH), row_blk(GLA_WIDTH), row_blk(GLA_WIDTH),
                  row_blk(GLA_WIDTH, COL_GOUT),
                  gate_spec(COL_GATE_MLA), gate_spec(COL_GATE_GLA), gate_spec(COL_GATE_SWA),
                  w_spec, w_spec, w_spec,
                  pl.BlockSpec((None, TAIL_TN, D_MODEL), lambda i, j: (layer, j, 0)),
                  pl.BlockSpec((TAIL_TM, D_MODEL), lambda i, j: (i, 0)),
                  pl.BlockSpec((None, None, 8, D_MODEL), lambda i, j: (layer, 5, 0, 0)),
                  pl.BlockSpec((DEPTH, GLA_DV), lambda i, j: (0, 0))],
        out_specs=pl.BlockSpec((TAIL_TM, D_MODEL), lambda i, j: (i, 0)),
        out_shape=jax.ShapeDtypeStruct((m, D_MODEL), F32),
        scratch_shapes=[pltpu.VMEM((TAIL_TM, GLA_WIDTH), BF16)],
        compiler_params=pltpu.CompilerParams(
            dimension_semantics=("arbitrary", "arbitrary"), vmem_limit_bytes=_vmem_limit(est)),
        name="mixer_tail",
    )(o_mla, o_swa, o_f, o_b, proj, proj, proj, proj, w_br_mla, w_br_gla, w_br_swa, w_out, x, mods, g_gla_out)


def _trunk_layer(x, mods, layer, w, group):
    batch, t, base_row, rows_per_cond, rope, ctx = group
    x = _ffn(x, mods, layer, 0, w['g_norm1'], w['w_ff1_gu'], w['w_ff1_down'], base_row, rows_per_cond)
    ncols = PROJ_COLS_PROMPT if rope is None else PROJ_COLS_SAMPLE
    proj = _in_proj(x, mods, layer, w['g_norm2'], w['w_in_packed'], ncols, base_row, rows_per_cond)
    mla_ctx = swa_ctx = s0 = None
    if ctx is not None:
        mla_ctx, swa_ctx, s0 = ctx[:2], ctx[2:4], ctx[4]
    o_mla, ckv_n = _mla_branch(proj, batch, t, layer, w['mla_packed'][layer], w['w_mla_ukv'], w['g_mla_q'], w['g_mla_kv'],
                               rope, mla_ctx)
    o_swa, k_n, v_raw = _swa_branch(proj, batch, t, layer, w['g_swa_qn'][layer], w['g_swa_kn'][layer], w['swa_sink'],
                                    rope, swa_ctx)
    o_f, o_b, s_fin = _gla(proj, batch, t, layer, w['w_gla_gf'][layer], w['b_gla_gf'][layer],
                           w['w_gla_gb'][layer], w['b_gla_gb'][layer], s0)
    x = _mixer_tail(o_mla, o_swa, o_f, o_b, proj, x, mods, w['g_gla_out'], w['w_br_mla'], w['w_br_gla'], w['w_br_swa'],
                    w['w_out'], layer, base_row, rows_per_cond)
    x = _ffn(x, mods, layer, 6, w['g_norm3'], w['w_ff2_gu'], w['w_ff2_down'], base_row, rows_per_cond)
    new_ctx = None
    if ctx is None:
        kpe = proj[:, COL_KPE:COL_KPE + MLA_ROPE].reshape(batch, t, MLA_ROPE)
        new_ctx = (ckv_n, kpe, k_n.reshape(batch, t, SWA_KV_HEADS, SWA_HEAD_DIM),
                   v_raw.reshape(batch, t, SWA_KV_HEADS, SWA_HEAD_DIM), s_fin)
    return x, new_ctx


def kernel(x_prompt, x_sample, cache_mla_ckv, cache_mla_kpe, cache_swa_k, cache_swa_v, state_gla,
           c, c_ctx, w_ada, b_ada, g_norm1, g_norm2, g_norm3,
           w_ff1_gu, w_ff1_down, w_ff2_gu, w_ff2_down, w_in,
           g_mla_q, w_mla_uq, g_mla_kv, w_mla_ukv, g_mla_qn, g_mla_kn,
           w_gla_gf, b_gla_gf, w_gla_gb, b_gla_gb, g_gla_out,
           g_swa_qn, g_swa_kn, swa_sink, w_br_mla, w_br_gla, w_br_swa, w_out):
    bp, tp, _ = x_prompt.shape
    bs, ts, _ = x_sample.shape
    assert bs + 1 <= 8, "conditioning rows are packed into one 8-row tile"
    cond8 = jnp.zeros((8, D_MODEL), F32).at[0].set(c_ctx).at[1:1 + bs].set(c)
    mods = _adaln(cond8, w_ada, b_ada)
    w = dict(g_norm1=g_norm1, g_norm2=g_norm2, g_norm3=g_norm3,
             w_ff1_gu=w_ff1_gu, w_ff1_down=w_ff1_down, w_ff2_gu=w_ff2_gu, w_ff2_down=w_ff2_down,
             w_in_packed=_pack_w_in(w_in),
             mla_packed=[_pack_mla(w_mla_uq[l], g_mla_qn[l], g_mla_kn[l]) for l in range(DEPTH)],
             w_mla_ukv=w_mla_ukv, g_mla_q=g_mla_q, g_mla_kv=g_mla_kv,
             g_swa_qn=g_swa_qn, g_swa_kn=g_swa_kn, swa_sink=swa_sink,
             w_gla_gf=w_gla_gf, b_gla_gf=b_gla_gf, w_gla_gb=w_gla_gb, b_gla_gb=b_gla_gb, g_gla_out=g_gla_out,
             w_br_mla=w_br_mla.astype(BF16), w_br_gla=w_br_gla.astype(BF16), w_br_swa=w_br_swa.astype(BF16),
             w_out=w_out.astype(BF16))
    past = cache_mla_ckv.shape[2]
    ctx = (cache_mla_ckv, jnp.pad(cache_mla_kpe, ((0, 0), (0, 0), (0, 0), (0, 128 - MLA_ROPE))),
           cache_swa_k.reshape(bs, DEPTH, past, SWA_KV_HEADS * SWA_HEAD_DIM),
           cache_swa_v.reshape(bs, DEPTH, past, SWA_KV_HEADS * SWA_HEAD_DIM), state_gla)
    group_p = (bp, tp, 0, bp * tp, None, None)
    group_s = (bs, ts, 1, ts, _rope_tables(ts), ctx)
    y_p = x_prompt.reshape(bp * tp, D_MODEL)
    y_s = x_sample.reshape(bs * ts, D_MODEL)
    new = []
    for l in range(DEPTH):
        y_p, new_ctx = _trunk_layer(y_p, mods, l, w, group_p)
        new.append(new_ctx)
        y_s, _ = _trunk_layer(y_s, mods, l, w, group_s)
    stacked = tuple(jnp.stack([new[l][k] for l in range(DEPTH)], axis=1) for k in range(5))
    return (y_p.reshape(bp, tp, D_MODEL), y_s.reshape(bs, ts, D_MODEL)) + stacked
```

```python
import functools

import numpy as np
import jax
import jax.numpy as jnp
from jax import lax
from jax.experimental import pallas as pl
from jax.experimental.pallas import tpu as pltpu

F32 = jnp.float32
BF16 = jnp.bfloat16

D_MODEL = 2048
DEPTH = 2
GRID_W = 64
ROPE_BASE = 10000.0
NORM_EPS = 1e-6
MLA_HEADS = 8
MLA_Q_LORA = 512
MLA_KV_LORA = 512
MLA_NOPE = 128
MLA_ROPE = 64
MLA_V = 128
MLA_QK = MLA_NOPE + MLA_ROPE
MLA_SCALE = MLA_QK ** -0.5
MLA_WIDTH = MLA_HEADS * MLA_V
MLA_HEAD_PAD = 256
GLA_HEADS = 4
GLA_DK = 128
GLA_DV = 256
GLA_GATE_RANK = 16
GLA_GATE_NORM = 16.0
GLA_WIDTH = GLA_HEADS * GLA_DV
GLA_CHUNK = 128
SWA_HEADS = 16
SWA_KV_HEADS = 4
SWA_HEAD_DIM = 64
SWA_WINDOW = 128
SWA_SCALE = SWA_HEAD_DIM ** -0.5
SWA_WIDTH = SWA_HEADS * SWA_HEAD_DIM
D_FF = 5632
N_MOD = 9
IN_SPLITS = (MLA_Q_LORA, MLA_KV_LORA, MLA_ROPE,
             GLA_HEADS * GLA_DK, GLA_HEADS * GLA_DK, GLA_WIDTH, GLA_WIDTH, GLA_GATE_RANK, GLA_GATE_RANK,
             SWA_WIDTH, SWA_KV_HEADS * SWA_HEAD_DIM, SWA_KV_HEADS * SWA_HEAD_DIM,
             D_MODEL, D_MODEL, D_MODEL)

V7X_LANES = 128
V7X_SUBLANES = 8
V7X_VMEM_BYTES = 64 * 1024 * 1024
MIB = 1024 * 1024

NT_DIMS = (((1,), (1,)), ((), ()))
LOG2_E = 1.4426950408889634

COL_GATE_MLA = 0
COL_GATE_GLA = 2048
COL_GATE_SWA = 4096
COL_CQ = 6144
COL_CKV = 6656
COL_GQ = 7168
COL_GK = 7680
COL_GV = 8192
COL_GOUT = 9216
COL_SQ = 10240
COL_SK = 11264
COL_SV = 11520
COL_KPE = 11776
COL_GG = 11904
PROJ_COLS = 12288
PROJ_TN = 1024


def _vmem_limit(nbytes):
    return int(min(nbytes + 12 * MIB, V7X_VMEM_BYTES - 6 * MIB))


def _rope_partner(x):
    lane = lax.broadcasted_iota(jnp.int32, x.shape, 1)
    return jnp.where((lane & 31) < 16, pltpu.roll(x, 112, axis=1), pltpu.roll(x, 16, axis=1))


ADA_TN = 1024


def _adaln_kernel(cond_ref, w_ref, b_ref, o_ref):
    c = cond_ref[...]
    s = (c * jax.nn.sigmoid(c)).astype(BF16)
    o_ref[...] = jnp.dot(s, w_ref[...].astype(BF16), preferred_element_type=F32) + b_ref[...]


def _adaln(cond8, w_ada, b_ada):
    halves = D_MODEL // ADA_TN
    nj = N_MOD * halves
    return pl.pallas_call(
        _adaln_kernel,
        grid=(DEPTH, nj),
        in_specs=[
            pl.BlockSpec((8, D_MODEL), lambda l, j: (0, 0)),
            pl.BlockSpec((None, D_MODEL, ADA_TN), lambda l, j: (l, 0, j)),
            pl.BlockSpec((None, 1, ADA_TN), lambda l, j: (l, 0, j)),
        ],
        out_specs=pl.BlockSpec((None, None, 8, ADA_TN), lambda l, j: (l, j // halves, 0, j % halves)),
        out_shape=jax.ShapeDtypeStruct((DEPTH, N_MOD, 8, D_MODEL), F32),
        compiler_params=pltpu.CompilerParams(
            dimension_semantics=("arbitrary", "arbitrary"),
            vmem_limit_bytes=_vmem_limit(2 * D_MODEL * ADA_TN * 4)),
        name="adaln",
    )(cond8, w_ada, b_ada.reshape(DEPTH, 1, N_MOD * D_MODEL))


NORM_ROWS = 64


def _modulated_norm_to(h_ref, x_ref, g, shift, scale):
    gain = g * (1.0 + scale)

    def body(c, carry):
        r0 = pl.multiple_of(c * NORM_ROWS, NORM_ROWS)
        x = x_ref[pl.ds(r0, NORM_ROWS), :]
        ms = jnp.mean(x * x, axis=-1, keepdims=True)
        h_ref[pl.ds(r0, NORM_ROWS), :] = ((x * lax.rsqrt(ms + NORM_EPS)) * gain + shift).astype(BF16)
        return carry

    lax.fori_loop(0, x_ref.shape[0] // NORM_ROWS, body, 0, unroll=2)


def _cond_row(i, tm, base_row, rows_per_cond):
    return base_row + (i * tm) // rows_per_cond


FFN_TM = 1024
FFN_TF = 256
FFN_TN = 512


def _ffn_kernel(x_ref, shift_ref, scale_ref, gate_ref, g_ref, wg_ref, wu_ref, wd_ref, o_ref, h_ref,
                *, layer, base_row, rows_per_cond):
    i = pl.program_id(0)
    j = pl.program_id(1)
    row = _cond_row(i, x_ref.shape[0], base_row, rows_per_cond)

    @pl.when(j == 0)
    def _():
        _modulated_norm_to(h_ref, x_ref, g_ref[pl.ds(layer, 1), :],
                           shift_ref[pl.ds(row, 1), :], scale_ref[pl.ds(row, 1), :])
        o_ref[...] = jnp.zeros_like(o_ref)

    h = h_ref[...]
    a = jnp.dot(h, wg_ref[...].astype(BF16), preferred_element_type=F32)
    u = jnp.dot(h, wu_ref[...].astype(BF16), preferred_element_type=F32)
    act = (a * jax.nn.sigmoid(a) * u).astype(BF16)
    for n in range(0, D_MODEL, FFN_TN):
        o_ref[:, n:n + FFN_TN] += jnp.dot(act, wd_ref[:, n:n + FFN_TN].astype(BF16), preferred_element_type=F32)

    @pl.when(j == pl.num_programs(1) - 1)
    def _():
        o_ref[...] = x_ref[...] + (0.5 * gate_ref[pl.ds(row, 1), :]) * o_ref[...]


def _ffn(x, mods, layer, first_mod, g_norm, w_gu, w_down, base_row, rows_per_cond):
    m = x.shape[0]
    nf = D_FF // FFN_TF
    mod_spec = lambda k: pl.BlockSpec((None, None, 8, D_MODEL), lambda i, j: (layer, first_mod + k, 0, 0))
    est = (2 * FFN_TM * D_MODEL * 4 * 2 + FFN_TM * D_MODEL * 2
           + 2 * 3 * D_MODEL * FFN_TF * w_gu.dtype.itemsize)
    return pl.pallas_call(
        functools.partial(_ffn_kernel, layer=layer, base_row=base_row, rows_per_cond=rows_per_cond),
        grid=(m // FFN_TM, nf),
        in_specs=[
            pl.BlockSpec((FFN_TM, D_MODEL), lambda i, j: (i, 0)),
            mod_spec(0), mod_spec(1), mod_spec(2),
            pl.BlockSpec((DEPTH, D_MODEL), lambda i, j: (0, 0)),
            pl.BlockSpec((None, D_MODEL, FFN_TF), lambda i, j: (layer, 0, j)),
            pl.BlockSpec((None, D_MODEL, FFN_TF), lambda i, j: (layer, 0, j + nf)),
            pl.BlockSpec((None, FFN_TF, D_MODEL), lambda i, j: (layer, j, 0)),
        ],
        out_specs=pl.BlockSpec((FFN_TM, D_MODEL), lambda i, j: (i, 0)),
        out_shape=jax.ShapeDtypeStruct((m, D_MODEL), F32),
        scratch_shapes=[pltpu.VMEM((FFN_TM, D_MODEL), BF16)],
        compiler_params=pltpu.CompilerParams(
            dimension_semantics=("arbitrary", "arbitrary"), vmem_limit_bytes=_vmem_limit(est)),
        name="ffn",
    )(x, mods, mods, mods, g_norm, w_gu, w_gu, w_down)


PROJ_TM = 1024


IN_COLS = sum(IN_SPLITS)
_SRC = dict(zip(('cq', 'ckv', 'kpe', 'gq', 'gk', 'gv', 'gout', 'ggf', 'ggb', 'sq', 'sk', 'sv', 'gate_mla'),
                np.concatenate([[0], np.cumsum(IN_SPLITS)]).tolist()))
PACK_ROWS = 512
PACK_TILES = PROJ_COLS // PACK_ROWS


def _pack_plan():
    small, a, b = [], [], []
    for dst, src, width in ((COL_GATE_MLA, _SRC['gate_mla'], 3 * D_MODEL), (COL_CQ, _SRC['cq'], 1024),
                            (COL_GQ, _SRC['gq'], 3072), (COL_SQ, _SRC['sq'], 1536)):
        assert dst == len(a) * PACK_ROWS and width % PACK_ROWS == 0
        for t in range(width // PACK_ROWS):
            small.append(0); a.append(src + t * PACK_ROWS); b.append(_SRC['ggf'])
    assert COL_KPE == len(a) * PACK_ROWS and COL_GG == COL_KPE + 128
    small.append(1); a.append(_SRC['kpe']); b.append(_SRC['ggf'])
    assert len(a) == PACK_TILES
    return np.asarray([small, a, b], np.int32)


def _pack_kernel(plan_ref, a3_ref, b3_ref, o_ref):
    is_small = plan_ref[0, pl.program_id(1)] == 1
    a_ref, b_ref = a3_ref.at[0], b3_ref.at[0]

    @pl.when(jnp.logical_not(is_small))
    def _():
        o_ref[...] = a_ref[...].astype(BF16)

    @pl.when(is_small)
    def _():
        o_ref[...] = jnp.zeros_like(o_ref)
        o_ref[0:MLA_ROPE, :] = a_ref[0:MLA_ROPE, :].astype(BF16)
        o_ref[128:128 + 2 * GLA_GATE_RANK, :] = b_ref[0:2 * GLA_GATE_RANK, :].astype(BF16)


def _pack_w_in(w_in):
    w_t = jnp.swapaxes(w_in, 1, 2)
    window = lambda which: pl.BlockSpec((pl.Element(1), pl.Element(PACK_ROWS), pl.Element(D_MODEL)),
                                        lambda l, j, plan: (l, pl.multiple_of(plan[which, j], 32), 0))
    return pl.pallas_call(
        _pack_kernel,
        grid_spec=pltpu.PrefetchScalarGridSpec(
            num_scalar_prefetch=1, grid=(DEPTH, PACK_TILES),
            in_specs=[window(1), window(2)],
            out_specs=pl.BlockSpec((None, PACK_ROWS, D_MODEL), lambda l, j, plan: (l, j, 0))),
        out_shape=jax.ShapeDtypeStruct((DEPTH, PROJ_COLS, D_MODEL), BF16),
        compiler_params=pltpu.CompilerParams(
            dimension_semantics=("arbitrary", "arbitrary"),
            vmem_limit_bytes=_vmem_limit(2 * PACK_ROWS * D_MODEL * (4 + 4 + 2))),
        name="pack_w_in",
    )(jnp.asarray(_pack_plan()), w_t, w_t)


def _proj_kernel(x_ref, shift_ref, scale_ref, g_ref, w_ref, o_ref, h_ref, *, layer, base_row, rows_per_cond):
    i = pl.program_id(0)
    j = pl.program_id(1)
    row = _cond_row(i, x_ref.shape[0], base_row, rows_per_cond)

    @pl.when(j == 0)
    def _():
        _modulated_norm_to(h_ref, x_ref, g_ref[pl.ds(layer, 1), :],
                           shift_ref[pl.ds(row, 1), :], scale_ref[pl.ds(row, 1), :])

    o_ref[...] = lax.dot_general(h_ref[...], w_ref[...], NT_DIMS, preferred_element_type=F32)


def _in_proj(x, mods, layer, g_norm2, w_packed, base_row, rows_per_cond):
    m = x.shape[0]
    ncols = w_packed.shape[1]
    mod_spec = lambda k: pl.BlockSpec((None, None, 8, D_MODEL), lambda i, j: (layer, 3 + k, 0, 0))
    est = 2 * PROJ_TM * D_MODEL * 4 + PROJ_TM * D_MODEL * 2 + 2 * D_MODEL * PROJ_TN * 2 + 2 * PROJ_TM * PROJ_TN * 4
    return pl.pallas_call(
        functools.partial(_proj_kernel, layer=layer, base_row=base_row, rows_per_cond=rows_per_cond),
        grid=(m // PROJ_TM, ncols // PROJ_TN),
        in_specs=[
            pl.BlockSpec((PROJ_TM, D_MODEL), lambda i, j: (i, 0)),
            mod_spec(0), mod_spec(1),
            pl.BlockSpec((DEPTH, D_MODEL), lambda i, j: (0, 0)),
            pl.BlockSpec((None, PROJ_TN, D_MODEL), lambda i, j: (layer, j, 0)),
        ],
        out_specs=pl.BlockSpec((PROJ_TM, PROJ_TN), lambda i, j: (i, j)),
        out_shape=jax.ShapeDtypeStruct((m, ncols), F32),
        scratch_shapes=[pltpu.VMEM((PROJ_TM, D_MODEL), BF16)],
        compiler_params=pltpu.CompilerParams(
            dimension_semantics=("arbitrary", "arbitrary"), vmem_limit_bytes=_vmem_limit(est)),
        name="in_proj",
    )(x, mods, mods, g_norm2, w_packed)


def _rope_tables(t):
    pos = jnp.arange(t)
    inv_freq = ROPE_BASE ** (-jnp.arange(16, dtype=F32) / 16)

    def cs(p):
        ang = p.astype(F32)[:, None] * inv_freq[None, :]
        return jnp.concatenate([jnp.cos(ang)] * 2, axis=1), jnp.concatenate([jnp.sin(ang)] * 2, axis=1)

    cr, sr = cs(pos // GRID_W)
    cc, sc = cs(pos % GRID_W)
    sign = np.where(np.arange(64) % 32 < 16, -1.0, 1.0).astype(np.float32)
    return jnp.concatenate([cr, cc], axis=1), jnp.concatenate([sr, sc], axis=1) * sign[None, :]


MLA_TR = 256
MLA_TQ = 256


def _pack_mla(w_uq_l, g_qn_l, g_kn_l):
    w = w_uq_l.reshape(MLA_Q_LORA, MLA_HEADS, MLA_QK)
    z64 = jnp.zeros((MLA_Q_LORA, MLA_HEADS, 64), w.dtype)
    wq = jnp.concatenate([w, z64], axis=-1).reshape(MLA_Q_LORA, -1).astype(BF16)
    v64 = jnp.zeros((64,), F32)
    gq_full = jnp.concatenate([g_qn_l, v64])[None]
    gk_n = g_kn_l[:MLA_NOPE][None]
    gk_r = jnp.concatenate([g_kn_l[MLA_NOPE:], v64])[None]
    return wq, gq_full, gk_n, gk_r


def _mla_kv_kernel(*refs, layer, normalize, rope, emit_ckv):
    it = iter(refs)
    ckv_ref, kpe_ref = next(it), next(it)
    if rope:
        c_ref, s_ref = next(it), next(it)
    w_ref, gkv_ref, gn_ref, gr_ref = next(it), next(it), next(it), next(it)
    k_ref, v_ref = next(it), next(it)
    ckv = ckv_ref[...]
    if normalize:
        ckv = (ckv * lax.rsqrt(jnp.mean(ckv * ckv, axis=-1, keepdims=True) + NORM_EPS)) * gkv_ref[pl.ds(layer, 1), :]
    if emit_ckv:
        next(it)[...] = ckv
    kv = jnp.dot(ckv.astype(BF16), w_ref[...].astype(BF16), preferred_element_type=F32)
    kpe = kpe_ref[...]
    ss_pe = jnp.sum(kpe * kpe, axis=-1, keepdims=True)
    kr = kpe * gr_ref[...]
    if rope:
        kr = kr * c_ref[...] + _rope_partner(kr) * s_ref[...]
    for h in range(MLA_HEADS):
        kn = kv[:, 256 * h:256 * h + 128]
        r = lax.rsqrt((jnp.sum(kn * kn, axis=-1, keepdims=True) + ss_pe) * (1.0 / MLA_QK) + NORM_EPS)
        k_ref[:, 256 * h:256 * h + 128] = ((kn * r) * gn_ref[...]).astype(BF16)
        k_ref[:, 256 * h + 128:256 * h + 256] = (kr * r).astype(BF16)
        v_ref[:, 128 * h:128 * h + 128] = kv[:, 256 * h + 128:256 * h + 256].astype(BF16)


def _mla_kv(ckv_src, ckv_spec, kpe_src, kpe_spec, rope_tabs, w_ukv, g_kv, gk_n, gk_r,
            batch, rows, layer, normalize, emit_ckv):
    rope = rope_tabs is not None
    nt = rows // MLA_TR
    ins = [ckv_src, kpe_src]
    specs = [ckv_spec, kpe_spec]
    if rope:
        ins += list(rope_tabs)
        specs += [pl.BlockSpec((MLA_TR, 128), lambda b, t: (t, 0))] * 2
    ins += [w_ukv, g_kv, gk_n, gk_r]
    specs += [pl.BlockSpec((None, MLA_KV_LORA, 2048), lambda b, t: (layer, 0, 0)),
              pl.BlockSpec((DEPTH, MLA_KV_LORA), lambda b, t: (0, 0))] + [pl.BlockSpec((1, 128), lambda b, t: (0, 0))] * 2
    out_shape = [jax.ShapeDtypeStruct((batch, rows, MLA_HEADS * MLA_HEAD_PAD), BF16),
                 jax.ShapeDtypeStruct((batch, rows, MLA_WIDTH), BF16)]
    out_specs = [pl.BlockSpec((None, MLA_TR, MLA_HEADS * MLA_HEAD_PAD), lambda b, t: (b, t, 0)),
                 pl.BlockSpec((None, MLA_TR, MLA_WIDTH), lambda b, t: (b, t, 0))]
    if emit_ckv:
        out_shape.append(jax.ShapeDtypeStruct((batch, rows, MLA_KV_LORA), F32))
        out_specs.append(pl.BlockSpec((None, MLA_TR, MLA_KV_LORA), lambda b, t: (b, t, 0)))
    est = 2 * MLA_KV_LORA * 2048 * 4 + 4 * MLA_TR * 2048 * 4
    return pl.pallas_call(
        functools.partial(_mla_kv_kernel, layer=layer, normalize=normalize, rope=rope, emit_ckv=emit_ckv),
        grid=(batch, nt), in_specs=specs, out_specs=out_specs, out_shape=out_shape,
        compiler_params=pltpu.CompilerParams(
            dimension_semantics=("arbitrary", "arbitrary"), vmem_limit_bytes=_vmem_limit(est)),
        name="mla_kv",
    )(*ins)


def _mla_attn_kernel(*refs, layer, rope, has_ctx):
    it = iter(refs)
    cq_ref, wq_ref, gq_ref, gfull_ref = next(it), next(it), next(it), next(it)
    if rope:
        c_ref, s_ref = next(it), next(it)
    k_ref, v_ref = next(it), next(it)
    if has_ctx:
        kc_ref, vc_ref = next(it), next(it)
    o_ref = next(it)
    cq = cq_ref[...]
    ql = ((cq * lax.rsqrt(jnp.mean(cq * cq, axis=-1, keepdims=True) + NORM_EPS)) * gq_ref[pl.ds(layer, 1), :]).astype(BF16)
    q_raw = jnp.dot(ql, wq_ref[...], preferred_element_type=F32)
    for h in range(MLA_HEADS):
        sl = slice(256 * h, 256 * h + 256)
        vs = slice(128 * h, 128 * h + 128)
        q = q_raw[:, sl]
        r = lax.rsqrt(jnp.sum(q * q, axis=-1, keepdims=True) * (1.0 / MLA_QK) + NORM_EPS)
        qh = q * gfull_ref[...]
        if rope:
            q_rope = qh[:, 128:]
            qh = jnp.concatenate([qh[:, :128], q_rope * c_ref[...] + _rope_partner(q_rope) * s_ref[...]], axis=1)
        qh = ((qh * r) * (MLA_SCALE * LOG2_E)).astype(BF16)
        s = lax.dot_general(qh, k_ref[:, sl], NT_DIMS, preferred_element_type=F32)
        m = jnp.max(s, axis=-1, keepdims=True)
        if has_ctx:
            sc = lax.dot_general(qh, kc_ref[:, sl], NT_DIMS, preferred_element_type=F32)
            m = jnp.maximum(m, jnp.max(sc, axis=-1, keepdims=True))
        e = jnp.exp2(s - m)
        den = jnp.sum(e, axis=-1, keepdims=True)
        o = jnp.dot(e.astype(BF16), v_ref[:, vs], preferred_element_type=F32)
        if has_ctx:
            ec = jnp.exp2(sc - m)
            den = den + jnp.sum(ec, axis=-1, keepdims=True)
            o = o + jnp.dot(ec.astype(BF16), vc_ref[:, vs], preferred_element_type=F32)
        o_ref[:, vs] = (o * (1.0 / den)).astype(BF16)


def _mla_attn(proj, wq, g_q, gq_full, rope_tabs, k, v, ctx_kv, batch, t, layer):
    rope = rope_tabs is not None
    has_ctx = ctx_kv is not None
    nq = t // MLA_TQ
    const = lambda shape: pl.BlockSpec(shape, lambda b, i: (0,) * len(shape))
    ins = [proj, wq, g_q, gq_full]
    specs = [pl.BlockSpec((MLA_TQ, MLA_Q_LORA), lambda b, i: (b * nq + i, COL_CQ // MLA_Q_LORA)),
             const((MLA_Q_LORA, 2048)), const((DEPTH, MLA_Q_LORA)), const((1, 256))]
    if rope:
        ins += list(rope_tabs)
        specs += [pl.BlockSpec((MLA_TQ, 128), lambda b, i: (i, 0))] * 2
    ins += [k, v]
    specs += [pl.BlockSpec((None, t, 2048), lambda b, i: (b, 0, 0)), pl.BlockSpec((None, t, MLA_WIDTH), lambda b, i: (b, 0, 0))]
    est = 2 * (t * 2048 * 2 + t * MLA_WIDTH * 2) + 6 * MLA_TQ * t * 4
    if has_ctx:
        kc, vc = ctx_kv
        lc = kc.shape[1]
        ins += [kc, vc]
        specs += [pl.BlockSpec((None, lc, 2048), lambda b, i: (b, 0, 0)), pl.BlockSpec((None, lc, MLA_WIDTH), lambda b, i: (b, 0, 0))]
        est += 2 * lc * 3072 * 2
    return pl.pallas_call(
        functools.partial(_mla_attn_kernel, layer=layer, rope=rope, has_ctx=has_ctx),
        grid=(batch, nq), in_specs=specs,
        out_specs=pl.BlockSpec((MLA_TQ, MLA_WIDTH), lambda b, i: (b * nq + i, 0)),
        out_shape=jax.ShapeDtypeStruct((batch * t, MLA_WIDTH), BF16),
        compiler_params=pltpu.CompilerParams(
            dimension_semantics=("arbitrary", "arbitrary"), vmem_limit_bytes=_vmem_limit(est)),
        name="mla_attn",
    )(*ins)


def _mla_branch(proj, batch, t, layer, mla_w, w_ukv, g_q, g_kv, rope, ctx):
    wq, gq_full, gk_n, gk_r = mla_w
    nt = t // MLA_TR
    ckv_spec = pl.BlockSpec((MLA_TR, MLA_KV_LORA), lambda b, i: (b * nt + i, COL_CKV // MLA_KV_LORA))
    kpe_spec = pl.BlockSpec((MLA_TR, 128), lambda b, i: (b * nt + i, COL_KPE // 128))
    if rope is None:
        k, v, ckv_n = _mla_kv(proj, ckv_spec, proj, kpe_spec, None, w_ukv, g_kv, gk_n, gk_r,
                              batch, t, layer, normalize=True, emit_ckv=True)
        return _mla_attn(proj, wq, g_q, gq_full, None, k, v, None, batch, t, layer), ckv_n
    c64, s64 = rope
    z64 = jnp.zeros_like(c64)
    tabs = (jnp.concatenate([c64, z64], axis=1), jnp.concatenate([s64, z64], axis=1))
    k, v = _mla_kv(proj, ckv_spec, proj, kpe_spec, tabs, w_ukv, g_kv, gk_n, gk_r,
                   batch, t, layer, normalize=True, emit_ckv=False)
    ctx_ckv, ctx_kpe = ctx
    lc = ctx_ckv.shape[2]
    kc, vc = _mla_kv(ctx_ckv, pl.BlockSpec((None, None, MLA_TR, MLA_KV_LORA), lambda b, i: (b, layer, i, 0)),
                     ctx_kpe, pl.BlockSpec((None, None, MLA_TR, 128), lambda b, i: (b, layer, i, 0)),
                     None, w_ukv, g_kv, gk_n, gk_r, batch, lc, layer, normalize=False, emit_ckv=False)
    return _mla_attn(proj, wq, g_q, gq_full, tabs, k, v, (kc, vc), batch, t, layer), None


SWA_TR = 128
SWA_NEG = -1e30


def _pair_sum_matrix():
    g = (np.arange(128)[:, None] // 64 == np.arange(128)[None, :] // 64).astype(np.float32)
    return jnp.asarray(np.concatenate([g, g], axis=0), BF16)


def _group_rms_scale(x, pair_ref):
    sq = x * x
    hi = sq.astype(BF16)
    lo = (sq - hi.astype(F32)).astype(BF16)
    ss = jnp.dot(jnp.concatenate([hi, lo], axis=1), pair_ref[...], preferred_element_type=F32)
    return lax.rsqrt(ss * (1.0 / SWA_HEAD_DIM) + NORM_EPS)


def _swa_kv_kernel(*refs, normalize, rope, emit, pad_blocks, layer):
    it = iter(refs)
    k_ref, v_ref = next(it), next(it)
    if rope:
        c_ref, s_ref = next(it), next(it)
    if normalize:
        g_ref, pair_ref = next(it), next(it)
    ko_ref, vo_ref = next(it), next(it)
    if emit:
        kn_ref, vn_ref = next(it), next(it)
    lane = lax.broadcasted_iota(jnp.int32, (SWA_TR, 128), 1)
    low = lane < 64

    def halves(x, c, o_ref):
        sw = pltpu.roll(x, 64, axis=1)
        zero = jnp.zeros_like(x)
        o_ref[4 * c + 0] = jnp.where(low, x, zero).astype(BF16)
        o_ref[4 * c + 1] = jnp.where(low, zero, sw).astype(BF16)
        o_ref[4 * c + 2] = jnp.where(low, sw, zero).astype(BF16)
        o_ref[4 * c + 3] = jnp.where(low, zero, x).astype(BF16)

    def compute():
        for c in range(2):
            sl = slice(128 * c, 128 * c + 128)
            kb = k_ref[:, sl]
            if normalize:
                r = _group_rms_scale(kb, pair_ref)
                kn = (kb * r) * g_ref[...]
                if emit:
                    kn_ref[:, sl] = kn
                if rope:
                    kn = kn * c_ref[...] + _rope_partner(kn) * s_ref[...]
            else:
                kn = kb
            halves(kn, c, ko_ref)
            vb = v_ref[:, sl]
            if emit:
                vn_ref[:, sl] = vb
            halves(vb, c, vo_ref)

    if pad_blocks:
        t = pl.program_id(1)
        is_pad = jnp.logical_or(t == 0, t == pl.num_programs(1) - 1)

        @pl.when(is_pad)
        def _():
            ko_ref[...] = jnp.zeros_like(ko_ref)
            vo_ref[...] = jnp.zeros_like(vo_ref)

        pl.when(jnp.logical_not(is_pad))(compute)
    else:
        compute()


def _swa_kv(k_src, k_spec, v_src, v_spec, rope_tabs, norm_ins, batch, rows, layer, emit, pad_blocks):
    rope = rope_tabs is not None
    normalize = norm_ins is not None
    nt = rows // SWA_TR + (2 if pad_blocks else 0)
    ins, specs = [k_src, v_src], [k_spec, v_spec]
    if rope:
        tab_row = (lambda t: jnp.clip(t - 1, 0, nt - 3)) if pad_blocks else (lambda t: t)
        ins += list(rope_tabs)
        specs += [pl.BlockSpec((SWA_TR, 128), lambda b, t: (tab_row(t), 0))] * 2
    if normalize:
        ins += list(norm_ins)
        specs += [pl.BlockSpec((1, 128), lambda b, t: (0, 0)), pl.BlockSpec((256, 128), lambda b, t: (0, 0))]
    out_shape = [jax.ShapeDtypeStruct((batch, 8, nt * SWA_TR, 128), BF16)] * 2
    out_specs = [pl.BlockSpec((None, 8, SWA_TR, 128), lambda b, t: (b, 0, t, 0))] * 2
    if emit:
        out_shape += [jax.ShapeDtypeStruct((batch, rows, 256), F32)] * 2
        out_specs += [pl.BlockSpec((None, SWA_TR, 256), lambda b, t: (b, t, 0))] * 2
    return pl.pallas_call(
        functools.partial(_swa_kv_kernel, normalize=normalize, rope=rope, emit=emit, pad_blocks=pad_blocks, layer=layer),
        grid=(batch, nt), in_specs=specs, out_specs=out_specs, out_shape=out_shape,
        compiler_params=pltpu.CompilerParams(dimension_semantics=("arbitrary", "arbitrary")),
        name="swa_kv",
    )(*ins)


def _swa_attn_kernel(*refs, rope, windowed, has_ctx, t_total, layer):
    it = iter(refs)
    sink_ref, q_ref = next(it), next(it)
    if rope:
        c_ref, s_ref = next(it), next(it)
    g_ref, pair_ref = next(it), next(it)
    k_ref, v_ref = next(it), next(it)
    if has_ctx:
        kc_ref, vc_ref = next(it), next(it)
    o_ref = next(it)
    n = pl.program_id(1)
    if windowed:
        start = pl.multiple_of(n * SWA_TR, SWA_TR)
        win = pl.ds(start, 3 * SWA_TR)
        r_i = lax.broadcasted_iota(jnp.int32, (2 * SWA_TR, 3 * SWA_TR), 0) & (SWA_TR - 1)
        c_i = lax.broadcasted_iota(jnp.int32, (2 * SWA_TR, 3 * SWA_TR), 1)
        kpos = (n - 1) * SWA_TR + c_i
        diff = SWA_TR + r_i - c_i
        valid2 = (kpos >= 0) & (kpos < t_total) & (diff <= SWA_WINDOW) & (diff >= -SWA_WINDOW)
    else:
        win = slice(None)
    def normed_queries(cb):
        sl = slice(128 * cb, 128 * cb + 128)
        qb = q_ref[:, sl]
        r = _group_rms_scale(qb, pair_ref)
        qn = (qb * r) * g_ref[...]
        if rope:
            qn = qn * c_ref[...] + _rope_partner(qn) * s_ref[...]
        return (qn * (SWA_SCALE * LOG2_E)).astype(BF16)

    tq = q_ref.shape[0]
    low_lanes = lax.broadcasted_iota(jnp.int32, (2 * tq, 128), 1) < SWA_HEAD_DIM

    lk = 3 * SWA_TR if windowed else k_ref.shape[1]
    lc = kc_ref.shape[1] if has_ctx else 0
    row_blk = lax.broadcasted_iota(jnp.int32, (2 * tq, 1), 0) // tq
    for j in range(SWA_KV_HEADS):
        q2 = jnp.concatenate([normed_queries(2 * j), normed_queries(2 * j + 1)], axis=0)
        k_parts, v_parts = [], []
        for e in range(2):
            k_parts.append(k_ref[2 * j + e, win, :])
            v_parts.append(v_ref[2 * j + e, win, :])
            if has_ctx:
                k_parts.append(kc_ref[2 * j + e])
                v_parts.append(vc_ref[2 * j + e])
        s = lax.dot_general(q2, jnp.concatenate(k_parts, axis=0), NT_DIMS, preferred_element_type=F32)
        p_parts, inv = [], []
        for e in range(2):
            off = e * (lk + lc)
            sink = jnp.where(row_blk == 0, sink_ref[layer, 4 * j + e], sink_ref[layer, 4 * j + 2 + e]) * LOG2_E
            s_loc = s[:, off:off + lk]
            if windowed:
                s_loc = jnp.where(valid2, s_loc, SWA_NEG)
            m = jnp.maximum(jnp.max(s_loc, axis=-1, keepdims=True), sink)
            if has_ctx:
                s_ctx = s[:, off + lk:off + lk + lc]
                m = jnp.maximum(m, jnp.max(s_ctx, axis=-1, keepdims=True))
            p = jnp.exp2(s_loc - m)
            den = jnp.sum(p, axis=-1, keepdims=True) + jnp.exp2(sink - m)
            p_parts.append(p.astype(BF16))
            if has_ctx:
                pc = jnp.exp2(s_ctx - m)
                den = den + jnp.sum(pc, axis=-1, keepdims=True)
                p_parts.append(pc.astype(BF16))
            inv.append(1.0 / den)
        o = jnp.dot(jnp.concatenate(p_parts, axis=1), jnp.concatenate(v_parts, axis=0), preferred_element_type=F32)
        o = o * jnp.where(low_lanes, inv[0], inv[1])
        o_ref[:, 256 * j:256 * j + 128] = o[:tq].astype(BF16)
        o_ref[:, 256 * j + 128:256 * j + 256] = o[tq:].astype(BF16)


def _swa_attn(sink, proj, rope_tabs, norm_ins, k, v, ctx_kv, batch, t, layer):
    rope = rope_tabs is not None
    has_ctx = ctx_kv is not None
    tq = SWA_TR if has_ctx else 2 * SWA_TR
    nq = t // tq
    ins = [sink, proj]
    specs = [pl.BlockSpec(memory_space=pltpu.SMEM),
             pl.BlockSpec((tq, SWA_WIDTH), lambda b, i: (b * nq + i, COL_SQ // SWA_WIDTH))]
    if rope:
        ins += list(rope_tabs)
        specs += [pl.BlockSpec((tq, 128), lambda b, i: (i, 0))] * 2
    ins += list(norm_ins)
    specs += [pl.BlockSpec((1, 128), lambda b, i: (0, 0)), pl.BlockSpec((256, 128), lambda b, i: (0, 0))]
    lk = k.shape[2]
    ins += [k, v]
    specs += [pl.BlockSpec((None, 8, lk, 128), lambda b, i: (b, 0, 0, 0))] * 2
    if has_ctx:
        lc = ctx_kv[0].shape[2]
        ins += list(ctx_kv)
        specs += [pl.BlockSpec((None, 8, lc, 128), lambda b, i: (b, 0, 0, 0))] * 2
    return pl.pallas_call(
        functools.partial(_swa_attn_kernel, rope=rope, windowed=has_ctx, has_ctx=has_ctx, t_total=t, layer=layer),
        grid=(batch, nq), in_specs=specs,
        out_specs=pl.BlockSpec((tq, SWA_WIDTH), lambda b, i: (b * nq + i, 0)),
        out_shape=jax.ShapeDtypeStruct((batch * t, SWA_WIDTH), BF16),
        compiler_params=pltpu.CompilerParams(dimension_semantics=("arbitrary", "arbitrary")),
        name="swa_attn",
    )(*ins)


def _swa_branch(proj, batch, t, layer, g_qn_l, g_kn_l, sink, rope, ctx):
    pair = _pair_sum_matrix()
    tile2 = lambda g: jnp.concatenate([g, g])[None]
    nt = t // SWA_TR
    if rope is None:
        k_spec = pl.BlockSpec((SWA_TR, 256), lambda b, i: (b * nt + i, COL_SK // 256))
        v_spec = pl.BlockSpec((SWA_TR, 256), lambda b, i: (b * nt + i, COL_SV // 256))
        k, v, k_n, v_raw = _swa_kv(proj, k_spec, proj, v_spec, None, (tile2(g_kn_l), pair),
                                   batch, t, layer, emit=True, pad_blocks=False)
        o = _swa_attn(sink, proj, None, (tile2(g_qn_l), pair), k, v, None, batch, t, layer)
        return o, k_n, v_raw
    c64, s64 = rope
    tabs = (jnp.concatenate([c64, c64], axis=1), jnp.concatenate([s64, s64], axis=1))
    row = lambda b, i: b * nt + jnp.clip(i - 1, 0, nt - 1)
    k_spec = pl.BlockSpec((SWA_TR, 256), lambda b, i: (row(b, i), COL_SK // 256))
    v_spec = pl.BlockSpec((SWA_TR, 256), lambda b, i: (row(b, i), COL_SV // 256))
    k, v = _swa_kv(proj, k_spec, proj, v_spec, tabs, (tile2(g_kn_l), pair), batch, t, layer, emit=False, pad_blocks=True)
    ctx_k, ctx_v = ctx
    lc = ctx_k.shape[2]
    c_spec = pl.BlockSpec((None, None, SWA_TR, 256), lambda b, i: (b, layer, i, 0))
    kc, vc = _swa_kv(ctx_k, c_spec, ctx_v, c_spec, None, None, batch, lc, layer, emit=False, pad_blocks=False)
    o = _swa_attn(sink, proj, tabs, (tile2(g_qn_l), pair), k, v, (kc, vc), batch, t, layer)
    return o, None, None


GLA_LEVELS = (64, 32, 16, 8, 4, 2, 1)
GLA_MXU_LEVELS = (2, 1)
GLA_ROWSETS = 1 + len(GLA_MXU_LEVELS)


def _gla_sum_matrix(backward):
    c = GLA_CHUNK
    t = np.arange(c)[:, None]
    j = np.arange(c)[None, :]
    sets = [j <= t]
    for g in GLA_MXU_LEVELS:
        e = (t // (2 * g)) * 2 * g + g - 1
        upper = (t // g) % 2 == 1
        sets.append(np.where(upper, (j > e) & (j <= t), (j > t) & (j <= e)))
    n = np.concatenate(sets, axis=0).astype(np.float32)
    if backward:
        n = n.reshape(GLA_ROWSETS, c, c)[:, ::-1, ::-1].reshape(GLA_ROWSETS * c, c)
    return jnp.asarray(np.concatenate([n, n, n], axis=1), BF16)


def _gla_kernel(*refs, has_s0):
    it = iter(refs)
    srcs = [tuple(next(it) for _ in range(4)) for _ in range(2)]
    nmat = (next(it), next(it))
    wg = (next(it), next(it))
    bg = (next(it), next(it))
    s0_ref = next(it) if has_s0 else None
    o_refs = (next(it), next(it))
    sfin_ref = next(it)
    s_ref = next(it)
    i = pl.program_id(1)
    c = GLA_CHUNK

    @pl.when(i == 0)
    def _():
        s_ref[...] = s0_ref[...] if has_s0 else jnp.zeros_like(s_ref)

    row = lax.broadcasted_iota(jnp.int32, (c, c), 0)
    col = lax.broadcasted_iota(jnp.int32, (c, c), 1)
    diag = row == col
    log2 = lambda g: int(g).bit_length() - 1
    second_half = [((row >> log2(g)) & 1) == 1 for g in GLA_LEVELS]
    same_block = [(row >> (log2(g) + 1)) == (col >> (log2(g) + 1)) for g in GLA_LEVELS]
    for d in range(2):
        q_ref, k_ref, v_ref, gg_ref = srcs[d]
        z = jnp.dot(gg_ref[...].astype(BF16), wg[d][...], preferred_element_type=F32) + bg[d][...]
        la = (jnp.minimum(z, 0.0) - jnp.log1p(jnp.exp(-jnp.abs(z)))) * (1.0 / GLA_GATE_NORM)
        hi = la.astype(BF16)
        r1 = la - hi.astype(F32)
        mid = r1.astype(BF16)
        lo = (r1 - mid.astype(F32)).astype(BF16)
        ex_all = jnp.dot(nmat[d][...], jnp.concatenate([hi, mid, lo], axis=0), preferred_element_type=F32)
        for h in range(GLA_HEADS):
            sl = slice(GLA_DK * h, GLA_DK * h + GLA_DK)
            vs = slice(GLA_DV * h, GLA_DV * h + GLA_DV)
            ex = ex_all[:, sl]
            q = q_ref[:, sl] * (GLA_DK ** -0.5)
            k = k_ref[:, sl]
            v = v_ref[:, vs].astype(BF16)
            b = ex[0:c]
            eb = jnp.exp(b)
            b_exit = b[c - 1:c] if d == 0 else b[0:1]
            ek = jnp.exp(b_exit - b)
            s_old = s_ref[d, h]
            o = jnp.dot((q * eb).astype(BF16), s_old.astype(BF16), preferred_element_type=F32)
            a = jnp.where(diag, jnp.sum(q * k, axis=-1, keepdims=True), 0.0)
            for li, g in enumerate(GLA_LEVELS):
                if g in GLA_MXU_LEVELS:
                    r = 1 + GLA_MXU_LEVELS.index(g)
                    eg = jnp.exp(ex[r * c:(r + 1) * c])
                else:
                    b3 = b.reshape(c // (2 * g), 2 * g, GLA_DK)
                    edge = g - 1 if d == 0 else g
                    b_edge = jnp.broadcast_to(b3[:, edge:edge + 1, :], b3.shape).reshape(c, GLA_DK)
                    eg = jnp.exp(-jnp.abs(b - b_edge))
                qe, ke = q * eg, k * eg
                late = second_half[li]
                qg = (jnp.where(late, qe, 0.0) if d == 0 else jnp.where(late, 0.0, qe)).astype(BF16)
                kg = (jnp.where(late, 0.0, ke) if d == 0 else jnp.where(late, ke, 0.0)).astype(BF16)
                ag = lax.dot_general(qg, kg, NT_DIMS, preferred_element_type=F32)
                if 2 * g < c:
                    ag = jnp.where(same_block[li], ag, 0.0)
                a = a + ag
            o = o + jnp.dot(a.astype(BF16), v, preferred_element_type=F32)
            o_refs[d][:, vs] = o
            a_col = eb.T[:, c - 1:c] if d == 0 else eb.T[:, 0:1]
            kt = (k * ek).T.astype(BF16)
            s_ref[d, h] = a_col * s_old + jnp.dot(kt, v, preferred_element_type=F32)

    @pl.when(i == pl.num_programs(1) - 1)
    def _():
        sfin_ref[...] = s_ref[...]


def _gla(proj, batch, t, layer, w_gf, b_gf, w_gb, b_gb, s0):
    c = GLA_CHUNK
    n = t // c
    fwd = lambda b, i: b * n + i
    bwd = lambda b, i: b * n + (n - 1 - i)
    ins, specs = [], []
    for rowf in (fwd, bwd):
        ins += [proj] * 4
        specs += [pl.BlockSpec((c, 512), lambda b, i, rowf=rowf: (rowf(b, i), COL_GQ // 512)),
                  pl.BlockSpec((c, 512), lambda b, i, rowf=rowf: (rowf(b, i), COL_GK // 512)),
                  pl.BlockSpec((c, GLA_WIDTH), lambda b, i, rowf=rowf: (rowf(b, i), COL_GV // GLA_WIDTH)),
                  pl.BlockSpec((c, 128), lambda b, i, rowf=rowf: (rowf(b, i), COL_GG // 128))]
    const = lambda shape: pl.BlockSpec(shape, lambda b, i: (0,) * len(shape))
    pad_w = lambda w, off: jnp.zeros((128, 512), F32).at[off:off + GLA_GATE_RANK].set(w).astype(BF16)
    ins += [_gla_sum_matrix(False), _gla_sum_matrix(True), pad_w(w_gf, 0), pad_w(w_gb, GLA_GATE_RANK), b_gf[None], b_gb[None]]
    specs += [const((GLA_ROWSETS * c, 3 * c))] * 2 + [const((128, 512))] * 2 + [const((1, 512))] * 2
    state_spec = pl.BlockSpec((None, 2, GLA_HEADS, GLA_DK, GLA_DV), lambda b, i: (b, 0, 0, 0, 0))
    if s0 is not None:
        ins.append(s0)
        specs.append(pl.BlockSpec((None, None, 2, GLA_HEADS, GLA_DK, GLA_DV), lambda b, i: (b, layer, 0, 0, 0, 0)))
    return pl.pallas_call(
        functools.partial(_gla_kernel, has_s0=s0 is not None),
        grid=(batch, n), in_specs=specs,
        out_specs=[pl.BlockSpec((c, GLA_WIDTH), lambda b, i: (fwd(b, i), 0)),
                   pl.BlockSpec((c, GLA_WIDTH), lambda b, i: (bwd(b, i), 0)), state_spec],
        out_shape=[jax.ShapeDtypeStruct((batch * t, GLA_WIDTH), F32)] * 2
        + [jax.ShapeDtypeStruct((batch, 2, GLA_HEADS, GLA_DK, GLA_DV), F32)],
        scratch_shapes=[pltpu.VMEM((2, GLA_HEADS, GLA_DK, GLA_DV), F32)],
        compiler_params=pltpu.CompilerParams(dimension_semantics=("arbitrary", "arbitrary")),
        name="gla",
    )(*ins)


GOUT_TM = 512


def _gla_out_kernel(of_ref, ob_ref, gate_ref, g_ref, o_ref, *, layer):
    g = g_ref[pl.ds(layer, 1), :]
    for h in range(GLA_HEADS):
        vs = slice(GLA_DV * h, GLA_DV * h + GLA_DV)
        o = of_ref[:, vs] + ob_ref[:, vs]
        y = (o * lax.rsqrt(jnp.mean(o * o, axis=-1, keepdims=True) + NORM_EPS)) * g
        gate = gate_ref[:, vs]
        o_ref[:, vs] = (y * (gate * jax.nn.sigmoid(gate))).astype(BF16)


def _gla_out(o_f, o_b, proj, g_gla_out, layer):
    m = o_f.shape[0]
    return pl.pallas_call(
        functools.partial(_gla_out_kernel, layer=layer),
        grid=(m // GOUT_TM,),
        in_specs=[pl.BlockSpec((GOUT_TM, GLA_WIDTH), lambda i: (i, 0)),
                  pl.BlockSpec((GOUT_TM, GLA_WIDTH), lambda i: (i, 0)),
                  pl.BlockSpec((GOUT_TM, GLA_WIDTH), lambda i: (i, COL_GOUT // GLA_WIDTH)),
                  pl.BlockSpec((DEPTH, GLA_DV), lambda i: (0, 0))],
        out_specs=pl.BlockSpec((GOUT_TM, GLA_WIDTH), lambda i: (i, 0)),
        out_shape=jax.ShapeDtypeStruct((m, GLA_WIDTH), BF16),
        compiler_params=pltpu.CompilerParams(dimension_semantics=("arbitrary",)),
        name="gla_out",
    )(o_f, o_b, proj, g_gla_out)


MERGE_TM = 1024
MERGE_TN = 512


def _merge_kernel(om_ref, og_ref, os_ref, gm_ref, gg_ref, gs_ref, wm_ref, wg_ref, ws_ref, o_ref):
    def branch(o_r, gate_r, w_r):
        return jax.nn.sigmoid(gate_r[...]) * jnp.dot(o_r[...], w_r[...], preferred_element_type=F32)

    merged = branch(om_ref, gm_ref, wm_ref) + branch(og_ref, gg_ref, wg_ref) + branch(os_ref, gs_ref, ws_ref)
    o_ref[...] = merged.astype(BF16)


def _merge(o_mla, o_gla, o_swa, proj, w_br_mla, w_br_gla, w_br_swa, layer):
    m = o_mla.shape[0]
    nn = D_MODEL // MERGE_TN
    o_spec = pl.BlockSpec((MERGE_TM, 1024), lambda i, j: (i, 0))
    gate_spec = lambda col: pl.BlockSpec((MERGE_TM, MERGE_TN), lambda i, j: (i, col // MERGE_TN + j))
    w_spec = pl.BlockSpec((None, 1024, MERGE_TN), lambda i, j: (layer, 0, j))
    est = 2 * (3 * MERGE_TM * 1024 * 2 + 3 * MERGE_TM * MERGE_TN * 4 + 3 * 1024 * MERGE_TN * 2 + MERGE_TM * MERGE_TN * 2)
    return pl.pallas_call(
        _merge_kernel,
        grid=(m // MERGE_TM, nn),
        in_specs=[o_spec, o_spec, o_spec, gate_spec(COL_GATE_MLA), gate_spec(COL_GATE_GLA), gate_spec(COL_GATE_SWA),
                  w_spec, w_spec, w_spec],
        out_specs=pl.BlockSpec((MERGE_TM, MERGE_TN), lambda i, j: (i, j)),
        out_shape=jax.ShapeDtypeStruct((m, D_MODEL), BF16),
        compiler_params=pltpu.CompilerParams(
            dimension_semantics=("arbitrary", "arbitrary"), vmem_limit_bytes=_vmem_limit(est)),
        name="merge",
    )(o_mla, o_gla, o_swa, proj, proj, proj, w_br_mla, w_br_gla, w_br_swa)


OUT_TM = 1024
OUT_TN = 512


def _out_proj_kernel(m_ref, w_ref, x_ref, gate_ref, o_ref, *, base_row, rows_per_cond):
    row = _cond_row(pl.program_id(0), m_ref.shape[0], base_row, rows_per_cond)
    y = jnp.dot(m_ref[...], w_ref[...], preferred_element_type=F32)
    o_ref[...] = x_ref[...] + gate_ref[pl.ds(row, 1), :] * y


def _out_proj(merged, x, mods, w_out, layer, base_row, rows_per_cond):
    m = x.shape[0]
    est = 2 * (OUT_TM * D_MODEL * 2 + D_MODEL * OUT_TN * 2 + 2 * OUT_TM * OUT_TN * 4)
    return pl.pallas_call(
        functools.partial(_out_proj_kernel, base_row=base_row, rows_per_cond=rows_per_cond),
        grid=(m // OUT_TM, D_MODEL // OUT_TN),
        in_specs=[pl.BlockSpec((OUT_TM, D_MODEL), lambda i, j: (i, 0)),
                  pl.BlockSpec((None, D_MODEL, OUT_TN), lambda i, j: (layer, 0, j)),
                  pl.BlockSpec((OUT_TM, OUT_TN), lambda i, j: (i, j)),
                  pl.BlockSpec((None, None, 8, OUT_TN), lambda i, j: (layer, 5, 0, j))],
        out_specs=pl.BlockSpec((OUT_TM, OUT_TN), lambda i, j: (i, j)),
        out_shape=jax.ShapeDtypeStruct((m, D_MODEL), F32),
        compiler_params=pltpu.CompilerParams(
            dimension_semantics=("arbitrary", "arbitrary"), vmem_limit_bytes=_vmem_limit(est)),
        name="out_proj",
    )(merged, w_out, x, mods)


def _trunk_layer(x, mods, layer, w, group):
    batch, t, base_row, rows_per_cond, rope, ctx = group
    x = _ffn(x, mods, layer, 0, w['g_norm1'], w['w_ff1_gu'], w['w_ff1_down'], base_row, rows_per_cond)
    proj = _in_proj(x, mods, layer, w['g_norm2'], w['w_in_packed'], base_row, rows_per_cond)
    mla_ctx = swa_ctx = s0 = None
    if ctx is not None:
        mla_ctx, swa_ctx, s0 = ctx[:2], ctx[2:4], ctx[4]
    o_mla, ckv_n = _mla_branch(proj, batch, t, layer, w['mla_packed'][layer], w['w_mla_ukv'], w['g_mla_q'], w['g_mla_kv'],
                               rope, mla_ctx)
    o_swa, k_n, v_raw = _swa_branch(proj, batch, t, layer, w['g_swa_qn'][layer], w['g_swa_kn'][layer], w['swa_sink'],
                                    rope, swa_ctx)
    o_f, o_b, s_fin = _gla(proj, batch, t, layer, w['w_gla_gf'][layer], w['b_gla_gf'][layer],
                           w['w_gla_gb'][layer], w['b_gla_gb'][layer], s0)
    o_gla = _gla_out(o_f, o_b, proj, w['g_gla_out'], layer)
    merged = _merge(o_mla, o_gla, o_swa, proj, w['w_br_mla'], w['w_br_gla'], w['w_br_swa'], layer)
    x = _out_proj(merged, x, mods, w['w_out'], layer, base_row, rows_per_cond)
    x = _ffn(x, mods, layer, 6, w['g_norm3'], w['w_ff2_gu'], w['w_ff2_down'], base_row, rows_per_cond)
    new_ctx = None
    if ctx is None:
        kpe = proj[:, COL_KPE:COL_KPE + MLA_ROPE].reshape(batch, t, MLA_ROPE)
        new_ctx = (ckv_n, kpe, k_n.reshape(batch, t, SWA_KV_HEADS, SWA_HEAD_DIM),
                   v_raw.reshape(batch, t, SWA_KV_HEADS, SWA_HEAD_DIM), s_fin)
    return x, new_ctx


def kernel(x_prompt, x_sample, cache_mla_ckv, cache_mla_kpe, cache_swa_k, cache_swa_v, state_gla,
           c, c_ctx, w_ada, b_ada, g_norm1, g_norm2, g_norm3,
           w_ff1_gu, w_ff1_down, w_ff2_gu, w_ff2_down, w_in,
           g_mla_q, w_mla_uq, g_mla_kv, w_mla_ukv, g_mla_qn, g_mla_kn,
           w_gla_gf, b_gla_gf, w_gla_gb, b_gla_gb, g_gla_out,
           g_swa_qn, g_swa_kn, swa_sink, w_br_mla, w_br_gla, w_br_swa, w_out):
    bp, tp, _ = x_prompt.shape
    bs, ts, _ = x_sample.shape
    assert bs + 1 <= 8, "conditioning rows are packed into one 8-row tile"
    cond8 = jnp.zeros((8, D_MODEL), F32).at[0].set(c_ctx).at[1:1 + bs].set(c)
    mods = _adaln(cond8, w_ada, b_ada)
    w = dict(g_norm1=g_norm1, g_norm2=g_norm2, g_norm3=g_norm3,
             w_ff1_gu=w_ff1_gu, w_ff1_down=w_ff1_down, w_ff2_gu=w_ff2_gu, w_ff2_down=w_ff2_down,
             w_in_packed=_pack_w_in(w_in),
             mla_packed=[_pack_mla(w_mla_uq[l], g_mla_qn[l], g_mla_kn[l]) for l in range(DEPTH)],
             w_mla_ukv=w_mla_ukv, g_mla_q=g_mla_q, g_mla_kv=g_mla_kv,
             g_swa_qn=g_swa_qn, g_swa_kn=g_swa_kn, swa_sink=swa_sink,
             w_gla_gf=w_gla_gf, b_gla_gf=b_gla_gf, w_gla_gb=w_gla_gb, b_gla_gb=b_gla_gb, g_gla_out=g_gla_out,
             w_br_mla=w_br_mla.astype(BF16), w_br_gla=w_br_gla.astype(BF16), w_br_swa=w_br_swa.astype(BF16),
             w_out=w_out.astype(BF16))
    past = cache_mla_ckv.shape[2]
    ctx = (cache_mla_ckv, jnp.pad(cache_mla_kpe, ((0, 0), (0, 0), (0, 0), (0, 128 - MLA_ROPE))),
           cache_swa_k.reshape(bs, DEPTH, past, SWA_KV_HEADS * SWA_HEAD_DIM),
           cache_swa_v.reshape(bs, DEPTH, past, SWA_KV_HEADS * SWA_HEAD_DIM), state_gla)
    group_p = (bp, tp, 0, bp * tp, None, None)
    group_s = (bs, ts, 1, ts, _rope_tables(ts), ctx)
    y_p = x_prompt.reshape(bp * tp, D_MODEL)
    y_s = x_sample.reshape(bs * ts, D_MODEL)
    new = []
    for l in range(DEPTH):
        y_p, new_ctx = _trunk_layer(y_p, mods, l, w, group_p)
        new.append(new_ctx)
        y_s, _ = _trunk_layer(y_s, mods, l, w, group_s)
    stacked = tuple(jnp.stack([new[l][k] for l in range(DEPTH)], axis=1) for k in range(5))
    return (y_p.reshape(bp, tp, D_MODEL), y_s.reshape(bs, ts, D_MODEL)) + stacked
```

```python
import functools

import numpy as np
import jax
import jax.numpy as jnp
from jax import lax
from jax.experimental import pallas as pl
from jax.experimental.pallas import tpu as pltpu

F32 = jnp.float32
BF16 = jnp.bfloat16

D_MODEL = 2048
DEPTH = 2
GRID_W = 64
ROPE_BASE = 10000.0
NORM_EPS = 1e-6
MLA_HEADS = 8
MLA_Q_LORA = 512
MLA_KV_LORA = 512
MLA_NOPE = 128
MLA_ROPE = 64
MLA_V = 128
MLA_QK = MLA_NOPE + MLA_ROPE
MLA_SCALE = MLA_QK ** -0.5
MLA_WIDTH = MLA_HEADS * MLA_V
MLA_HEAD_PAD = 256
GLA_HEADS = 4
GLA_DK = 128
GLA_DV = 256
GLA_GATE_RANK = 16
GLA_GATE_NORM = 16.0
GLA_WIDTH = GLA_HEADS * GLA_DV
GLA_CHUNK = 128
SWA_HEADS = 16
SWA_KV_HEADS = 4
SWA_HEAD_DIM = 64
SWA_WINDOW = 128
SWA_SCALE = SWA_HEAD_DIM ** -0.5
SWA_WIDTH = SWA_HEADS * SWA_HEAD_DIM
D_FF = 5632
N_MOD = 9
IN_SPLITS = (MLA_Q_LORA, MLA_KV_LORA, MLA_ROPE,
             GLA_HEADS * GLA_DK, GLA_HEADS * GLA_DK, GLA_WIDTH, GLA_WIDTH, GLA_GATE_RANK, GLA_GATE_RANK,
             SWA_WIDTH, SWA_KV_HEADS * SWA_HEAD_DIM, SWA_KV_HEADS * SWA_HEAD_DIM,
             D_MODEL, D_MODEL, D_MODEL)

V7X_LANES = 128
V7X_SUBLANES = 8
V7X_VMEM_BYTES = 64 * 1024 * 1024
MIB = 1024 * 1024

NT_DIMS = (((1,), (1,)), ((), ()))
LOG2_E = 1.4426950408889634

COL_GATE_MLA = 0
COL_GATE_GLA = 2048
COL_GATE_SWA = 4096
GATE_COLS = 6144
COL_CQ = 0
COL_CKV = 512
COL_GQ = 1024
COL_GK = 1536
COL_GV = 2048
COL_GOUT = 3072
COL_SQ = 4096
COL_SK = 5120
COL_SV = 5376
COL_KPE = 5632
COL_GG = 5760
MAIN_COLS = 6144
PROJ_COLS = GATE_COLS + MAIN_COLS
PROJ_TN = 1024


def _vmem_limit(nbytes):
    return int(min(nbytes + 12 * MIB, V7X_VMEM_BYTES - 6 * MIB))


def _rope_partner(x):
    lane = lax.broadcasted_iota(jnp.int32, x.shape, 1)
    return jnp.where((lane & 31) < 16, pltpu.roll(x, 112, axis=1), pltpu.roll(x, 16, axis=1))


ADA_TN = 1024


def _adaln_kernel(cond_ref, w_ref, b_ref, o_ref):
    c = cond_ref[...]
    s = (c * jax.nn.sigmoid(c)).astype(BF16)
    o_ref[...] = jnp.dot(s, w_ref[...].astype(BF16), preferred_element_type=F32) + b_ref[...]


def _adaln(cond8, w_ada, b_ada):
    halves = D_MODEL // ADA_TN
    nj = N_MOD * halves
    return pl.pallas_call(
        _adaln_kernel,
        grid=(DEPTH, nj),
        in_specs=[
            pl.BlockSpec((8, D_MODEL), lambda l, j: (0, 0)),
            pl.BlockSpec((None, D_MODEL, ADA_TN), lambda l, j: (l, 0, j)),
            pl.BlockSpec((None, 1, ADA_TN), lambda l, j: (l, 0, j)),
        ],
        out_specs=pl.BlockSpec((None, None, 8, ADA_TN), lambda l, j: (l, j // halves, 0, j % halves)),
        out_shape=jax.ShapeDtypeStruct((DEPTH, N_MOD, 8, D_MODEL), F32),
        compiler_params=pltpu.CompilerParams(
            dimension_semantics=("arbitrary", "arbitrary"),
            vmem_limit_bytes=_vmem_limit(2 * D_MODEL * ADA_TN * 4)),
        name="adaln",
    )(cond8, w_ada, b_ada.reshape(DEPTH, 1, N_MOD * D_MODEL))


NORM_ROWS = 64


def _modulated_norm_to(h_ref, x_ref, g, shift, scale):
    gain = g * (1.0 + scale)

    def body(c, carry):
        r0 = pl.multiple_of(c * NORM_ROWS, NORM_ROWS)
        x = x_ref[pl.ds(r0, NORM_ROWS), :]
        ms = jnp.mean(x * x, axis=-1, keepdims=True)
        h_ref[pl.ds(r0, NORM_ROWS), :] = ((x * lax.rsqrt(ms + NORM_EPS)) * gain + shift).astype(BF16)
        return carry

    lax.fori_loop(0, x_ref.shape[0] // NORM_ROWS, body, 0, unroll=2)


def _cond_row(i, tm, base_row, rows_per_cond):
    return base_row + (i * tm) // rows_per_cond


FFN_TM = 1024
FFN_TF = 256
FFN_TN = 512


def _ffn_kernel(x_ref, shift_ref, scale_ref, gate_ref, g_ref, wg_ref, wu_ref, wd_ref, o_ref, h_ref,
                *, layer, base_row, rows_per_cond):
    i = pl.program_id(0)
    j = pl.program_id(1)
    row = _cond_row(i, x_ref.shape[0], base_row, rows_per_cond)

    @pl.when(j == 0)
    def _():
        _modulated_norm_to(h_ref, x_ref, g_ref[pl.ds(layer, 1), :],
                           shift_ref[pl.ds(row, 1), :], scale_ref[pl.ds(row, 1), :])
        o_ref[...] = jnp.zeros_like(o_ref)

    h = h_ref[...]
    a = jnp.dot(h, wg_ref[...].astype(BF16), preferred_element_type=F32)
    u = jnp.dot(h, wu_ref[...].astype(BF16), preferred_element_type=F32)
    act = (a * jax.nn.sigmoid(a) * u).astype(BF16)
    for n in range(0, D_MODEL, FFN_TN):
        o_ref[:, n:n + FFN_TN] += jnp.dot(act, wd_ref[:, n:n + FFN_TN].astype(BF16), preferred_element_type=F32)

    @pl.when(j == pl.num_programs(1) - 1)
    def _():
        o_ref[...] = x_ref[...] + (0.5 * gate_ref[pl.ds(row, 1), :]) * o_ref[...]


def _ffn(x, mods, layer, first_mod, g_norm, w_gu, w_down, base_row, rows_per_cond):
    m = x.shape[0]
    nf = D_FF // FFN_TF
    mod_spec = lambda k: pl.BlockSpec((None, None, 8, D_MODEL), lambda i, j: (layer, first_mod + k, 0, 0))
    est = (2 * FFN_TM * D_MODEL * 4 * 2 + FFN_TM * D_MODEL * 2
           + 2 * 3 * D_MODEL * FFN_TF * w_gu.dtype.itemsize)
    return pl.pallas_call(
        functools.partial(_ffn_kernel, layer=layer, base_row=base_row, rows_per_cond=rows_per_cond),
        grid=(m // FFN_TM, nf),
        in_specs=[
            pl.BlockSpec((FFN_TM, D_MODEL), lambda i, j: (i, 0)),
            mod_spec(0), mod_spec(1), mod_spec(2),
            pl.BlockSpec((DEPTH, D_MODEL), lambda i, j: (0, 0)),
            pl.BlockSpec((None, D_MODEL, FFN_TF), lambda i, j: (layer, 0, j)),
            pl.BlockSpec((None, D_MODEL, FFN_TF), lambda i, j: (layer, 0, j + nf)),
            pl.BlockSpec((None, FFN_TF, D_MODEL), lambda i, j: (layer, j, 0)),
        ],
        out_specs=pl.BlockSpec((FFN_TM, D_MODEL), lambda i, j: (i, 0)),
        out_shape=jax.ShapeDtypeStruct((m, D_MODEL), F32),
        scratch_shapes=[pltpu.VMEM((FFN_TM, D_MODEL), BF16)],
        compiler_params=pltpu.CompilerParams(
            dimension_semantics=("arbitrary", "arbitrary"), vmem_limit_bytes=_vmem_limit(est)),
        name="ffn",
    )(x, mods, mods, mods, g_norm, w_gu, w_gu, w_down)


PROJ_TM = 1024


IN_COLS = sum(IN_SPLITS)
_SRC = dict(zip(('cq', 'ckv', 'kpe', 'gq', 'gk', 'gv', 'gout', 'ggf', 'ggb', 'sq', 'sk', 'sv', 'gate_mla'),
                np.concatenate([[0], np.cumsum(IN_SPLITS)]).tolist()))
PACK_ROWS = 512
PACK_TILES = PROJ_COLS // PACK_ROWS


def _pack_plan():
    small, a, b = [], [], []
    for dst, src, width in ((COL_GATE_MLA, _SRC['gate_mla'], GATE_COLS), (GATE_COLS + COL_CQ, _SRC['cq'], 1024),
                            (GATE_COLS + COL_GQ, _SRC['gq'], 3072), (GATE_COLS + COL_SQ, _SRC['sq'], 1536)):
        assert dst == len(a) * PACK_ROWS and width % PACK_ROWS == 0
        for t in range(width // PACK_ROWS):
            small.append(0); a.append(src + t * PACK_ROWS); b.append(_SRC['ggf'])
    assert GATE_COLS + COL_KPE == len(a) * PACK_ROWS and COL_GG == COL_KPE + 128
    small.append(1); a.append(_SRC['kpe']); b.append(_SRC['ggf'])
    assert len(a) == PACK_TILES
    return np.asarray([small, a, b], np.int32)


def _pack_kernel(plan_ref, a3_ref, b3_ref, o_ref):
    is_small = plan_ref[0, pl.program_id(1)] == 1
    a_ref, b_ref = a3_ref.at[0], b3_ref.at[0]

    @pl.when(jnp.logical_not(is_small))
    def _():
        o_ref[...] = a_ref[...].astype(BF16)

    @pl.when(is_small)
    def _():
        o_ref[...] = jnp.zeros_like(o_ref)
        o_ref[0:MLA_ROPE, :] = a_ref[0:MLA_ROPE, :].astype(BF16)
        o_ref[128:128 + 2 * GLA_GATE_RANK, :] = b_ref[0:2 * GLA_GATE_RANK, :].astype(BF16)


def _pack_w_in(w_in):
    w_t = jnp.swapaxes(w_in, 1, 2)
    window = lambda which: pl.BlockSpec((pl.Element(1), pl.Element(PACK_ROWS), pl.Element(D_MODEL)),
                                        lambda l, j, plan: (l, pl.multiple_of(plan[which, j], 32), 0))
    return pl.pallas_call(
        _pack_kernel,
        grid_spec=pltpu.PrefetchScalarGridSpec(
            num_scalar_prefetch=1, grid=(DEPTH, PACK_TILES),
            in_specs=[window(1), window(2)],
            out_specs=pl.BlockSpec((None, PACK_ROWS, D_MODEL), lambda l, j, plan: (l, j, 0))),
        out_shape=jax.ShapeDtypeStruct((DEPTH, PROJ_COLS, D_MODEL), BF16),
        compiler_params=pltpu.CompilerParams(
            dimension_semantics=("arbitrary", "arbitrary"),
            vmem_limit_bytes=_vmem_limit(2 * PACK_ROWS * D_MODEL * (4 + 4 + 2))),
        name="pack_w_in",
    )(jnp.asarray(_pack_plan()), w_t, w_t)


def _proj_kernel(x_ref, shift_ref, scale_ref, g_ref, w_ref, gates_ref, main_ref, h_ref,
                 *, layer, base_row, rows_per_cond):
    i = pl.program_id(0)
    j = pl.program_id(1)
    row = _cond_row(i, x_ref.shape[0], base_row, rows_per_cond)

    @pl.when(j == 0)
    def _():
        _modulated_norm_to(h_ref, x_ref, g_ref[pl.ds(layer, 1), :],
                           shift_ref[pl.ds(row, 1), :], scale_ref[pl.ds(row, 1), :])

    @pl.when(j < GATE_COLS // PROJ_TN)
    def _():
        gates_ref[...] = lax.dot_general(h_ref[...], w_ref[...], NT_DIMS, preferred_element_type=F32).astype(BF16)

    @pl.when(j >= GATE_COLS // PROJ_TN)
    def _():
        main_ref[...] = lax.dot_general(h_ref[...], w_ref[...], NT_DIMS, preferred_element_type=F32)


def _in_proj(x, mods, layer, g_norm2, w_packed, base_row, rows_per_cond):
    m = x.shape[0]
    n_gate = GATE_COLS // PROJ_TN
    mod_spec = lambda k: pl.BlockSpec((None, None, 8, D_MODEL), lambda i, j: (layer, 3 + k, 0, 0))
    est = 2 * PROJ_TM * D_MODEL * 4 + PROJ_TM * D_MODEL * 2 + 2 * D_MODEL * PROJ_TN * 2 + 2 * PROJ_TM * PROJ_TN * 6
    return pl.pallas_call(
        functools.partial(_proj_kernel, layer=layer, base_row=base_row, rows_per_cond=rows_per_cond),
        grid=(m // PROJ_TM, PROJ_COLS // PROJ_TN),
        in_specs=[
            pl.BlockSpec((PROJ_TM, D_MODEL), lambda i, j: (i, 0)),
            mod_spec(0), mod_spec(1),
            pl.BlockSpec((DEPTH, D_MODEL), lambda i, j: (0, 0)),
            pl.BlockSpec((None, PROJ_TN, D_MODEL), lambda i, j: (layer, j, 0)),
        ],
        out_specs=[pl.BlockSpec((PROJ_TM, PROJ_TN), lambda i, j: (i, jnp.minimum(j, n_gate - 1))),
                   pl.BlockSpec((PROJ_TM, PROJ_TN), lambda i, j: (i, jnp.maximum(j - n_gate, 0)))],
        out_shape=[jax.ShapeDtypeStruct((m, GATE_COLS), BF16), jax.ShapeDtypeStruct((m, MAIN_COLS), F32)],
        scratch_shapes=[pltpu.VMEM((PROJ_TM, D_MODEL), BF16)],
        compiler_params=pltpu.CompilerParams(
            dimension_semantics=("arbitrary", "arbitrary"), vmem_limit_bytes=_vmem_limit(est)),
        name="in_proj",
    )(x, mods, mods, g_norm2, w_packed)


def _rope_tables(t):
    pos = jnp.arange(t)
    inv_freq = ROPE_BASE ** (-jnp.arange(16, dtype=F32) / 16)

    def cs(p):
        ang = p.astype(F32)[:, None] * inv_freq[None, :]
        return jnp.concatenate([jnp.cos(ang)] * 2, axis=1), jnp.concatenate([jnp.sin(ang)] * 2, axis=1)

    cr, sr = cs(pos // GRID_W)
    cc, sc = cs(pos % GRID_W)
    sign = np.where(np.arange(64) % 32 < 16, -1.0, 1.0).astype(np.float32)
    return jnp.concatenate([cr, cc], axis=1), jnp.concatenate([sr, sc], axis=1) * sign[None, :]


MLA_TR = 256
MLA_TQ = 256


def _pack_mla(w_uq_l, g_qn_l, g_kn_l):
    w = w_uq_l.reshape(MLA_Q_LORA, MLA_HEADS, MLA_QK)
    z64 = jnp.zeros((MLA_Q_LORA, MLA_HEADS, 64), w.dtype)
    wq = jnp.concatenate([w, z64], axis=-1).reshape(MLA_Q_LORA, -1).astype(BF16)
    v64 = jnp.zeros((64,), F32)
    gq_full = jnp.concatenate([g_qn_l, v64])[None]
    gk_n = g_kn_l[:MLA_NOPE][None]
    gk_r = jnp.concatenate([g_kn_l[MLA_NOPE:], v64])[None]
    return wq, gq_full, gk_n, gk_r


def _mla_kv_kernel(*refs, layer, normalize, rope, emit_ckv):
    it = iter(refs)
    ckv_ref, kpe_ref = next(it), next(it)
    if rope:
        c_ref, s_ref = next(it), next(it)
    w_ref, gkv_ref, gn_ref, gr_ref = next(it), next(it), next(it), next(it)
    k_ref, v_ref = next(it), next(it)
    ckv = ckv_ref[...]
    if normalize:
        ckv = (ckv * lax.rsqrt(jnp.mean(ckv * ckv, axis=-1, keepdims=True) + NORM_EPS)) * gkv_ref[pl.ds(layer, 1), :]
    if emit_ckv:
        next(it)[...] = ckv
    kv = jnp.dot(ckv.astype(BF16), w_ref[...].astype(BF16), preferred_element_type=F32)
    kpe = kpe_ref[...]
    ss_pe = jnp.sum(kpe * kpe, axis=-1, keepdims=True)
    kr = kpe * gr_ref[...]
    if rope:
        kr = kr * c_ref[...] + _rope_partner(kr) * s_ref[...]
    for h in range(MLA_HEADS):
        kn = kv[:, 256 * h:256 * h + 128]
        r = lax.rsqrt((jnp.sum(kn * kn, axis=-1, keepdims=True) + ss_pe) * (1.0 / MLA_QK) + NORM_EPS)
        k_ref[:, 256 * h:256 * h + 128] = ((kn * r) * gn_ref[...]).astype(BF16)
        k_ref[:, 256 * h + 128:256 * h + 256] = (kr * r).astype(BF16)
        v_ref[:, 128 * h:128 * h + 128] = kv[:, 256 * h + 128:256 * h + 256].astype(BF16)


def _mla_kv(ckv_src, ckv_spec, kpe_src, kpe_spec, rope_tabs, w_ukv, g_kv, gk_n, gk_r,
            batch, rows, layer, normalize, emit_ckv):
    rope = rope_tabs is not None
    nt = rows // MLA_TR
    ins = [ckv_src, kpe_src]
    specs = [ckv_spec, kpe_spec]
    if rope:
        ins += list(rope_tabs)
        specs += [pl.BlockSpec((MLA_TR, 128), lambda b, t: (t, 0))] * 2
    ins += [w_ukv, g_kv, gk_n, gk_r]
    specs += [pl.BlockSpec((None, MLA_KV_LORA, 2048), lambda b, t: (layer, 0, 0)),
              pl.BlockSpec((DEPTH, MLA_KV_LORA), lambda b, t: (0, 0))] + [pl.BlockSpec((1, 128), lambda b, t: (0, 0))] * 2
    out_shape = [jax.ShapeDtypeStruct((batch, rows, MLA_HEADS * MLA_HEAD_PAD), BF16),
                 jax.ShapeDtypeStruct((batch, rows, MLA_WIDTH), BF16)]
    out_specs = [pl.BlockSpec((None, MLA_TR, MLA_HEADS * MLA_HEAD_PAD), lambda b, t: (b, t, 0)),
                 pl.BlockSpec((None, MLA_TR, MLA_WIDTH), lambda b, t: (b, t, 0))]
    if emit_ckv:
        out_shape.append(jax.ShapeDtypeStruct((batch, rows, MLA_KV_LORA), F32))
        out_specs.append(pl.BlockSpec((None, MLA_TR, MLA_KV_LORA), lambda b, t: (b, t, 0)))
    est = 2 * MLA_KV_LORA * 2048 * 4 + 4 * MLA_TR * 2048 * 4
    return pl.pallas_call(
        functools.partial(_mla_kv_kernel, layer=layer, normalize=normalize, rope=rope, emit_ckv=emit_ckv),
        grid=(batch, nt), in_specs=specs, out_specs=out_specs, out_shape=out_shape,
        compiler_params=pltpu.CompilerParams(
            dimension_semantics=("arbitrary", "arbitrary"), vmem_limit_bytes=_vmem_limit(est)),
        name="mla_kv",
    )(*ins)


def _mla_attn_kernel(*refs, layer, rope, has_ctx):
    it = iter(refs)
    cq_ref, wq_ref, gq_ref, gfull_ref = next(it), next(it), next(it), next(it)
    if rope:
        c_ref, s_ref = next(it), next(it)
    k_ref, v_ref = next(it), next(it)
    if has_ctx:
        kc_ref, vc_ref = next(it), next(it)
    o_ref = next(it)
    cq = cq_ref[...]
    ql = ((cq * lax.rsqrt(jnp.mean(cq * cq, axis=-1, keepdims=True) + NORM_EPS)) * gq_ref[pl.ds(layer, 1), :]).astype(BF16)
    q_raw = jnp.dot(ql, wq_ref[...], preferred_element_type=F32)
    for h in range(MLA_HEADS):
        sl = slice(256 * h, 256 * h + 256)
        vs = slice(128 * h, 128 * h + 128)
        q = q_raw[:, sl]
        r = lax.rsqrt(jnp.sum(q * q, axis=-1, keepdims=True) * (1.0 / MLA_QK) + NORM_EPS)
        qh = q * gfull_ref[...]
        if rope:
            q_rope = qh[:, 128:]
            qh = jnp.concatenate([qh[:, :128], q_rope * c_ref[...] + _rope_partner(q_rope) * s_ref[...]], axis=1)
        qh = ((qh * r) * (MLA_SCALE * LOG2_E)).astype(BF16)
        s = lax.dot_general(qh, k_ref[:, sl], NT_DIMS, preferred_element_type=F32)
        m = jnp.max(s, axis=-1, keepdims=True)
        if has_ctx:
            sc = lax.dot_general(qh, kc_ref[:, sl], NT_DIMS, preferred_element_type=F32)
            m = jnp.maximum(m, jnp.max(sc, axis=-1, keepdims=True))
        e = jnp.exp2(s - m)
        den = jnp.sum(e, axis=-1, keepdims=True)
        o = jnp.dot(e.astype(BF16), v_ref[:, vs], preferred_element_type=F32)
        if has_ctx:
            ec = jnp.exp2(sc - m)
            den = den + jnp.sum(ec, axis=-1, keepdims=True)
            o = o + jnp.dot(ec.astype(BF16), vc_ref[:, vs], preferred_element_type=F32)
        o_ref[:, vs] = (o * (1.0 / den)).astype(BF16)


def _mla_attn(proj, wq, g_q, gq_full, rope_tabs, k, v, ctx_kv, batch, t, layer):
    rope = rope_tabs is not None
    has_ctx = ctx_kv is not None
    nq = t // MLA_TQ
    const = lambda shape: pl.BlockSpec(shape, lambda b, i: (0,) * len(shape))
    ins = [proj, wq, g_q, gq_full]
    specs = [pl.BlockSpec((MLA_TQ, MLA_Q_LORA), lambda b, i: (b * nq + i, COL_CQ // MLA_Q_LORA)),
             const((MLA_Q_LORA, 2048)), const((DEPTH, MLA_Q_LORA)), const((1, 256))]
    if rope:
        ins += list(rope_tabs)
        specs += [pl.BlockSpec((MLA_TQ, 128), lambda b, i: (i, 0))] * 2
    ins += [k, v]
    specs += [pl.BlockSpec((None, t, 2048), lambda b, i: (b, 0, 0)), pl.BlockSpec((None, t, MLA_WIDTH), lambda b, i: (b, 0, 0))]
    est = 2 * (t * 2048 * 2 + t * MLA_WIDTH * 2) + 6 * MLA_TQ * t * 4
    if has_ctx:
        kc, vc = ctx_kv
        lc = kc.shape[1]
        ins += [kc, vc]
        specs += [pl.BlockSpec((None, lc, 2048), lambda b, i: (b, 0, 0)), pl.BlockSpec((None, lc, MLA_WIDTH), lambda b, i: (b, 0, 0))]
        est += 2 * lc * 3072 * 2
    return pl.pallas_call(
        functools.partial(_mla_attn_kernel, layer=layer, rope=rope, has_ctx=has_ctx),
        grid=(batch, nq), in_specs=specs,
        out_specs=pl.BlockSpec((MLA_TQ, MLA_WIDTH), lambda b, i: (b * nq + i, 0)),
        out_shape=jax.ShapeDtypeStruct((batch * t, MLA_WIDTH), BF16),
        compiler_params=pltpu.CompilerParams(
            dimension_semantics=("arbitrary", "arbitrary"), vmem_limit_bytes=_vmem_limit(est)),
        name="mla_attn",
    )(*ins)


def _mla_branch(proj, batch, t, layer, mla_w, w_ukv, g_q, g_kv, rope, ctx):
    wq, gq_full, gk_n, gk_r = mla_w
    nt = t // MLA_TR
    ckv_spec = pl.BlockSpec((MLA_TR, MLA_KV_LORA), lambda b, i: (b * nt + i, COL_CKV // MLA_KV_LORA))
    kpe_spec = pl.BlockSpec((MLA_TR, 128), lambda b, i: (b * nt + i, COL_KPE // 128))
    if rope is None:
        k, v, ckv_n = _mla_kv(proj, ckv_spec, proj, kpe_spec, None, w_ukv, g_kv, gk_n, gk_r,
                              batch, t, layer, normalize=True, emit_ckv=True)
        return _mla_attn(proj, wq, g_q, gq_full, None, k, v, None, batch, t, layer), ckv_n
    c64, s64 = rope
    z64 = jnp.zeros_like(c64)
    tabs = (jnp.concatenate([c64, z64], axis=1), jnp.concatenate([s64, z64], axis=1))
    k, v = _mla_kv(proj, ckv_spec, proj, kpe_spec, tabs, w_ukv, g_kv, gk_n, gk_r,
                   batch, t, layer, normalize=True, emit_ckv=False)
    ctx_ckv, ctx_kpe = ctx
    lc = ctx_ckv.shape[2]
    kc, vc = _mla_kv(ctx_ckv, pl.BlockSpec((None, None, MLA_TR, MLA_KV_LORA), lambda b, i: (b, layer, i, 0)),
                     ctx_kpe, pl.BlockSpec((None, None, MLA_TR, 128), lambda b, i: (b, layer, i, 0)),
                     None, w_ukv, g_kv, gk_n, gk_r, batch, lc, layer, normalize=False, emit_ckv=False)
    return _mla_attn(proj, wq, g_q, gq_full, tabs, k, v, (kc, vc), batch, t, layer), None


SWA_TR = 128
SWA_NEG = -1e30


def _pair_sum_matrix():
    g = (np.arange(128)[:, None] // 64 == np.arange(128)[None, :] // 64).astype(np.float32)
    return jnp.asarray(np.concatenate([g, g], axis=0), BF16)


def _group_rms_scale(x, pair_ref):
    sq = x * x
    hi = sq.astype(BF16)
    lo = (sq - hi.astype(F32)).astype(BF16)
    ss = jnp.dot(jnp.concatenate([hi, lo], axis=1), pair_ref[...], preferred_element_type=F32)
    return lax.rsqrt(ss * (1.0 / SWA_HEAD_DIM) + NORM_EPS)


def _swa_kv_kernel(*refs, normalize, rope, emit, pad_blocks, layer):
    it = iter(refs)
    k_ref, v_ref = next(it), next(it)
    if rope:
        c_ref, s_ref = next(it), next(it)
    if normalize:
        g_ref, pair_ref = next(it), next(it)
    ko_ref, vo_ref = next(it), next(it)
    if emit:
        kn_ref, vn_ref = next(it), next(it)
    lane = lax.broadcasted_iota(jnp.int32, (SWA_TR, 128), 1)
    low = lane < 64

    def halves(x, c, o_ref):
        sw = pltpu.roll(x, 64, axis=1)
        zero = jnp.zeros_like(x)
        o_ref[4 * c + 0] = jnp.where(low, x, zero).astype(BF16)
        o_ref[4 * c + 1] = jnp.where(low, zero, sw).astype(BF16)
        o_ref[4 * c + 2] = jnp.where(low, sw, zero).astype(BF16)
        o_ref[4 * c + 3] = jnp.where(low, zero, x).astype(BF16)

    def compute():
        for c in range(2):
            sl = slice(128 * c, 128 * c + 128)
            kb = k_ref[:, sl]
            if normalize:
                r = _group_rms_scale(kb, pair_ref)
                kn = (kb * r) * g_ref[...]
                if emit:
                    kn_ref[:, sl] = kn
                if rope:
                    kn = kn * c_ref[...] + _rope_partner(kn) * s_ref[...]
            else:
                kn = kb
            halves(kn, c, ko_ref)
            vb = v_ref[:, sl]
            if emit:
                vn_ref[:, sl] = vb
            halves(vb, c, vo_ref)

    if pad_blocks:
        t = pl.program_id(1)
        is_pad = jnp.logical_or(t == 0, t == pl.num_programs(1) - 1)

        @pl.when(is_pad)
        def _():
            ko_ref[...] = jnp.zeros_like(ko_ref)
            vo_ref[...] = jnp.zeros_like(vo_ref)

        pl.when(jnp.logical_not(is_pad))(compute)
    else:
        compute()


def _swa_kv(k_src, k_spec, v_src, v_spec, rope_tabs, norm_ins, batch, rows, layer, emit, pad_blocks):
    rope = rope_tabs is not None
    normalize = norm_ins is not None
    nt = rows // SWA_TR + (2 if pad_blocks else 0)
    ins, specs = [k_src, v_src], [k_spec, v_spec]
    if rope:
        tab_row = (lambda t: jnp.clip(t - 1, 0, nt - 3)) if pad_blocks else (lambda t: t)
        ins += list(rope_tabs)
        specs += [pl.BlockSpec((SWA_TR, 128), lambda b, t: (tab_row(t), 0))] * 2
    if normalize:
        ins += list(norm_ins)
        specs += [pl.BlockSpec((1, 128), lambda b, t: (0, 0)), pl.BlockSpec((256, 128), lambda b, t: (0, 0))]
    out_shape = [jax.ShapeDtypeStruct((batch, 8, nt * SWA_TR, 128), BF16)] * 2
    out_specs = [pl.BlockSpec((None, 8, SWA_TR, 128), lambda b, t: (b, 0, t, 0))] * 2
    if emit:
        out_shape += [jax.ShapeDtypeStruct((batch, rows, 256), F32)] * 2
        out_specs += [pl.BlockSpec((None, SWA_TR, 256), lambda b, t: (b, t, 0))] * 2
    return pl.pallas_call(
        functools.partial(_swa_kv_kernel, normalize=normalize, rope=rope, emit=emit, pad_blocks=pad_blocks, layer=layer),
        grid=(batch, nt), in_specs=specs, out_specs=out_specs, out_shape=out_shape,
        compiler_params=pltpu.CompilerParams(dimension_semantics=("arbitrary", "arbitrary")),
        name="swa_kv",
    )(*ins)


def _swa_attn_kernel(*refs, rope, windowed, has_ctx, t_total, layer):
    it = iter(refs)
    sink_ref, q_ref = next(it), next(it)
    if rope:
        c_ref, s_ref = next(it), next(it)
    g_ref, pair_ref = next(it), next(it)
    k_ref, v_ref = next(it), next(it)
    if has_ctx:
        kc_ref, vc_ref = next(it), next(it)
    o_ref = next(it)
    n = pl.program_id(1)
    if windowed:
        start = pl.multiple_of(n * SWA_TR, SWA_TR)
        win = pl.ds(start, 3 * SWA_TR)
        r_i = lax.broadcasted_iota(jnp.int32, (2 * SWA_TR, 3 * SWA_TR), 0) & (SWA_TR - 1)
        c_i = lax.broadcasted_iota(jnp.int32, (2 * SWA_TR, 3 * SWA_TR), 1)
        kpos = (n - 1) * SWA_TR + c_i
        diff = SWA_TR + r_i - c_i
        valid2 = (kpos >= 0) & (kpos < t_total) & (diff <= SWA_WINDOW) & (diff >= -SWA_WINDOW)
    else:
        win = slice(None)
    def normed_queries(cb):
        sl = slice(128 * cb, 128 * cb + 128)
        qb = q_ref[:, sl]
        r = _group_rms_scale(qb, pair_ref)
        qn = (qb * r) * g_ref[...]
        if rope:
            qn = qn * c_ref[...] + _rope_partner(qn) * s_ref[...]
        return (qn * (SWA_SCALE * LOG2_E)).astype(BF16)

    tq = q_ref.shape[0]
    low_lanes = lax.broadcasted_iota(jnp.int32, (2 * tq, 128), 1) < SWA_HEAD_DIM

    lk = 3 * SWA_TR if windowed else k_ref.shape[1]
    lc = kc_ref.shape[1] if has_ctx else 0
    row_blk = lax.broadcasted_iota(jnp.int32, (2 * tq, 1), 0) // tq
    for j in range(SWA_KV_HEADS):
        q2 = jnp.concatenate([normed_queries(2 * j), normed_queries(2 * j + 1)], axis=0)
        k_parts, v_parts = [], []
        for e in range(2):
            k_parts.append(k_ref[2 * j + e, win, :])
            v_parts.append(v_ref[2 * j + e, win, :])
            if has_ctx:
                k_parts.append(kc_ref[2 * j + e])
                v_parts.append(vc_ref[2 * j + e])
        s = lax.dot_general(q2, jnp.concatenate(k_parts, axis=0), NT_DIMS, preferred_element_type=F32)
        p_parts, inv = [], []
        for e in range(2):
            off = e * (lk + lc)
            sink = jnp.where(row_blk == 0, sink_ref[layer, 4 * j + e], sink_ref[layer, 4 * j + 2 + e]) * LOG2_E
            s_loc = s[:, off:off + lk]
            if windowed:
                s_loc = jnp.where(valid2, s_loc, SWA_NEG)
            m = jnp.maximum(jnp.max(s_loc, axis=-1, keepdims=True), sink)
            if has_ctx:
                s_ctx = s[:, off + lk:off + lk + lc]
                m = jnp.maximum(m, jnp.max(s_ctx, axis=-1, keepdims=True))
            p = jnp.exp2(s_loc - m)
            den = jnp.sum(p, axis=-1, keepdims=True) + jnp.exp2(sink - m)
            p_parts.append(p.astype(BF16))
            if has_ctx:
                pc = jnp.exp2(s_ctx - m)
                den = den + jnp.sum(pc, axis=-1, keepdims=True)
                p_parts.append(pc.astype(BF16))
            inv.append(1.0 / den)
        o = jnp.dot(jnp.concatenate(p_parts, axis=1), jnp.concatenate(v_parts, axis=0), preferred_element_type=F32)
        o = o * jnp.where(low_lanes, inv[0], inv[1])
        o_ref[:, 256 * j:256 * j + 128] = o[:tq].astype(BF16)
        o_ref[:, 256 * j + 128:256 * j + 256] = o[tq:].astype(BF16)


def _swa_attn(sink, proj, rope_tabs, norm_ins, k, v, ctx_kv, batch, t, layer):
    rope = rope_tabs is not None
    has_ctx = ctx_kv is not None
    tq = SWA_TR if has_ctx else 2 * SWA_TR
    nq = t // tq
    ins = [sink, proj]
    specs = [pl.BlockSpec(memory_space=pltpu.SMEM),
             pl.BlockSpec((tq, SWA_WIDTH), lambda b, i: (b * nq + i, COL_SQ // SWA_WIDTH))]
    if rope:
        ins += list(rope_tabs)
        specs += [pl.BlockSpec((tq, 128), lambda b, i: (i, 0))] * 2
    ins += list(norm_ins)
    specs += [pl.BlockSpec((1, 128), lambda b, i: (0, 0)), pl.BlockSpec((256, 128), lambda b, i: (0, 0))]
    lk = k.shape[2]
    ins += [k, v]
    specs += [pl.BlockSpec((None, 8, lk, 128), lambda b, i: (b, 0, 0, 0))] * 2
    if has_ctx:
        lc = ctx_kv[0].shape[2]
        ins += list(ctx_kv)
        specs += [pl.BlockSpec((None, 8, lc, 128), lambda b, i: (b, 0, 0, 0))] * 2
    return pl.pallas_call(
        functools.partial(_swa_attn_kernel, rope=rope, windowed=has_ctx, has_ctx=has_ctx, t_total=t, layer=layer),
        grid=(batch, nq), in_specs=specs,
        out_specs=pl.BlockSpec((tq, SWA_WIDTH), lambda b, i: (b * nq + i, 0)),
        out_shape=jax.ShapeDtypeStruct((batch * t, SWA_WIDTH), BF16),
        compiler_params=pltpu.CompilerParams(dimension_semantics=("arbitrary", "arbitrary")),
        name="swa_attn",
    )(*ins)


def _swa_branch(proj, batch, t, layer, g_qn_l, g_kn_l, sink, rope, ctx):
    pair = _pair_sum_matrix()
    tile2 = lambda g: jnp.concatenate([g, g])[None]
    nt = t // SWA_TR
    if rope is None:
        k_spec = pl.BlockSpec((SWA_TR, 256), lambda b, i: (b * nt + i, COL_SK // 256))
        v_spec = pl.BlockSpec((SWA_TR, 256), lambda b, i: (b * nt + i, COL_SV // 256))
        k, v, k_n, v_raw = _swa_kv(proj, k_spec, proj, v_spec, None, (tile2(g_kn_l), pair),
                                   batch, t, layer, emit=True, pad_blocks=False)
        o = _swa_attn(sink, proj, None, (tile2(g_qn_l), pair), k, v, None, batch, t, layer)
        return o, k_n, v_raw
    c64, s64 = rope
    tabs = (jnp.concatenate([c64, c64], axis=1), jnp.concatenate([s64, s64], axis=1))
    row = lambda b, i: b * nt + jnp.clip(i - 1, 0, nt - 1)
    k_spec = pl.BlockSpec((SWA_TR, 256), lambda b, i: (row(b, i), COL_SK // 256))
    v_spec = pl.BlockSpec((SWA_TR, 256), lambda b, i: (row(b, i), COL_SV // 256))
    k, v = _swa_kv(proj, k_spec, proj, v_spec, tabs, (tile2(g_kn_l), pair), batch, t, layer, emit=False, pad_blocks=True)
    ctx_k, ctx_v = ctx
    lc = ctx_k.shape[2]
    c_spec = pl.BlockSpec((None, None, SWA_TR, 256), lambda b, i: (b, layer, i, 0))
    kc, vc = _swa_kv(ctx_k, c_spec, ctx_v, c_spec, None, None, batch, lc, layer, emit=False, pad_blocks=False)
    o = _swa_attn(sink, proj, tabs, (tile2(g_qn_l), pair), k, v, (kc, vc), batch, t, layer)
    return o, None, None


GLA_LEVELS = (64, 32, 16, 8, 4, 2, 1)
GLA_MXU_LEVELS = (2, 1)
GLA_ROWSETS = 1 + len(GLA_MXU_LEVELS)


def _gla_sum_matrix(backward):
    c = GLA_CHUNK
    t = np.arange(c)[:, None]
    j = np.arange(c)[None, :]
    sets = [j <= t]
    for g in GLA_MXU_LEVELS:
        e = (t // (2 * g)) * 2 * g + g - 1
        upper = (t // g) % 2 == 1
        sets.append(np.where(upper, (j > e) & (j <= t), (j > t) & (j <= e)))
    n = np.concatenate(sets, axis=0).astype(np.float32)
    if backward:
        n = n.reshape(GLA_ROWSETS, c, c)[:, ::-1, ::-1].reshape(GLA_ROWSETS * c, c)
    return jnp.asarray(np.concatenate([n, n, n], axis=1), BF16)


def _gla_kernel(*refs, has_s0):
    it = iter(refs)
    srcs = [tuple(next(it) for _ in range(4)) for _ in range(2)]
    nmat = (next(it), next(it))
    wg = (next(it), next(it))
    bg = (next(it), next(it))
    s0_ref = next(it) if has_s0 else None
    o_refs = (next(it), next(it))
    sfin_ref = next(it)
    s_ref = next(it)
    i = pl.program_id(1)
    c = GLA_CHUNK

    @pl.when(i == 0)
    def _():
        s_ref[...] = s0_ref[...] if has_s0 else jnp.zeros_like(s_ref)

    row = lax.broadcasted_iota(jnp.int32, (c, c), 0)
    col = lax.broadcasted_iota(jnp.int32, (c, c), 1)
    diag = row == col
    log2 = lambda g: int(g).bit_length() - 1
    second_half = [((row >> log2(g)) & 1) == 1 for g in GLA_LEVELS]
    same_block = [(row >> (log2(g) + 1)) == (col >> (log2(g) + 1)) for g in GLA_LEVELS]
    for d in range(2):
        q_ref, k_ref, v_ref, gg_ref = srcs[d]
        z = jnp.dot(gg_ref[...].astype(BF16), wg[d][...], preferred_element_type=F32) + bg[d][...]
        la = (jnp.minimum(z, 0.0) - jnp.log1p(jnp.exp(-jnp.abs(z)))) * (1.0 / GLA_GATE_NORM)
        hi = la.astype(BF16)
        r1 = la - hi.astype(F32)
        mid = r1.astype(BF16)
        lo = (r1 - mid.astype(F32)).astype(BF16)
        ex_all = jnp.dot(nmat[d][...], jnp.concatenate([hi, mid, lo], axis=0), preferred_element_type=F32)
        for h in range(GLA_HEADS):
            sl = slice(GLA_DK * h, GLA_DK * h + GLA_DK)
            vs = slice(GLA_DV * h, GLA_DV * h + GLA_DV)
            ex = ex_all[:, sl]
            q = q_ref[:, sl] * (GLA_DK ** -0.5)
            k = k_ref[:, sl]
            v = v_ref[:, vs].astype(BF16)
            b = ex[0:c]
            eb = jnp.exp(b)
            b_exit = b[c - 1:c] if d == 0 else b[0:1]
            ek = jnp.exp(b_exit - b)
            s_old = s_ref[d, h]
            o = jnp.dot((q * eb).astype(BF16), s_old.astype(BF16), preferred_element_type=F32)
            a = jnp.where(diag, jnp.sum(q * k, axis=-1, keepdims=True), 0.0)
            for li, g in enumerate(GLA_LEVELS):
                if g in GLA_MXU_LEVELS:
                    r = 1 + GLA_MXU_LEVELS.index(g)
                    eg = jnp.exp(ex[r * c:(r + 1) * c])
                else:
                    b3 = b.reshape(c // (2 * g), 2 * g, GLA_DK)
                    edge = g - 1 if d == 0 else g
                    b_edge = jnp.broadcast_to(b3[:, edge:edge + 1, :], b3.shape).reshape(c, GLA_DK)
                    eg = jnp.exp(-jnp.abs(b - b_edge))
                qe, ke = q * eg, k * eg
                late = second_half[li]
                qg = (jnp.where(late, qe, 0.0) if d == 0 else jnp.where(late, 0.0, qe)).astype(BF16)
                kg = (jnp.where(late, 0.0, ke) if d == 0 else jnp.where(late, ke, 0.0)).astype(BF16)
                ag = lax.dot_general(qg, kg, NT_DIMS, preferred_element_type=F32)
                if 2 * g < c:
                    ag = jnp.where(same_block[li], ag, 0.0)
                a = a + ag
            o = o + jnp.dot(a.astype(BF16), v, preferred_element_type=F32)
            o_refs[d][:, vs] = o
            a_col = eb.T[:, c - 1:c] if d == 0 else eb.T[:, 0:1]
            kt = (k * ek).T.astype(BF16)
            s_ref[d, h] = a_col * s_old + jnp.dot(kt, v, preferred_element_type=F32)

    @pl.when(i == pl.num_programs(1) - 1)
    def _():
        sfin_ref[...] = s_ref[...]


def _gla(proj, batch, t, layer, w_gf, b_gf, w_gb, b_gb, s0):
    c = GLA_CHUNK
    n = t // c
    fwd = lambda b, i: b * n + i
    bwd = lambda b, i: b * n + (n - 1 - i)
    ins, specs = [], []
    for rowf in (fwd, bwd):
        ins += [proj] * 4
        specs += [pl.BlockSpec((c, 512), lambda b, i, rowf=rowf: (rowf(b, i), COL_GQ // 512)),
                  pl.BlockSpec((c, 512), lambda b, i, rowf=rowf: (rowf(b, i), COL_GK // 512)),
                  pl.BlockSpec((c, GLA_WIDTH), lambda b, i, rowf=rowf: (rowf(b, i), COL_GV // GLA_WIDTH)),
                  pl.BlockSpec((c, 128), lambda b, i, rowf=rowf: (rowf(b, i), COL_GG // 128))]
    const = lambda shape: pl.BlockSpec(shape, lambda b, i: (0,) * len(shape))
    pad_w = lambda w, off: jnp.zeros((128, 512), F32).at[off:off + GLA_GATE_RANK].set(w).astype(BF16)
    ins += [_gla_sum_matrix(False), _gla_sum_matrix(True), pad_w(w_gf, 0), pad_w(w_gb, GLA_GATE_RANK), b_gf[None], b_gb[None]]
    specs += [const((GLA_ROWSETS * c, 3 * c))] * 2 + [const((128, 512))] * 2 + [const((1, 512))] * 2
    state_spec = pl.BlockSpec((None, 2, GLA_HEADS, GLA_DK, GLA_DV), lambda b, i: (b, 0, 0, 0, 0))
    if s0 is not None:
        ins.append(s0)
        specs.append(pl.BlockSpec((None, None, 2, GLA_HEADS, GLA_DK, GLA_DV), lambda b, i: (b, layer, 0, 0, 0, 0)))
    return pl.pallas_call(
        functools.partial(_gla_kernel, has_s0=s0 is not None),
        grid=(batch, n), in_specs=specs,
        out_specs=[pl.BlockSpec((c, GLA_WIDTH), lambda b, i: (fwd(b, i), 0)),
                   pl.BlockSpec((c, GLA_WIDTH), lambda b, i: (bwd(b, i), 0)), state_spec],
        out_shape=[jax.ShapeDtypeStruct((batch * t, GLA_WIDTH), F32)] * 2
        + [jax.ShapeDtypeStruct((batch, 2, GLA_HEADS, GLA_DK, GLA_DV), F32)],
        scratch_shapes=[pltpu.VMEM((2, GLA_HEADS, GLA_DK, GLA_DV), F32)],
        compiler_params=pltpu.CompilerParams(dimension_semantics=("arbitrary", "arbitrary")),
        name="gla",
    )(*ins)


GOUT_TM = 512


def _gla_out_kernel(of_ref, ob_ref, gate_ref, g_ref, o_ref, *, layer):
    g = g_ref[pl.ds(layer, 1), :]
    for h in range(GLA_HEADS):
        vs = slice(GLA_DV * h, GLA_DV * h + GLA_DV)
        o = of_ref[:, vs] + ob_ref[:, vs]
        y = (o * lax.rsqrt(jnp.mean(o * o, axis=-1, keepdims=True) + NORM_EPS)) * g
        gate = gate_ref[:, vs]
        o_ref[:, vs] = (y * (gate * jax.nn.sigmoid(gate))).astype(BF16)


def _gla_out(o_f, o_b, proj, g_gla_out, layer):
    m = o_f.shape[0]
    return pl.pallas_call(
        functools.partial(_gla_out_kernel, layer=layer),
        grid=(m // GOUT_TM,),
        in_specs=[pl.BlockSpec((GOUT_TM, GLA_WIDTH), lambda i: (i, 0)),
                  pl.BlockSpec((GOUT_TM, GLA_WIDTH), lambda i: (i, 0)),
                  pl.BlockSpec((GOUT_TM, GLA_WIDTH), lambda i: (i, COL_GOUT // GLA_WIDTH)),
                  pl.BlockSpec((DEPTH, GLA_DV), lambda i: (0, 0))],
        out_specs=pl.BlockSpec((GOUT_TM, GLA_WIDTH), lambda i: (i, 0)),
        out_shape=jax.ShapeDtypeStruct((m, GLA_WIDTH), BF16),
        compiler_params=pltpu.CompilerParams(dimension_semantics=("arbitrary",)),
        name="gla_out",
    )(o_f, o_b, proj, g_gla_out)


MERGE_TM = 1024
MERGE_TN = 512


def _merge_kernel(om_ref, og_ref, os_ref, gm_ref, gg_ref, gs_ref, wm_ref, wg_ref, ws_ref, o_ref):
    def branch(o_r, gate_r, w_r):
        return jax.nn.sigmoid(gate_r[...].astype(F32)) * jnp.dot(o_r[...], w_r[...], preferred_element_type=F32)

    merged = branch(om_ref, gm_ref, wm_ref) + branch(og_ref, gg_ref, wg_ref) + branch(os_ref, gs_ref, ws_ref)
    o_ref[...] = merged.astype(BF16)


def _merge(o_mla, o_gla, o_swa, gates, w_br_mla, w_br_gla, w_br_swa, layer):
    m = o_mla.shape[0]
    nn = D_MODEL // MERGE_TN
    o_spec = pl.BlockSpec((MERGE_TM, 1024), lambda i, j: (i, 0))
    gate_spec = lambda col: pl.BlockSpec((MERGE_TM, MERGE_TN), lambda i, j: (i, col // MERGE_TN + j))
    w_spec = pl.BlockSpec((None, 1024, MERGE_TN), lambda i, j: (layer, 0, j))
    est = 2 * (3 * MERGE_TM * 1024 * 2 + 3 * MERGE_TM * MERGE_TN * 2 + 3 * 1024 * MERGE_TN * 2 + MERGE_TM * MERGE_TN * 2)
    return pl.pallas_call(
        _merge_kernel,
        grid=(m // MERGE_TM, nn),
        in_specs=[o_spec, o_spec, o_spec, gate_spec(COL_GATE_MLA), gate_spec(COL_GATE_GLA), gate_spec(COL_GATE_SWA),
                  w_spec, w_spec, w_spec],
        out_specs=pl.BlockSpec((MERGE_TM, MERGE_TN), lambda i, j: (i, j)),
        out_shape=jax.ShapeDtypeStruct((m, D_MODEL), BF16),
        compiler_params=pltpu.CompilerParams(
            dimension_semantics=("arbitrary", "arbitrary"), vmem_limit_bytes=_vmem_limit(est)),
        name="merge",
    )(o_mla, o_gla, o_swa, gates, gates, gates, w_br_mla, w_br_gla, w_br_swa)


OUT_TM = 1024
OUT_TN = 512


def _out_proj_kernel(m_ref, w_ref, x_ref, gate_ref, o_ref, *, base_row, rows_per_cond):
    row = _cond_row(pl.program_id(0), m_ref.shape[0], base_row, rows_per_cond)
    y = jnp.dot(m_ref[...], w_ref[...], preferred_element_type=F32)
    o_ref[...] = x_ref[...] + gate_ref[pl.ds(row, 1), :] * y


def _out_proj(merged, x, mods, w_out, layer, base_row, rows_per_cond):
    m = x.shape[0]
    est = 2 * (OUT_TM * D_MODEL * 2 + D_MODEL * OUT_TN * 2 + 2 * OUT_TM * OUT_TN * 4)
    return pl.pallas_call(
        functools.partial(_out_proj_kernel, base_row=base_row, rows_per_cond=rows_per_cond),
        grid=(m // OUT_TM, D_MODEL // OUT_TN),
        in_specs=[pl.BlockSpec((OUT_TM, D_MODEL), lambda i, j: (i, 0)),
                  pl.BlockSpec((None, D_MODEL, OUT_TN), lambda i, j: (layer, 0, j)),
                  pl.BlockSpec((OUT_TM, OUT_TN), lambda i, j: (i, j)),
                  pl.BlockSpec((None, None, 8, OUT_TN), lambda i, j: (layer, 5, 0, j))],
        out_specs=pl.BlockSpec((OUT_TM, OUT_TN), lambda i, j: (i, j)),
        out_shape=jax.ShapeDtypeStruct((m, D_MODEL), F32),
        compiler_params=pltpu.CompilerParams(
            dimension_semantics=("arbitrary", "arbitrary"), vmem_limit_bytes=_vmem_limit(est)),
        name="out_proj",
    )(merged, w_out, x, mods)


def _trunk_layer(x, mods, layer, w, group):
    batch, t, base_row, rows_per_cond, rope, ctx = group
    x = _ffn(x, mods, layer, 0, w['g_norm1'], w['w_ff1_gu'], w['w_ff1_down'], base_row, rows_per_cond)
    gates, proj = _in_proj(x, mods, layer, w['g_norm2'], w['w_in_packed'], base_row, rows_per_cond)
    mla_ctx = swa_ctx = s0 = None
    if ctx is not None:
        mla_ctx, swa_ctx, s0 = ctx[:2], ctx[2:4], ctx[4]
    o_mla, ckv_n = _mla_branch(proj, batch, t, layer, w['mla_packed'][layer], w['w_mla_ukv'], w['g_mla_q'], w['g_mla_kv'],
                               rope, mla_ctx)
    o_swa, k_n, v_raw = _swa_branch(proj, batch, t, layer, w['g_swa_qn'][layer], w['g_swa_kn'][layer], w['swa_sink'],
                                    rope, swa_ctx)
    o_f, o_b, s_fin = _gla(proj, batch, t, layer, w['w_gla_gf'][layer], w['b_gla_gf'][layer],
                           w['w_gla_gb'][layer], w['b_gla_gb'][layer], s0)
    o_gla = _gla_out(o_f, o_b, proj, w['g_gla_out'], layer)
    merged = _merge(o_mla, o_gla, o_swa, gates, w['w_br_mla'], w['w_br_gla'], w['w_br_swa'], layer)
    x = _out_proj(merged, x, mods, w['w_out'], layer, base_row, rows_per_cond)
    x = _ffn(x, mods, layer, 6, w['g_norm3'], w['w_ff2_gu'], w['w_ff2_down'], base_row, rows_per_cond)
    new_ctx = None
    if ctx is None:
        kpe = proj[:, COL_KPE:COL_KPE + MLA_ROPE].reshape(batch, t, MLA_ROPE)
        new_ctx = (ckv_n, kpe, k_n.reshape(batch, t, SWA_KV_HEADS, SWA_HEAD_DIM),
                   v_raw.reshape(batch, t, SWA_KV_HEADS, SWA_HEAD_DIM), s_fin)
    return x, new_ctx


def kernel(x_prompt, x_sample, cache_mla_ckv, cache_mla_kpe, cache_swa_k, cache_swa_v, state_gla,
           c, c_ctx, w_ada, b_ada, g_norm1, g_norm2, g_norm3,
           w_ff1_gu, w_ff1_down, w_ff2_gu, w_ff2_down, w_in,
           g_mla_q, w_mla_uq, g_mla_kv, w_mla_ukv, g_mla_qn, g_mla_kn,
           w_gla_gf, b_gla_gf, w_gla_gb, b_gla_gb, g_gla_out,
           g_swa_qn, g_swa_kn, swa_sink, w_br_mla, w_br_gla, w_br_swa, w_out):
    bp, tp, _ = x_prompt.shape
    bs, ts, _ = x_sample.shape
    assert bs + 1 <= 8, "conditioning rows are packed into one 8-row tile"
    cond8 = jnp.zeros((8, D_MODEL), F32).at[0].set(c_ctx).at[1:1 + bs].set(c)
    mods = _adaln(cond8, w_ada, b_ada)
    w = dict(g_norm1=g_norm1, g_norm2=g_norm2, g_norm3=g_norm3,
             w_ff1_gu=w_ff1_gu, w_ff1_down=w_ff1_down, w_ff2_gu=w_ff2_gu, w_ff2_down=w_ff2_down,
             w_in_packed=_pack_w_in(w_in),
             mla_packed=[_pack_mla(w_mla_uq[l], g_mla_qn[l], g_mla_kn[l]) for l in range(DEPTH)],
             w_mla_ukv=w_mla_ukv, g_mla_q=g_mla_q, g_mla_kv=g_mla_kv,
             g_swa_qn=g_swa_qn, g_swa_kn=g_swa_kn, swa_sink=swa_sink,
             w_gla_gf=w_gla_gf, b_gla_gf=b_gla_gf, w_gla_gb=w_gla_gb, b_gla_gb=b_gla_gb, g_gla_out=g_gla_out,
             w_br_mla=w_br_mla.astype(BF16), w_br_gla=w_br_gla.astype(BF16), w_br_swa=w_br_swa.astype(BF16),
             w_out=w_out.astype(BF16))
    past = cache_mla_ckv.shape[2]
    ctx = (cache_mla_ckv, jnp.pad(cache_mla_kpe, ((0, 0), (0, 0), (0, 0), (0, 128 - MLA_ROPE))),
           cache_swa_k.reshape(bs, DEPTH, past, SWA_KV_HEADS * SWA_HEAD_DIM),
           cache_swa_v.reshape(bs, DEPTH, past, SWA_KV_HEADS * SWA_HEAD_DIM), state_gla)
    group_p = (bp, tp, 0, bp * tp, None, None)
    group_s = (bs, ts, 1, ts, _rope_tables(ts), ctx)
    y_p = x_prompt.reshape(bp * tp, D_MODEL)
    y_s = x_sample.reshape(bs * ts, D_MODEL)
    new = []
    for l in range(DEPTH):
        y_p, new_ctx = _trunk_layer(y_p, mods, l, w, group_p)
        new.append(new_ctx)
        y_s, _ = _trunk_layer(y_s, mods, l, w, group_s)
    stacked = tuple(jnp.stack([new[l][k] for l in range(DEPTH)], axis=1) for k in range(5))
    return (y_p.reshape(bp, tp, D_MODEL), y_s.reshape(bs, ts, D_MODEL)) + stacked
```

```python
import functools

import numpy as np
import jax
import jax.numpy as jnp
from jax import lax
from jax.experimental import pallas as pl
from jax.experimental.pallas import tpu as pltpu

F32 = jnp.float32
BF16 = jnp.bfloat16

D_MODEL = 2048
DEPTH = 2
GRID_W = 64
ROPE_BASE = 10000.0
NORM_EPS = 1e-6
MLA_HEADS = 8
MLA_Q_LORA = 512
MLA_KV_LORA = 512
MLA_NOPE = 128
MLA_ROPE = 64
MLA_V = 128
MLA_QK = MLA_NOPE + MLA_ROPE
MLA_SCALE = MLA_QK ** -0.5
MLA_WIDTH = MLA_HEADS * MLA_V
MLA_HEAD_PAD = 256
GLA_HEADS = 4
GLA_DK = 128
GLA_DV = 256
GLA_GATE_RANK = 16
GLA_GATE_NORM = 16.0
GLA_WIDTH = GLA_HEADS * GLA_DV
GLA_CHUNK = 128
SWA_HEADS = 16
SWA_KV_HEADS = 4
SWA_HEAD_DIM = 64
SWA_WINDOW = 128
SWA_SCALE = SWA_HEAD_DIM ** -0.5
SWA_WIDTH = SWA_HEADS * SWA_HEAD_DIM
D_FF = 5632
N_MOD = 9
IN_SPLITS = (MLA_Q_LORA, MLA_KV_LORA, MLA_ROPE,
             GLA_HEADS * GLA_DK, GLA_HEADS * GLA_DK, GLA_WIDTH, GLA_WIDTH, GLA_GATE_RANK, GLA_GATE_RANK,
             SWA_WIDTH, SWA_KV_HEADS * SWA_HEAD_DIM, SWA_KV_HEADS * SWA_HEAD_DIM,
             D_MODEL, D_MODEL, D_MODEL)

V7X_LANES = 128
V7X_SUBLANES = 8
V7X_VMEM_BYTES = 64 * 1024 * 1024
MIB = 1024 * 1024

NT_DIMS = (((1,), (1,)), ((), ()))
LOG2_E = 1.4426950408889634

COL_GATE_MLA = 0
COL_GATE_GLA = 2048
COL_GATE_SWA = 4096
COL_CQ = 6144
COL_CKV = 6656
COL_GQ = 7168
COL_GK = 7680
COL_GV = 8192
COL_GOUT = 9216
COL_SQ = 10240
COL_SK = 11264
COL_SV = 11520
COL_KPE = 11776
COL_GG = 11904
PROJ_COLS = 12288
PROJ_TN = 1024


def _vmem_limit(nbytes):
    return int(min(nbytes + 12 * MIB, V7X_VMEM_BYTES - 6 * MIB))


def _rope_partner(x):
    lane = lax.broadcasted_iota(jnp.int32, x.shape, 1)
    return jnp.where((lane & 31) < 16, pltpu.roll(x, 112, axis=1), pltpu.roll(x, 16, axis=1))


ADA_TN = 1024


def _adaln_kernel(cond_ref, w_ref, b_ref, o_ref):
    c = cond_ref[...]
    s = (c * jax.nn.sigmoid(c)).astype(BF16)
    o_ref[...] = jnp.dot(s, w_ref[...].astype(BF16), preferred_element_type=F32) + b_ref[...]


def _adaln(cond8, w_ada, b_ada):
    halves = D_MODEL // ADA_TN
    nj = N_MOD * halves
    return pl.pallas_call(
        _adaln_kernel,
        grid=(DEPTH, nj),
        in_specs=[
            pl.BlockSpec((8, D_MODEL), lambda l, j: (0, 0)),
            pl.BlockSpec((None, D_MODEL, ADA_TN), lambda l, j: (l, 0, j)),
            pl.BlockSpec((None, 1, ADA_TN), lambda l, j: (l, 0, j)),
        ],
        out_specs=pl.BlockSpec((None, None, 8, ADA_TN), lambda l, j: (l, j // halves, 0, j % halves)),
        out_shape=jax.ShapeDtypeStruct((DEPTH, N_MOD, 8, D_MODEL), F32),
        compiler_params=pltpu.CompilerParams(
            dimension_semantics=("arbitrary", "arbitrary"),
            vmem_limit_bytes=_vmem_limit(2 * D_MODEL * ADA_TN * 4)),
        name="adaln",
    )(cond8, w_ada, b_ada.reshape(DEPTH, 1, N_MOD * D_MODEL))


NORM_ROWS = 64


def _modulated_norm_to(h_ref, x_ref, g, shift, scale):
    gain = g * (1.0 + scale)

    def body(c, carry):
        r0 = pl.multiple_of(c * NORM_ROWS, NORM_ROWS)
        x = x_ref[pl.ds(r0, NORM_ROWS), :]
        ms = jnp.mean(x * x, axis=-1, keepdims=True)
        h_ref[pl.ds(r0, NORM_ROWS), :] = ((x * lax.rsqrt(ms + NORM_EPS)) * gain + shift).astype(BF16)
        return carry

    lax.fori_loop(0, x_ref.shape[0] // NORM_ROWS, body, 0, unroll=2)


def _cond_row(i, tm, base_row, rows_per_cond):
    return base_row + (i * tm) // rows_per_cond


FFN_TM = 1024
FFN_TF = 256
FFN_TN = 512


def _ffn_kernel(x_ref, shift_ref, scale_ref, gate_ref, g_ref, wg_ref, wu_ref, wd_ref, o_ref, h_ref,
                *, layer, base_row, rows_per_cond):
    i = pl.program_id(0)
    j = pl.program_id(1)
    row = _cond_row(i, x_ref.shape[0], base_row, rows_per_cond)

    @pl.when(j == 0)
    def _():
        _modulated_norm_to(h_ref, x_ref, g_ref[pl.ds(layer, 1), :],
                           shift_ref[pl.ds(row, 1), :], scale_ref[pl.ds(row, 1), :])
        o_ref[...] = jnp.zeros_like(o_ref)

    h = h_ref[...]
    a = jnp.dot(h, wg_ref[...].astype(BF16), preferred_element_type=F32)
    u = jnp.dot(h, wu_ref[...].astype(BF16), preferred_element_type=F32)
    act = (a * jax.nn.sigmoid(a) * u).astype(BF16)
    for n in range(0, D_MODEL, FFN_TN):
        o_ref[:, n:n + FFN_TN] += jnp.dot(act, wd_ref[:, n:n + FFN_TN].astype(BF16), preferred_element_type=F32)

    @pl.when(j == pl.num_programs(1) - 1)
    def _():
        o_ref[...] = x_ref[...] + (0.5 * gate_ref[pl.ds(row, 1), :]) * o_ref[...]


def _ffn(x, mods, layer, first_mod, g_norm, w_gu, w_down, base_row, rows_per_cond):
    m = x.shape[0]
    nf = D_FF // FFN_TF
    mod_spec = lambda k: pl.BlockSpec((None, None, 8, D_MODEL), lambda i, j: (layer, first_mod + k, 0, 0))
    est = (2 * FFN_TM * D_MODEL * 4 * 2 + FFN_TM * D_MODEL * 2
           + 2 * 3 * D_MODEL * FFN_TF * w_gu.dtype.itemsize)
    return pl.pallas_call(
        functools.partial(_ffn_kernel, layer=layer, base_row=base_row, rows_per_cond=rows_per_cond),
        grid=(m // FFN_TM, nf),
        in_specs=[
            pl.BlockSpec((FFN_TM, D_MODEL), lambda i, j: (i, 0)),
            mod_spec(0), mod_spec(1), mod_spec(2),
            pl.BlockSpec((DEPTH, D_MODEL), lambda i, j: (0, 0)),
            pl.BlockSpec((None, D_MODEL, FFN_TF), lambda i, j: (layer, 0, j)),
            pl.BlockSpec((None, D_MODEL, FFN_TF), lambda i, j: (layer, 0, j + nf)),
            pl.BlockSpec((None, FFN_TF, D_MODEL), lambda i, j: (layer, j, 0)),
        ],
        out_specs=pl.BlockSpec((FFN_TM, D_MODEL), lambda i, j: (i, 0)),
        out_shape=jax.ShapeDtypeStruct((m, D_MODEL), F32),
        scratch_shapes=[pltpu.VMEM((FFN_TM, D_MODEL), BF16)],
        compiler_params=pltpu.CompilerParams(
            dimension_semantics=("arbitrary", "arbitrary"), vmem_limit_bytes=_vmem_limit(est)),
        name="ffn",
    )(x, mods, mods, mods, g_norm, w_gu, w_gu, w_down)


PROJ_TM = 1024


IN_COLS = sum(IN_SPLITS)
_SRC = dict(zip(('cq', 'ckv', 'kpe', 'gq', 'gk', 'gv', 'gout', 'ggf', 'ggb', 'sq', 'sk', 'sv', 'gate_mla'),
                np.concatenate([[0], np.cumsum(IN_SPLITS)]).tolist()))
PACK_ROWS = 512
PACK_TILES = PROJ_COLS // PACK_ROWS


def _pack_plan():
    small, a, b = [], [], []
    for dst, src, width in ((COL_GATE_MLA, _SRC['gate_mla'], 3 * D_MODEL), (COL_CQ, _SRC['cq'], 1024),
                            (COL_GQ, _SRC['gq'], 3072), (COL_SQ, _SRC['sq'], 1536)):
        assert dst == len(a) * PACK_ROWS and width % PACK_ROWS == 0
        for t in range(width // PACK_ROWS):
            small.append(0); a.append(src + t * PACK_ROWS); b.append(_SRC['ggf'])
    assert COL_KPE == len(a) * PACK_ROWS and COL_GG == COL_KPE + 128
    small.append(1); a.append(_SRC['kpe']); b.append(_SRC['ggf'])
    assert len(a) == PACK_TILES
    return np.asarray([small, a, b], np.int32)


def _pack_kernel(plan_ref, a3_ref, b3_ref, o_ref):
    is_small = plan_ref[0, pl.program_id(1)] == 1
    a_ref, b_ref = a3_ref.at[0], b3_ref.at[0]

    @pl.when(jnp.logical_not(is_small))
    def _():
        o_ref[...] = a_ref[...].astype(BF16)

    @pl.when(is_small)
    def _():
        o_ref[...] = jnp.zeros_like(o_ref)
        o_ref[0:MLA_ROPE, :] = a_ref[0:MLA_ROPE, :].astype(BF16)
        o_ref[128:128 + 2 * GLA_GATE_RANK, :] = b_ref[0:2 * GLA_GATE_RANK, :].astype(BF16)


def _pack_w_in(w_in):
    w_t = jnp.swapaxes(w_in, 1, 2)
    window = lambda which: pl.BlockSpec((pl.Element(1), pl.Element(PACK_ROWS), pl.Element(D_MODEL)),
                                        lambda l, j, plan: (l, pl.multiple_of(plan[which, j], 32), 0))
    return pl.pallas_call(
        _pack_kernel,
        grid_spec=pltpu.PrefetchScalarGridSpec(
            num_scalar_prefetch=1, grid=(DEPTH, PACK_TILES),
            in_specs=[window(1), window(2)],
            out_specs=pl.BlockSpec((None, PACK_ROWS, D_MODEL), lambda l, j, plan: (l, j, 0))),
        out_shape=jax.ShapeDtypeStruct((DEPTH, PROJ_COLS, D_MODEL), BF16),
        compiler_params=pltpu.CompilerParams(
            dimension_semantics=("arbitrary", "arbitrary"),
            vmem_limit_bytes=_vmem_limit(2 * PACK_ROWS * D_MODEL * (4 + 4 + 2))),
        name="pack_w_in",
    )(jnp.asarray(_pack_plan()), w_t, w_t)


def _proj_kernel(x_ref, shift_ref, scale_ref, g_ref, w_ref, o_ref, h_ref, *, layer, base_row, rows_per_cond):
    i = pl.program_id(0)
    j = pl.program_id(1)
    row = _cond_row(i, x_ref.shape[0], base_row, rows_per_cond)

    @pl.when(j == 0)
    def _():
        _modulated_norm_to(h_ref, x_ref, g_ref[pl.ds(layer, 1), :],
                           shift_ref[pl.ds(row, 1), :], scale_ref[pl.ds(row, 1), :])

    o_ref[...] = lax.dot_general(h_ref[...], w_ref[...], NT_DIMS, preferred_element_type=F32)


def _in_proj(x, mods, layer, g_norm2, w_packed, base_row, rows_per_cond):
    m = x.shape[0]
    ncols = w_packed.shape[1]
    mod_spec = lambda k: pl.BlockSpec((None, None, 8, D_MODEL), lambda i, j: (layer, 3 + k, 0, 0))
    est = 2 * PROJ_TM * D_MODEL * 4 + PROJ_TM * D_MODEL * 2 + 2 * D_MODEL * PROJ_TN * 2 + 2 * PROJ_TM * PROJ_TN * 4
    return pl.pallas_call(
        functools.partial(_proj_kernel, layer=layer, base_row=base_row, rows_per_cond=rows_per_cond),
        grid=(m // PROJ_TM, ncols // PROJ_TN),
        in_specs=[
            pl.BlockSpec((PROJ_TM, D_MODEL), lambda i, j: (i, 0)),
            mod_spec(0), mod_spec(1),
            pl.BlockSpec((DEPTH, D_MODEL), lambda i, j: (0, 0)),
            pl.BlockSpec((None, PROJ_TN, D_MODEL), lambda i, j: (layer, j, 0)),
        ],
        out_specs=pl.BlockSpec((PROJ_TM, PROJ_TN), lambda i, j: (i, j)),
        out_shape=jax.ShapeDtypeStruct((m, ncols), F32),
        scratch_shapes=[pltpu.VMEM((PROJ_TM, D_MODEL), BF16)],
        compiler_params=pltpu.CompilerParams(
            dimension_semantics=("arbitrary", "arbitrary"), vmem_limit_bytes=_vmem_limit(est)),
        name="in_proj",
    )(x, mods, mods, g_norm2, w_packed)


def _rope_tables(t):
    pos = np.arange(t)
    inv_freq = (ROPE_BASE ** (-np.arange(16, dtype=np.float32) / 16)).astype(np.float32)

    def cs(p):
        ang = p.astype(np.float32)[:, None] * inv_freq[None, :]
        return np.concatenate([np.cos(ang)] * 2, axis=1), np.concatenate([np.sin(ang)] * 2, axis=1)

    cr, sr = cs(pos // GRID_W)
    cc, sc = cs(pos % GRID_W)
    sign = np.where(np.arange(64) % 32 < 16, -1.0, 1.0).astype(np.float32)
    return (np.concatenate([cr, cc], axis=1).astype(np.float32),
            (np.concatenate([sr, sc], axis=1) * sign[None, :]).astype(np.float32))


MLA_TR = 256
MLA_TQ = 256


def _pack_mla(w_uq_l, g_qn_l, g_kn_l):
    w = w_uq_l.reshape(MLA_Q_LORA, MLA_HEADS, MLA_QK)
    z64 = jnp.zeros((MLA_Q_LORA, MLA_HEADS, 64), w.dtype)
    wq = jnp.concatenate([w, z64], axis=-1).reshape(MLA_Q_LORA, -1).astype(BF16)
    v64 = jnp.zeros((64,), F32)
    gq_full = jnp.concatenate([g_qn_l, v64])[None]
    gk_n = g_kn_l[:MLA_NOPE][None]
    gk_r = jnp.concatenate([g_kn_l[MLA_NOPE:], v64])[None]
    return wq, gq_full, gk_n, gk_r


def _mla_kv_kernel(*refs, layer, normalize, rope, emit_ckv):
    it = iter(refs)
    ckv_ref, kpe_ref = next(it), next(it)
    if rope:
        c_ref, s_ref = next(it), next(it)
    w_ref, gkv_ref, gn_ref, gr_ref = next(it), next(it), next(it), next(it)
    k_ref, v_ref = next(it), next(it)
    ckv = ckv_ref[...]
    if normalize:
        ckv = (ckv * lax.rsqrt(jnp.mean(ckv * ckv, axis=-1, keepdims=True) + NORM_EPS)) * gkv_ref[pl.ds(layer, 1), :]
    if emit_ckv:
        next(it)[...] = ckv
    kv = jnp.dot(ckv.astype(BF16), w_ref[...].astype(BF16), preferred_element_type=F32)
    kpe = kpe_ref[...]
    ss_pe = jnp.sum(kpe * kpe, axis=-1, keepdims=True)
    kr = kpe * gr_ref[...]
    if rope:
        kr = kr * c_ref[...] + _rope_partner(kr) * s_ref[...]
    for h in range(MLA_HEADS):
        kn = kv[:, 256 * h:256 * h + 128]
        r = lax.rsqrt((jnp.sum(kn * kn, axis=-1, keepdims=True) + ss_pe) * (1.0 / MLA_QK) + NORM_EPS)
        k_ref[:, 256 * h:256 * h + 128] = ((kn * r) * gn_ref[...]).astype(BF16)
        k_ref[:, 256 * h + 128:256 * h + 256] = (kr * r).astype(BF16)
        v_ref[:, 128 * h:128 * h + 128] = kv[:, 256 * h + 128:256 * h + 256].astype(BF16)


def _mla_kv(ckv_src, ckv_spec, kpe_src, kpe_spec, rope_tabs, w_ukv, g_kv, gk_n, gk_r,
            batch, rows, layer, normalize, emit_ckv):
    rope = rope_tabs is not None
    nt = rows // MLA_TR
    ins = [ckv_src, kpe_src]
    specs = [ckv_spec, kpe_spec]
    if rope:
        ins += list(rope_tabs)
        specs += [pl.BlockSpec((MLA_TR, 128), lambda b, t: (t, 0))] * 2
    ins += [w_ukv, g_kv, gk_n, gk_r]
    specs += [pl.BlockSpec((None, MLA_KV_LORA, 2048), lambda b, t: (layer, 0, 0)),
              pl.BlockSpec((DEPTH, MLA_KV_LORA), lambda b, t: (0, 0))] + [pl.BlockSpec((1, 128), lambda b, t: (0, 0))] * 2
    out_shape = [jax.ShapeDtypeStruct((batch, rows, MLA_HEADS * MLA_HEAD_PAD), BF16),
                 jax.ShapeDtypeStruct((batch, rows, MLA_WIDTH), BF16)]
    out_specs = [pl.BlockSpec((None, MLA_TR, MLA_HEADS * MLA_HEAD_PAD), lambda b, t: (b, t, 0)),
                 pl.BlockSpec((None, MLA_TR, MLA_WIDTH), lambda b, t: (b, t, 0))]
    if emit_ckv:
        out_shape.append(jax.ShapeDtypeStruct((batch, rows, MLA_KV_LORA), F32))
        out_specs.append(pl.BlockSpec((None, MLA_TR, MLA_KV_LORA), lambda b, t: (b, t, 0)))
    est = 2 * MLA_KV_LORA * 2048 * 4 + 4 * MLA_TR * 2048 * 4
    return pl.pallas_call(
        functools.partial(_mla_kv_kernel, layer=layer, normalize=normalize, rope=rope, emit_ckv=emit_ckv),
        grid=(batch, nt), in_specs=specs, out_specs=out_specs, out_shape=out_shape,
        compiler_params=pltpu.CompilerParams(
            dimension_semantics=("arbitrary", "arbitrary"), vmem_limit_bytes=_vmem_limit(est)),
        name="mla_kv",
    )(*ins)


def _mla_attn_kernel(*refs, layer, rope, has_ctx):
    it = iter(refs)
    cq_ref, wq_ref, gq_ref, gfull_ref = next(it), next(it), next(it), next(it)
    if rope:
        c_ref, s_ref = next(it), next(it)
    k_ref, v_ref = next(it), next(it)
    if has_ctx:
        kc_ref, vc_ref = next(it), next(it)
    o_ref = next(it)
    cq = cq_ref[...]
    ql = ((cq * lax.rsqrt(jnp.mean(cq * cq, axis=-1, keepdims=True) + NORM_EPS)) * gq_ref[pl.ds(layer, 1), :]).astype(BF16)
    q_raw = jnp.dot(ql, wq_ref[...], preferred_element_type=F32)
    for h in range(MLA_HEADS):
        sl = slice(256 * h, 256 * h + 256)
        vs = slice(128 * h, 128 * h + 128)
        q = q_raw[:, sl]
        r = lax.rsqrt(jnp.sum(q * q, axis=-1, keepdims=True) * (1.0 / MLA_QK) + NORM_EPS)
        qh = q * gfull_ref[...]
        if rope:
            q_rope = qh[:, 128:]
            qh = jnp.concatenate([qh[:, :128], q_rope * c_ref[...] + _rope_partner(q_rope) * s_ref[...]], axis=1)
        qh = ((qh * r) * (MLA_SCALE * LOG2_E)).astype(BF16)
        s = lax.dot_general(qh, k_ref[:, sl], NT_DIMS, preferred_element_type=F32)
        m = jnp.max(s, axis=-1, keepdims=True)
        if has_ctx:
            sc = lax.dot_general(qh, kc_ref[:, sl], NT_DIMS, preferred_element_type=F32)
            m = jnp.maximum(m, jnp.max(sc, axis=-1, keepdims=True))
        e = jnp.exp2(s - m)
        den = jnp.sum(e, axis=-1, keepdims=True)
        o = jnp.dot(e.astype(BF16), v_ref[:, vs], preferred_element_type=F32)
        if has_ctx:
            ec = jnp.exp2(sc - m)
            den = den + jnp.sum(ec, axis=-1, keepdims=True)
            o = o + jnp.dot(ec.astype(BF16), vc_ref[:, vs], preferred_element_type=F32)
        o_ref[:, vs] = (o * (1.0 / den)).astype(BF16)


def _mla_attn(proj, wq, g_q, gq_full, rope_tabs, k, v, ctx_kv, batch, t, layer):
    rope = rope_tabs is not None
    has_ctx = ctx_kv is not None
    nq = t // MLA_TQ
    const = lambda shape: pl.BlockSpec(shape, lambda b, i: (0,) * len(shape))
    ins = [proj, wq, g_q, gq_full]
    specs = [pl.BlockSpec((MLA_TQ, MLA_Q_LORA), lambda b, i: (b * nq + i, COL_CQ // MLA_Q_LORA)),
             const((MLA_Q_LORA, 2048)), const((DEPTH, MLA_Q_LORA)), const((1, 256))]
    if rope:
        ins += list(rope_tabs)
        specs += [pl.BlockSpec((MLA_TQ, 128), lambda b, i: (i, 0))] * 2
    ins += [k, v]
    specs += [pl.BlockSpec((None, t, 2048), lambda b, i: (b, 0, 0)), pl.BlockSpec((None, t, MLA_WIDTH), lambda b, i: (b, 0, 0))]
    est = 2 * (t * 2048 * 2 + t * MLA_WIDTH * 2) + 6 * MLA_TQ * t * 4
    if has_ctx:
        kc, vc = ctx_kv
        lc = kc.shape[1]
        ins += [kc, vc]
        specs += [pl.BlockSpec((None, lc, 2048), lambda b, i: (b, 0, 0)), pl.BlockSpec((None, lc, MLA_WIDTH), lambda b, i: (b, 0, 0))]
        est += 2 * lc * 3072 * 2
    return pl.pallas_call(
        functools.partial(_mla_attn_kernel, layer=layer, rope=rope, has_ctx=has_ctx),
        grid=(batch, nq), in_specs=specs,
        out_specs=pl.BlockSpec((MLA_TQ, MLA_WIDTH), lambda b, i: (b * nq + i, 0)),
        out_shape=jax.ShapeDtypeStruct((batch * t, MLA_WIDTH), BF16),
        compiler_params=pltpu.CompilerParams(
            dimension_semantics=("arbitrary", "arbitrary"), vmem_limit_bytes=_vmem_limit(est)),
        name="mla_attn",
    )(*ins)


def _mla_branch(proj, batch, t, layer, mla_w, w_ukv, g_q, g_kv, rope, ctx):
    wq, gq_full, gk_n, gk_r = mla_w
    nt = t // MLA_TR
    ckv_spec = pl.BlockSpec((MLA_TR, MLA_KV_LORA), lambda b, i: (b * nt + i, COL_CKV // MLA_KV_LORA))
    kpe_spec = pl.BlockSpec((MLA_TR, 128), lambda b, i: (b * nt + i, COL_KPE // 128))
    if rope is None:
        k, v, ckv_n = _mla_kv(proj, ckv_spec, proj, kpe_spec, None, w_ukv, g_kv, gk_n, gk_r,
                              batch, t, layer, normalize=True, emit_ckv=True)
        return _mla_attn(proj, wq, g_q, gq_full, None, k, v, None, batch, t, layer), ckv_n
    c64, s64 = rope
    z64 = np.zeros_like(c64)
    tabs = (jnp.asarray(np.concatenate([c64, z64], axis=1)),
            jnp.asarray(np.concatenate([s64, z64], axis=1)))
    k, v = _mla_kv(proj, ckv_spec, proj, kpe_spec, tabs, w_ukv, g_kv, gk_n, gk_r,
                   batch, t, layer, normalize=True, emit_ckv=False)
    ctx_ckv, ctx_kpe = ctx
    lc = ctx_ckv.shape[2]
    kc, vc = _mla_kv(ctx_ckv, pl.BlockSpec((None, None, MLA_TR, MLA_KV_LORA), lambda b, i: (b, layer, i, 0)),
                     ctx_kpe, pl.BlockSpec((None, None, MLA_TR, 128), lambda b, i: (b, layer, i, 0)),
                     None, w_ukv, g_kv, gk_n, gk_r, batch, lc, layer, normalize=False, emit_ckv=False)
    return _mla_attn(proj, wq, g_q, gq_full, tabs, k, v, (kc, vc), batch, t, layer), None


SWA_TR = 128
SWA_NEG = -1e30


def _pair_sum_matrix():
    g = (np.arange(128)[:, None] // 64 == np.arange(128)[None, :] // 64).astype(np.float32)
    return jnp.asarray(np.concatenate([g, g], axis=0), BF16)


def _group_rms_scale(x, pair_ref):
    sq = x * x
    hi = sq.astype(BF16)
    lo = (sq - hi.astype(F32)).astype(BF16)
    ss = jnp.dot(jnp.concatenate([hi, lo], axis=1), pair_ref[...], preferred_element_type=F32)
    return lax.rsqrt(ss * (1.0 / SWA_HEAD_DIM) + NORM_EPS)


def _swa_kv_kernel(*refs, normalize, rope, emit, pad_blocks, layer):
    it = iter(refs)
    k_ref, v_ref = next(it), next(it)
    if rope:
        c_ref, s_ref = next(it), next(it)
    if normalize:
        g_ref, pair_ref = next(it), next(it)
    ko_ref, vo_ref = next(it), next(it)
    if emit:
        kn_ref, vn_ref = next(it), next(it)
    lane = lax.broadcasted_iota(jnp.int32, (k_ref.shape[0], 128), 1)
    low = lane < 64

    def halves(x, c, o_ref):
        sw = pltpu.roll(x, 64, axis=1)
        zero = jnp.zeros_like(x)
        o_ref[4 * c + 0] = jnp.where(low, x, zero).astype(BF16)
        o_ref[4 * c + 1] = jnp.where(low, zero, sw).astype(BF16)
        o_ref[4 * c + 2] = jnp.where(low, sw, zero).astype(BF16)
        o_ref[4 * c + 3] = jnp.where(low, zero, x).astype(BF16)

    def compute():
        for c in range(2):
            sl = slice(128 * c, 128 * c + 128)
            kb = k_ref[:, sl]
            if normalize:
                r = _group_rms_scale(kb, pair_ref)
                kn = (kb * r) * g_ref[...]
                if emit:
                    kn_ref[:, sl] = kn
                if rope:
                    kn = kn * c_ref[...] + _rope_partner(kn) * s_ref[...]
            else:
                kn = kb
            halves(kn, c, ko_ref)
            vb = v_ref[:, sl]
            if emit:
                vn_ref[:, sl] = vb
            halves(vb, c, vo_ref)

    if pad_blocks:
        t = pl.program_id(1)
        is_pad = jnp.logical_or(t == 0, t == pl.num_programs(1) - 1)

        @pl.when(is_pad)
        def _():
            ko_ref[...] = jnp.zeros_like(ko_ref)
            vo_ref[...] = jnp.zeros_like(vo_ref)

        pl.when(jnp.logical_not(is_pad))(compute)
    else:
        compute()


def _swa_kv(k_src, k_spec, v_src, v_spec, rope_tabs, norm_ins, batch, rows, tr, layer, emit, pad_blocks):
    rope = rope_tabs is not None
    normalize = norm_ins is not None
    nt = rows // tr + (2 if pad_blocks else 0)
    ins, specs = [k_src, v_src], [k_spec, v_spec]
    if rope:
        tab_row = (lambda t: jnp.clip(t - 1, 0, nt - 3)) if pad_blocks else (lambda t: t)
        ins += list(rope_tabs)
        specs += [pl.BlockSpec((tr, 128), lambda b, t: (tab_row(t), 0))] * 2
    if normalize:
        ins += list(norm_ins)
        specs += [pl.BlockSpec((1, 128), lambda b, t: (0, 0)), pl.BlockSpec((256, 128), lambda b, t: (0, 0))]
    out_shape = [jax.ShapeDtypeStruct((batch, 8, nt * tr, 128), BF16)] * 2
    out_specs = [pl.BlockSpec((None, 8, tr, 128), lambda b, t: (b, 0, t, 0))] * 2
    if emit:
        out_shape += [jax.ShapeDtypeStruct((batch, rows, 256), F32)] * 2
        out_specs += [pl.BlockSpec((None, tr, 256), lambda b, t: (b, t, 0))] * 2
    return pl.pallas_call(
        functools.partial(_swa_kv_kernel, normalize=normalize, rope=rope, emit=emit, pad_blocks=pad_blocks, layer=layer),
        grid=(batch, nt), in_specs=specs, out_specs=out_specs, out_shape=out_shape,
        compiler_params=pltpu.CompilerParams(dimension_semantics=("arbitrary", "arbitrary")),
        name="swa_kv",
    )(*ins)


def _swa_attn_kernel(*refs, rope, windowed, has_ctx, t_total, layer):
    it = iter(refs)
    sink_ref, q_ref = next(it), next(it)
    if rope:
        c_ref, s_ref = next(it), next(it)
    g_ref, pair_ref = next(it), next(it)
    k_ref, v_ref = next(it), next(it)
    if has_ctx:
        kc_ref, vc_ref = next(it), next(it)
    o_ref = next(it)
    n = pl.program_id(1)
    if windowed:
        start = pl.multiple_of(n * SWA_TR, SWA_TR)
        win = pl.ds(start, 3 * SWA_TR)
        r_i = lax.broadcasted_iota(jnp.int32, (2 * SWA_TR, 3 * SWA_TR), 0) & (SWA_TR - 1)
        c_i = lax.broadcasted_iota(jnp.int32, (2 * SWA_TR, 3 * SWA_TR), 1)
        kpos = (n - 1) * SWA_TR + c_i
        diff = SWA_TR + r_i - c_i
        valid2 = (kpos >= 0) & (kpos < t_total) & (diff <= SWA_WINDOW) & (diff >= -SWA_WINDOW)
    else:
        win = slice(None)
    def normed_queries(cb):
        sl = slice(128 * cb, 128 * cb + 128)
        qb = q_ref[:, sl]
        r = _group_rms_scale(qb, pair_ref)
        qn = (qb * r) * g_ref[...]
        if rope:
            qn = qn * c_ref[...] + _rope_partner(qn) * s_ref[...]
        return (qn * (SWA_SCALE * LOG2_E)).astype(BF16)

    tq = q_ref.shape[0]
    low_lanes = lax.broadcasted_iota(jnp.int32, (2 * tq, 128), 1) < SWA_HEAD_DIM

    lk = 3 * SWA_TR if windowed else k_ref.shape[1]
    lc = kc_ref.shape[1] if has_ctx else 0
    row_blk = lax.broadcasted_iota(jnp.int32, (2 * tq, 1), 0) // tq
    for j in range(SWA_KV_HEADS):
        q2 = jnp.concatenate([normed_queries(2 * j), normed_queries(2 * j + 1)], axis=0)
        k_parts, v_parts = [], []
        for e in range(2):
            k_parts.append(k_ref[2 * j + e, win, :])
            v_parts.append(v_ref[2 * j + e, win, :])
            if has_ctx:
                k_parts.append(kc_ref[2 * j + e])
                v_parts.append(vc_ref[2 * j + e])
        s = lax.dot_general(q2, jnp.concatenate(k_parts, axis=0), NT_DIMS, preferred_element_type=F32)
        p_parts, inv = [], []
        for e in range(2):
            off = e * (lk + lc)
            sink = jnp.where(row_blk == 0, sink_ref[layer, 4 * j + e], sink_ref[layer, 4 * j + 2 + e]) * LOG2_E
            s_loc = s[:, off:off + lk]
            if windowed:
                s_loc = jnp.where(valid2, s_loc, SWA_NEG)
            m = jnp.maximum(jnp.max(s_loc, axis=-1, keepdims=True), sink)
            if has_ctx:
                s_ctx = s[:, off + lk:off + lk + lc]
                m = jnp.maximum(m, jnp.max(s_ctx, axis=-1, keepdims=True))
            p = jnp.exp2(s_loc - m)
            den = jnp.sum(p, axis=-1, keepdims=True) + jnp.exp2(sink - m)
            p_parts.append(p.astype(BF16))
            if has_ctx:
                pc = jnp.exp2(s_ctx - m)
                den = den + jnp.sum(pc, axis=-1, keepdims=True)
                p_parts.append(pc.astype(BF16))
            inv.append(1.0 / den)
        o = jnp.dot(jnp.concatenate(p_parts, axis=1), jnp.concatenate(v_parts, axis=0), preferred_element_type=F32)
        o = o * jnp.where(low_lanes, inv[0], inv[1])
        o_ref[:, 256 * j:256 * j + 128] = o[:tq].astype(BF16)
        o_ref[:, 256 * j + 128:256 * j + 256] = o[tq:].astype(BF16)


def _swa_attn(sink, proj, rope_tabs, norm_ins, k, v, ctx_kv, batch, t, layer):
    rope = rope_tabs is not None
    has_ctx = ctx_kv is not None
    tq = SWA_TR if has_ctx else 2 * SWA_TR
    nq = t // tq
    ins = [sink, proj]
    specs = [pl.BlockSpec(memory_space=pltpu.SMEM),
             pl.BlockSpec((tq, SWA_WIDTH), lambda b, i: (b * nq + i, COL_SQ // SWA_WIDTH))]
    if rope:
        ins += list(rope_tabs)
        specs += [pl.BlockSpec((tq, 128), lambda b, i: (i, 0))] * 2
    ins += list(norm_ins)
    specs += [pl.BlockSpec((1, 128), lambda b, i: (0, 0)), pl.BlockSpec((256, 128), lambda b, i: (0, 0))]
    lk = k.shape[2]
    ins += [k, v]
    specs += [pl.BlockSpec((None, 8, lk, 128), lambda b, i: (b, 0, 0, 0))] * 2
    if has_ctx:
        lc = ctx_kv[0].shape[2]
        ins += list(ctx_kv)
        specs += [pl.BlockSpec((None, 8, lc, 128), lambda b, i: (b, 0, 0, 0))] * 2
    return pl.pallas_call(
        functools.partial(_swa_attn_kernel, rope=rope, windowed=has_ctx, has_ctx=has_ctx, t_total=t, layer=layer),
        grid=(batch, nq), in_specs=specs,
        out_specs=pl.BlockSpec((tq, SWA_WIDTH), lambda b, i: (b * nq + i, 0)),
        out_shape=jax.ShapeDtypeStruct((batch * t, SWA_WIDTH), BF16),
        compiler_params=pltpu.CompilerParams(dimension_semantics=("arbitrary", "arbitrary")),
        name="swa_attn",
    )(*ins)


def _swa_branch(proj, batch, t, layer, g_qn_l, g_kn_l, sink, rope, ctx):
    pair = _pair_sum_matrix()
    tile2 = lambda g: jnp.concatenate([g, g])[None]
    wide = 2 * SWA_TR
    if rope is None:
        nw = t // wide
        k_spec = pl.BlockSpec((wide, 256), lambda b, i: (b * nw + i, COL_SK // 256))
        v_spec = pl.BlockSpec((wide, 256), lambda b, i: (b * nw + i, COL_SV // 256))
        k, v, k_n, v_raw = _swa_kv(proj, k_spec, proj, v_spec, None, (tile2(g_kn_l), pair),
                                   batch, t, wide, layer, emit=True, pad_blocks=False)
        o = _swa_attn(sink, proj, None, (tile2(g_qn_l), pair), k, v, None, batch, t, layer)
        return o, k_n, v_raw
    nt = t // SWA_TR
    c64, s64 = rope
    tabs = (jnp.asarray(np.concatenate([c64, c64], axis=1)),
            jnp.asarray(np.concatenate([s64, s64], axis=1)))
    row = lambda b, i: b * nt + jnp.clip(i - 1, 0, nt - 1)
    k_spec = pl.BlockSpec((SWA_TR, 256), lambda b, i: (row(b, i), COL_SK // 256))
    v_spec = pl.BlockSpec((SWA_TR, 256), lambda b, i: (row(b, i), COL_SV // 256))
    k, v = _swa_kv(proj, k_spec, proj, v_spec, tabs, (tile2(g_kn_l), pair), batch, t, SWA_TR, layer,
                   emit=False, pad_blocks=True)
    ctx_k, ctx_v = ctx
    lc = ctx_k.shape[2]
    c_spec = pl.BlockSpec((None, None, wide, 256), lambda b, i: (b, layer, i, 0))
    kc, vc = _swa_kv(ctx_k, c_spec, ctx_v, c_spec, None, None, batch, lc, wide, layer, emit=False, pad_blocks=False)
    o = _swa_attn(sink, proj, tabs, (tile2(g_qn_l), pair), k, v, (kc, vc), batch, t, layer)
    return o, None, None


GLA_LEVELS = (64, 32, 16, 8, 4, 2, 1)
GLA_MXU_LEVELS = (2, 1)
GLA_ROWSETS = 1 + len(GLA_MXU_LEVELS)


def _gla_sum_matrix(backward):
    c = GLA_CHUNK
    t = np.arange(c)[:, None]
    j = np.arange(c)[None, :]
    sets = [j <= t]
    for g in GLA_MXU_LEVELS:
        e = (t // (2 * g)) * 2 * g + g - 1
        upper = (t // g) % 2 == 1
        sets.append(np.where(upper, (j > e) & (j <= t), (j > t) & (j <= e)))
    n = np.concatenate(sets, axis=0).astype(np.float32)
    if backward:
        n = n.reshape(GLA_ROWSETS, c, c)[:, ::-1, ::-1].reshape(GLA_ROWSETS * c, c)
    return jnp.asarray(np.concatenate([n, n, n], axis=1), BF16)


def _gla_kernel(*refs, has_s0):
    it = iter(refs)
    srcs = [tuple(next(it) for _ in range(4)) for _ in range(2)]
    nmat = (next(it), next(it))
    wg = (next(it), next(it))
    bg = (next(it), next(it))
    s0_ref = next(it) if has_s0 else None
    o_refs = (next(it), next(it))
    sfin_ref = next(it)
    s_ref = next(it)
    i = pl.program_id(1)
    c = GLA_CHUNK

    @pl.when(i == 0)
    def _():
        s_ref[...] = s0_ref[...] if has_s0 else jnp.zeros_like(s_ref)

    row = lax.broadcasted_iota(jnp.int32, (c, c), 0)
    col = lax.broadcasted_iota(jnp.int32, (c, c), 1)
    diag = row == col
    log2 = lambda g: int(g).bit_length() - 1
    second_half = [((row >> log2(g)) & 1) == 1 for g in GLA_LEVELS]
    same_block = [(row >> (log2(g) + 1)) == (col >> (log2(g) + 1)) for g in GLA_LEVELS]
    for d in range(2):
        q_ref, k_ref, v_ref, gg_ref = srcs[d]
        z = jnp.dot(gg_ref[...].astype(BF16), wg[d][...], preferred_element_type=F32) + bg[d][...]
        la = (jnp.minimum(z, 0.0) - jnp.log1p(jnp.exp(-jnp.abs(z)))) * (1.0 / GLA_GATE_NORM)
        hi = la.astype(BF16)
        r1 = la - hi.astype(F32)
        mid = r1.astype(BF16)
        lo = (r1 - mid.astype(F32)).astype(BF16)
        ex_all = jnp.dot(nmat[d][...], jnp.concatenate([hi, mid, lo], axis=0), preferred_element_type=F32)
        for h in range(GLA_HEADS):
            sl = slice(GLA_DK * h, GLA_DK * h + GLA_DK)
            vs = slice(GLA_DV * h, GLA_DV * h + GLA_DV)
            ex = ex_all[:, sl]
            q = q_ref[:, sl] * (GLA_DK ** -0.5)
            k = k_ref[:, sl]
            v = v_ref[:, vs].astype(BF16)
            b = ex[0:c]
            eb = jnp.exp(b)
            b_exit = b[c - 1:c] if d == 0 else b[0:1]
            ek = jnp.exp(b_exit - b)
            s_old = s_ref[d, h]
            o = jnp.dot((q * eb).astype(BF16), s_old.astype(BF16), preferred_element_type=F32)
            a = jnp.where(diag, jnp.sum(q * k, axis=-1, keepdims=True), 0.0)
            for li, g in enumerate(GLA_LEVELS):
                if g in GLA_MXU_LEVELS:
                    r = 1 + GLA_MXU_LEVELS.index(g)
                    eg = jnp.exp(ex[r * c:(r + 1) * c])
                else:
                    b3 = b.reshape(c // (2 * g), 2 * g, GLA_DK)
                    edge = g - 1 if d == 0 else g
                    b_edge = jnp.broadcast_to(b3[:, edge:edge + 1, :], b3.shape).reshape(c, GLA_DK)
                    eg = jnp.exp(-jnp.abs(b - b_edge))
                qe, ke = q * eg, k * eg
                late = second_half[li]
                qg = (jnp.where(late, qe, 0.0) if d == 0 else jnp.where(late, 0.0, qe)).astype(BF16)
                kg = (jnp.where(late, 0.0, ke) if d == 0 else jnp.where(late, ke, 0.0)).astype(BF16)
                ag = lax.dot_general(qg, kg, NT_DIMS, preferred_element_type=F32)
                if 2 * g < c:
                    ag = jnp.where(same_block[li], ag, 0.0)
                a = a + ag
            o = o + jnp.dot(a.astype(BF16), v, preferred_element_type=F32)
            o_refs[d][:, vs] = o
            a_col = eb.T[:, c - 1:c] if d == 0 else eb.T[:, 0:1]
            kt = (k * ek).T.astype(BF16)
            s_ref[d, h] = a_col * s_old + jnp.dot(kt, v, preferred_element_type=F32)

    @pl.when(i == pl.num_programs(1) - 1)
    def _():
        sfin_ref[...] = s_ref[...]


def _gla(proj, batch, t, layer, w_gf, b_gf, w_gb, b_gb, s0):
    c = GLA_CHUNK
    n = t // c
    fwd = lambda b, i: b * n + i
    bwd = lambda b, i: b * n + (n - 1 - i)
    ins, specs = [], []
    for rowf in (fwd, bwd):
        ins += [proj] * 4
        specs += [pl.BlockSpec((c, 512), lambda b, i, rowf=rowf: (rowf(b, i), COL_GQ // 512)),
                  pl.BlockSpec((c, 512), lambda b, i, rowf=rowf: (rowf(b, i), COL_GK // 512)),
                  pl.BlockSpec((c, GLA_WIDTH), lambda b, i, rowf=rowf: (rowf(b, i), COL_GV // GLA_WIDTH)),
                  pl.BlockSpec((c, 128), lambda b, i, rowf=rowf: (rowf(b, i), COL_GG // 128))]
    const = lambda shape: pl.BlockSpec(shape, lambda b, i: (0,) * len(shape))
    pad_w = lambda w, off: jnp.zeros((128, 512), F32).at[off:off + GLA_GATE_RANK].set(w).astype(BF16)
    ins += [_gla_sum_matrix(False), _gla_sum_matrix(True), pad_w(w_gf, 0), pad_w(w_gb, GLA_GATE_RANK), b_gf[None], b_gb[None]]
    specs += [const((GLA_ROWSETS * c, 3 * c))] * 2 + [const((128, 512))] * 2 + [const((1, 512))] * 2
    state_spec = pl.BlockSpec((None, 2, GLA_HEADS, GLA_DK, GLA_DV), lambda b, i: (b, 0, 0, 0, 0))
    if s0 is not None:
        ins.append(s0)
        specs.append(pl.BlockSpec((None, None, 2, GLA_HEADS, GLA_DK, GLA_DV), lambda b, i: (b, layer, 0, 0, 0, 0)))
    return pl.pallas_call(
        functools.partial(_gla_kernel, has_s0=s0 is not None),
        grid=(batch, n), in_specs=specs,
        out_specs=[pl.BlockSpec((c, GLA_WIDTH), lambda b, i: (fwd(b, i), 0)),
                   pl.BlockSpec((c, GLA_WIDTH), lambda b, i: (bwd(b, i), 0)), state_spec],
        out_shape=[jax.ShapeDtypeStruct((batch * t, GLA_WIDTH), F32)] * 2
        + [jax.ShapeDtypeStruct((batch, 2, GLA_HEADS, GLA_DK, GLA_DV), F32)],
        scratch_shapes=[pltpu.VMEM((2, GLA_HEADS, GLA_DK, GLA_DV), F32)],
        compiler_params=pltpu.CompilerParams(dimension_semantics=("arbitrary", "arbitrary")),
        name="gla",
    )(*ins)


GOUT_TM = 512


def _gla_out_kernel(of_ref, ob_ref, gate_ref, g_ref, o_ref, *, layer):
    g = g_ref[pl.ds(layer, 1), :]
    for h in range(GLA_HEADS):
        vs = slice(GLA_DV * h, GLA_DV * h + GLA_DV)
        o = of_ref[:, vs] + ob_ref[:, vs]
        y = (o * lax.rsqrt(jnp.mean(o * o, axis=-1, keepdims=True) + NORM_EPS)) * g
        gate = gate_ref[:, vs]
        o_ref[:, vs] = (y * (gate * jax.nn.sigmoid(gate))).astype(BF16)


def _gla_out(o_f, o_b, proj, g_gla_out, layer):
    m = o_f.shape[0]
    return pl.pallas_call(
        functools.partial(_gla_out_kernel, layer=layer),
        grid=(m // GOUT_TM,),
        in_specs=[pl.BlockSpec((GOUT_TM, GLA_WIDTH), lambda i: (i, 0)),
                  pl.BlockSpec((GOUT_TM, GLA_WIDTH), lambda i: (i, 0)),
                  pl.BlockSpec((GOUT_TM, GLA_WIDTH), lambda i: (i, COL_GOUT // GLA_WIDTH)),
                  pl.BlockSpec((DEPTH, GLA_DV), lambda i: (0, 0))],
        out_specs=pl.BlockSpec((GOUT_TM, GLA_WIDTH), lambda i: (i, 0)),
        out_shape=jax.ShapeDtypeStruct((m, GLA_WIDTH), BF16),
        compiler_params=pltpu.CompilerParams(dimension_semantics=("arbitrary",)),
        name="gla_out",
    )(o_f, o_b, proj, g_gla_out)


MERGE_TM = 1024
MERGE_TN = 512


def _merge_kernel(om_ref, og_ref, os_ref, gm_ref, gg_ref, gs_ref, wm_ref, wg_ref, ws_ref, o_ref):
    def branch(o_r, gate_r, w_r):
        return jax.nn.sigmoid(gate_r[...]) * jnp.dot(o_r[...], w_r[...], preferred_element_type=F32)

    merged = branch(om_ref, gm_ref, wm_ref) + branch(og_ref, gg_ref, wg_ref) + branch(os_ref, gs_ref, ws_ref)
    o_ref[...] = merged.astype(BF16)


def _merge(o_mla, o_gla, o_swa, proj, w_br_mla, w_br_gla, w_br_swa, layer):
    m = o_mla.shape[0]
    nn = D_MODEL // MERGE_TN
    o_spec = pl.BlockSpec((MERGE_TM, 1024), lambda i, j: (i, 0))
    gate_spec = lambda col: pl.BlockSpec((MERGE_TM, MERGE_TN), lambda i, j: (i, col // MERGE_TN + j))
    w_spec = pl.BlockSpec((None, 1024, MERGE_TN), lambda i, j: (layer, 0, j))
    est = 2 * (3 * MERGE_TM * 1024 * 2 + 3 * MERGE_TM * MERGE_TN * 4 + 3 * 1024 * MERGE_TN * 2 + MERGE_TM * MERGE_TN * 2)
    return pl.pallas_call(
        _merge_kernel,
        grid=(m // MERGE_TM, nn),
        in_specs=[o_spec, o_spec, o_spec, gate_spec(COL_GATE_MLA), gate_spec(COL_GATE_GLA), gate_spec(COL_GATE_SWA),
                  w_spec, w_spec, w_spec],
        out_specs=pl.BlockSpec((MERGE_TM, MERGE_TN), lambda i, j: (i, j)),
        out_shape=jax.ShapeDtypeStruct((m, D_MODEL), BF16),
        compiler_params=pltpu.CompilerParams(
            dimension_semantics=("arbitrary", "arbitrary"), vmem_limit_bytes=_vmem_limit(est)),
        name="merge",
    )(o_mla, o_gla, o_swa, proj, proj, proj, w_br_mla, w_br_gla, w_br_swa)


OUT_TM = 1024
OUT_TN = 512


def _out_proj_kernel(m_ref, w_ref, x_ref, gate_ref, o_ref, *, base_row, rows_per_cond):
    row = _cond_row(pl.program_id(0), m_ref.shape[0], base_row, rows_per_cond)
    y = jnp.dot(m_ref[...], w_ref[...], preferred_element_type=F32)
    o_ref[...] = x_ref[...] + gate_ref[pl.ds(row, 1), :] * y


def _out_proj(merged, x, mods, w_out, layer, base_row, rows_per_cond):
    m = x.shape[0]
    est = 2 * (OUT_TM * D_MODEL * 2 + D_MODEL * OUT_TN * 2 + 2 * OUT_TM * OUT_TN * 4)
    return pl.pallas_call(
        functools.partial(_out_proj_kernel, base_row=base_row, rows_per_cond=rows_per_cond),
        grid=(m // OUT_TM, D_MODEL // OUT_TN),
        in_specs=[pl.BlockSpec((OUT_TM, D_MODEL), lambda i, j: (i, 0)),
                  pl.BlockSpec((None, D_MODEL, OUT_TN), lambda i, j: (layer, 0, j)),
                  pl.BlockSpec((OUT_TM, OUT_TN), lambda i, j: (i, j)),
                  pl.BlockSpec((None, None, 8, OUT_TN), lambda i, j: (layer, 5, 0, j))],
        out_specs=pl.BlockSpec((OUT_TM, OUT_TN), lambda i, j: (i, j)),
        out_shape=jax.ShapeDtypeStruct((m, D_MODEL), F32),
        compiler_params=pltpu.CompilerParams(
            dimension_semantics=("arbitrary", "arbitrary"), vmem_limit_bytes=_vmem_limit(est)),
        name="out_proj",
    )(merged, w_out, x, mods)


def _trunk_layer(x, mods, layer, w, group):
    batch, t, base_row, rows_per_cond, rope, ctx = group
    x = _ffn(x, mods, layer, 0, w['g_norm1'], w['w_ff1_gu'], w['w_ff1_down'], base_row, rows_per_cond)
    proj = _in_proj(x, mods, layer, w['g_norm2'], w['w_in_packed'], base_row, rows_per_cond)
    mla_ctx = swa_ctx = s0 = None
    if ctx is not None:
        mla_ctx, swa_ctx, s0 = ctx[:2], ctx[2:4], ctx[4]
    o_mla, ckv_n = _mla_branch(proj, batch, t, layer, w['mla_packed'][layer], w['w_mla_ukv'], w['g_mla_q'], w['g_mla_kv'],
                               rope, mla_ctx)
    o_swa, k_n, v_raw = _swa_branch(proj, batch, t, layer, w['g_swa_qn'][layer], w['g_swa_kn'][layer], w['swa_sink'],
                                    rope, swa_ctx)
    o_f, o_b, s_fin = _gla(proj, batch, t, layer, w['w_gla_gf'][layer], w['b_gla_gf'][layer],
                           w['w_gla_gb'][layer], w['b_gla_gb'][layer], s0)
    o_gla = _gla_out(o_f, o_b, proj, w['g_gla_out'], layer)
    merged = _merge(o_mla, o_gla, o_swa, proj, w['w_br_mla'], w['w_br_gla'], w['w_br_swa'], layer)
    x = _out_proj(merged, x, mods, w['w_out'], layer, base_row, rows_per_cond)
    x = _ffn(x, mods, layer, 6, w['g_norm3'], w['w_ff2_gu'], w['w_ff2_down'], base_row, rows_per_cond)
    new_ctx = None
    if ctx is None:
        kpe = proj[:, COL_KPE:COL_KPE + MLA_ROPE].reshape(batch, t, MLA_ROPE)
        new_ctx = (ckv_n, kpe, k_n.reshape(batch, t, SWA_KV_HEADS, SWA_HEAD_DIM),
                   v_raw.reshape(batch, t, SWA_KV_HEADS, SWA_HEAD_DIM), s_fin)
    return x, new_ctx


def kernel(x_prompt, x_sample, cache_mla_ckv, cache_mla_kpe, cache_swa_k, cache_swa_v, state_gla,
           c, c_ctx, w_ada, b_ada, g_norm1, g_norm2, g_norm3,
           w_ff1_gu, w_ff1_down, w_ff2_gu, w_ff2_down, w_in,
           g_mla_q, w_mla_uq, g_mla_kv, w_mla_ukv, g_mla_qn, g_mla_kn,
           w_gla_gf, b_gla_gf, w_gla_gb, b_gla_gb, g_gla_out,
           g_swa_qn, g_swa_kn, swa_sink, w_br_mla, w_br_gla, w_br_swa, w_out):
    bp, tp, _ = x_prompt.shape
    bs, ts, _ = x_sample.shape
    assert bs + 1 <= 8, "conditioning rows are packed into one 8-row tile"
    cond8 = jnp.zeros((8, D_MODEL), F32).at[0].set(c_ctx).at[1:1 + bs].set(c)
    mods = _adaln(cond8, w_ada, b_ada)
    w = dict(g_norm1=g_norm1, g_norm2=g_norm2, g_norm3=g_norm3,
             w_ff1_gu=w_ff1_gu, w_ff1_down=w_ff1_down, w_ff2_gu=w_ff2_gu, w_ff2_down=w_ff2_down,
             w_in_packed=_pack_w_in(w_in),
             mla_packed=[_pack_mla(w_mla_uq[l], g_mla_qn[l], g_mla_kn[l]) for l in range(DEPTH)],
             w_mla_ukv=w_mla_ukv, g_mla_q=g_mla_q, g_mla_kv=g_mla_kv,
             g_swa_qn=g_swa_qn, g_swa_kn=g_swa_kn, swa_sink=swa_sink,
             w_gla_gf=w_gla_gf, b_gla_gf=b_gla_gf, w_gla_gb=w_gla_gb, b_gla_gb=b_gla_gb, g_gla_out=g_gla_out,
             w_br_mla=w_br_mla.astype(BF16), w_br_gla=w_br_gla.astype(BF16), w_br_swa=w_br_swa.astype(BF16),
             w_out=w_out.astype(BF16))
    past = cache_mla_ckv.shape[2]
    ctx = (cache_mla_ckv, jnp.pad(cache_mla_kpe, ((0, 0), (0, 0), (0, 0), (0, 128 - MLA_ROPE))),
           cache_swa_k.reshape(bs, DEPTH, past, SWA_KV_HEADS * SWA_HEAD_DIM),
           cache_swa_v.reshape(bs, DEPTH, past, SWA_KV_HEADS * SWA_HEAD_DIM), state_gla)
    group_p = (bp, tp, 0, bp * tp, None, None)
    group_s = (bs, ts, 1, ts, _rope_tables(ts), ctx)
    y_p = x_prompt.reshape(bp * tp, D_MODEL)
    y_s = x_sample.reshape(bs * ts, D_MODEL)
    new = []
    for l in range(DEPTH):
        y_p, new_ctx = _trunk_layer(y_p, mods, l, w, group_p)
        new.append(new_ctx)
        y_s, _ = _trunk_layer(y_s, mods, l, w, group_s)
    stacked = tuple(jnp.stack([new[l][k] for l in range(DEPTH)], axis=1) for k in range(5))
    return (y_p.reshape(bp, tp, D_MODEL), y_s.reshape(bs, ts, D_MODEL)) + stacked
```

```python
import functools

import numpy as np
import jax
import jax.numpy as jnp
from jax import lax
from jax.experimental import pallas as pl
from jax.experimental.pallas import tpu as pltpu

F32 = jnp.float32
BF16 = jnp.bfloat16

D_MODEL = 2048
DEPTH = 2
GRID_W = 64
ROPE_BASE = 10000.0
NORM_EPS = 1e-6
MLA_HEADS = 8
MLA_Q_LORA = 512
MLA_KV_LORA = 512
MLA_NOPE = 128
MLA_ROPE = 64
MLA_V = 128
MLA_QK = MLA_NOPE + MLA_ROPE
MLA_SCALE = MLA_QK ** -0.5
MLA_WIDTH = MLA_HEADS * MLA_V
MLA_HEAD_PAD = 256
GLA_HEADS = 4
GLA_DK = 128
GLA_DV = 256
GLA_GATE_RANK = 16
GLA_GATE_NORM = 16.0
GLA_WIDTH = GLA_HEADS * GLA_DV
GLA_CHUNK = 128
SWA_HEADS = 16
SWA_KV_HEADS = 4
SWA_HEAD_DIM = 64
SWA_WINDOW = 128
SWA_SCALE = SWA_HEAD_DIM ** -0.5
SWA_WIDTH = SWA_HEADS * SWA_HEAD_DIM
D_FF = 5632
N_MOD = 9
IN_SPLITS = (MLA_Q_LORA, MLA_KV_LORA, MLA_ROPE,
             GLA_HEADS * GLA_DK, GLA_HEADS * GLA_DK, GLA_WIDTH, GLA_WIDTH, GLA_GATE_RANK, GLA_GATE_RANK,
             SWA_WIDTH, SWA_KV_HEADS * SWA_HEAD_DIM, SWA_KV_HEADS * SWA_HEAD_DIM,
             D_MODEL, D_MODEL, D_MODEL)

V7X_LANES = 128
V7X_SUBLANES = 8
V7X_VMEM_BYTES = 64 * 1024 * 1024
MIB = 1024 * 1024

NT_DIMS = (((1,), (1,)), ((), ()))
LOG2_E = 1.4426950408889634

COL_GATE_MLA = 0
COL_GATE_GLA = 2048
COL_GATE_SWA = 4096
COL_CQ = 6144
COL_CKV = 6656
COL_GQ = 7168
COL_GK = 7680
COL_GV = 8192
COL_GOUT = 9216
COL_SQ = 10240
COL_SK = 11264
COL_SV = 11520
COL_KPE = 11776
COL_GG = 11904
PROJ_COLS = 12288
PROJ_TN = 1024


def _vmem_limit(nbytes):
    return int(min(nbytes + 12 * MIB, V7X_VMEM_BYTES - 6 * MIB))


def _rope_partner(x):
    lane = lax.broadcasted_iota(jnp.int32, x.shape, 1)
    return jnp.where((lane & 31) < 16, pltpu.roll(x, 112, axis=1), pltpu.roll(x, 16, axis=1))


ADA_TN = 1024


def _adaln_kernel(cond_ref, w_ref, b_ref, o_ref):
    c = cond_ref[...]
    s = (c * jax.nn.sigmoid(c)).astype(BF16)
    o_ref[...] = jnp.dot(s, w_ref[...].astype(BF16), preferred_element_type=F32) + b_ref[...]


def _adaln(cond8, w_ada, b_ada):
    halves = D_MODEL // ADA_TN
    nj = N_MOD * halves
    return pl.pallas_call(
        _adaln_kernel,
        grid=(DEPTH, nj),
        in_specs=[
            pl.BlockSpec((8, D_MODEL), lambda l, j: (0, 0)),
            pl.BlockSpec((None, D_MODEL, ADA_TN), lambda l, j: (l, 0, j)),
            pl.BlockSpec((None, 1, ADA_TN), lambda l, j: (l, 0, j)),
        ],
        out_specs=pl.BlockSpec((None, None, 8, ADA_TN), lambda l, j: (l, j // halves, 0, j % halves)),
        out_shape=jax.ShapeDtypeStruct((DEPTH, N_MOD, 8, D_MODEL), F32),
        compiler_params=pltpu.CompilerParams(
            dimension_semantics=("arbitrary", "arbitrary"),
            vmem_limit_bytes=_vmem_limit(2 * D_MODEL * ADA_TN * 4)),
        name="adaln",
    )(cond8, w_ada, b_ada.reshape(DEPTH, 1, N_MOD * D_MODEL))


NORM_ROWS = 64


def _modulated_norm_to(h_ref, x_ref, g, shift, scale):
    gain = g * (1.0 + scale)

    def body(c, carry):
        r0 = pl.multiple_of(c * NORM_ROWS, NORM_ROWS)
        x = x_ref[pl.ds(r0, NORM_ROWS), :]
        ms = jnp.mean(x * x, axis=-1, keepdims=True)
        h_ref[pl.ds(r0, NORM_ROWS), :] = ((x * lax.rsqrt(ms + NORM_EPS)) * gain + shift).astype(BF16)
        return carry

    lax.fori_loop(0, x_ref.shape[0] // NORM_ROWS, body, 0, unroll=2)


def _cond_row(i, tm, base_row, rows_per_cond):
    return base_row + (i * tm) // rows_per_cond


FFN_TM = 1024
FFN_TF = 256
FFN_TN = 512


def _ffn_kernel(x_ref, shift_ref, scale_ref, gate_ref, g_ref, wg_ref, wu_ref, wd_ref, o_ref, h_ref,
                *, layer, base_row, rows_per_cond):
    i = pl.program_id(0)
    j = pl.program_id(1)
    row = _cond_row(i, x_ref.shape[0], base_row, rows_per_cond)

    @pl.when(j == 0)
    def _():
        _modulated_norm_to(h_ref, x_ref, g_ref[pl.ds(layer, 1), :],
                           shift_ref[pl.ds(row, 1), :], scale_ref[pl.ds(row, 1), :])
        o_ref[...] = jnp.zeros_like(o_ref)

    h = h_ref[...]
    a = jnp.dot(h, wg_ref[...].astype(BF16), preferred_element_type=F32)
    u = jnp.dot(h, wu_ref[...].astype(BF16), preferred_element_type=F32)
    act = (a * jax.nn.sigmoid(a) * u).astype(BF16)
    for n in range(0, D_MODEL, FFN_TN):
        o_ref[:, n:n + FFN_TN] += jnp.dot(act, wd_ref[:, n:n + FFN_TN].astype(BF16), preferred_element_type=F32)

    @pl.when(j == pl.num_programs(1) - 1)
    def _():
        o_ref[...] = x_ref[...] + (0.5 * gate_ref[pl.ds(row, 1), :]) * o_ref[...]


def _ffn(x, mods, layer, first_mod, g_norm, w_gu, w_down, base_row, rows_per_cond):
    m = x.shape[0]
    nf = D_FF // FFN_TF
    mod_spec = lambda k: pl.BlockSpec((None, None, 8, D_MODEL), lambda i, j: (layer, first_mod + k, 0, 0))
    est = (2 * FFN_TM * D_MODEL * 4 * 2 + FFN_TM * D_MODEL * 2
           + 2 * 3 * D_MODEL * FFN_TF * w_gu.dtype.itemsize)
    return pl.pallas_call(
        functools.partial(_ffn_kernel, layer=layer, base_row=base_row, rows_per_cond=rows_per_cond),
        grid=(m // FFN_TM, nf),
        in_specs=[
            pl.BlockSpec((FFN_TM, D_MODEL), lambda i, j: (i, 0)),
            mod_spec(0), mod_spec(1), mod_spec(2),
            pl.BlockSpec((DEPTH, D_MODEL), lambda i, j: (0, 0)),
            pl.BlockSpec((None, D_MODEL, FFN_TF), lambda i, j: (layer, 0, j)),
            pl.BlockSpec((None, D_MODEL, FFN_TF), lambda i, j: (layer, 0, j + nf)),
            pl.BlockSpec((None, FFN_TF, D_MODEL), lambda i, j: (layer, j, 0)),
        ],
        out_specs=pl.BlockSpec((FFN_TM, D_MODEL), lambda i, j: (i, 0)),
        out_shape=jax.ShapeDtypeStruct((m, D_MODEL), F32),
        scratch_shapes=[pltpu.VMEM((FFN_TM, D_MODEL), BF16)],
        compiler_params=pltpu.CompilerParams(
            dimension_semantics=("arbitrary", "arbitrary"), vmem_limit_bytes=_vmem_limit(est)),
        name="ffn",
    )(x, mods, mods, mods, g_norm, w_gu, w_gu, w_down)


PROJ_TM = 1024


IN_COLS = sum(IN_SPLITS)
_SRC = dict(zip(('cq', 'ckv', 'kpe', 'gq', 'gk', 'gv', 'gout', 'ggf', 'ggb', 'sq', 'sk', 'sv', 'gate_mla'),
                np.concatenate([[0], np.cumsum(IN_SPLITS)]).tolist()))
PACK_ROWS = 512
PACK_TILES = PROJ_COLS // PACK_ROWS


def _pack_plan():
    small, a, b = [], [], []
    for dst, src, width in ((COL_GATE_MLA, _SRC['gate_mla'], 3 * D_MODEL), (COL_CQ, _SRC['cq'], 1024),
                            (COL_GQ, _SRC['gq'], 3072), (COL_SQ, _SRC['sq'], 1536)):
        assert dst == len(a) * PACK_ROWS and width % PACK_ROWS == 0
        for t in range(width // PACK_ROWS):
            small.append(0); a.append(src + t * PACK_ROWS); b.append(_SRC['ggf'])
    assert COL_KPE == len(a) * PACK_ROWS and COL_GG == COL_KPE + 128
    small.append(1); a.append(_SRC['kpe']); b.append(_SRC['ggf'])
    assert len(a) == PACK_TILES
    return np.asarray([small, a, b], np.int32)


def _pack_kernel(plan_ref, a3_ref, b3_ref, o_ref):
    is_small = plan_ref[0, pl.program_id(1)] == 1
    a_ref, b_ref = a3_ref.at[0], b3_ref.at[0]

    @pl.when(jnp.logical_not(is_small))
    def _():
        o_ref[...] = a_ref[...].astype(BF16)

    @pl.when(is_small)
    def _():
        o_ref[...] = jnp.zeros_like(o_ref)
        o_ref[0:MLA_ROPE, :] = a_ref[0:MLA_ROPE, :].astype(BF16)
        o_ref[128:128 + 2 * GLA_GATE_RANK, :] = b_ref[0:2 * GLA_GATE_RANK, :].astype(BF16)


def _pack_w_in(w_in):
    w_t = jnp.swapaxes(w_in, 1, 2)
    window = lambda which: pl.BlockSpec((pl.Element(1), pl.Element(PACK_ROWS), pl.Element(D_MODEL)),
                                        lambda l, j, plan: (l, pl.multiple_of(plan[which, j], 32), 0))
    return pl.pallas_call(
        _pack_kernel,
        grid_spec=pltpu.PrefetchScalarGridSpec(
            num_scalar_prefetch=1, grid=(DEPTH, PACK_TILES),
            in_specs=[window(1), window(2)],
            out_specs=pl.BlockSpec((None, PACK_ROWS, D_MODEL), lambda l, j, plan: (l, j, 0))),
        out_shape=jax.ShapeDtypeStruct((DEPTH, PROJ_COLS, D_MODEL), BF16),
        compiler_params=pltpu.CompilerParams(
            dimension_semantics=("arbitrary", "arbitrary"),
            vmem_limit_bytes=_vmem_limit(2 * PACK_ROWS * D_MODEL * (4 + 4 + 2))),
        name="pack_w_in",
    )(jnp.asarray(_pack_plan()), w_t, w_t)


def _proj_kernel(x_ref, shift_ref, scale_ref, g_ref, w_ref, o_ref, h_ref, *, layer, base_row, rows_per_cond):
    i = pl.program_id(0)
    j = pl.program_id(1)
    row = _cond_row(i, x_ref.shape[0], base_row, rows_per_cond)

    @pl.when(j == 0)
    def _():
        _modulated_norm_to(h_ref, x_ref, g_ref[pl.ds(layer, 1), :],
                           shift_ref[pl.ds(row, 1), :], scale_ref[pl.ds(row, 1), :])

    o_ref[...] = lax.dot_general(h_ref[...], w_ref[...], NT_DIMS, preferred_element_type=F32)


def _in_proj(x, mods, layer, g_norm2, w_packed, base_row, rows_per_cond):
    m = x.shape[0]
    ncols = w_packed.shape[1]
    mod_spec = lambda k: pl.BlockSpec((None, None, 8, D_MODEL), lambda i, j: (layer, 3 + k, 0, 0))
    est = 2 * PROJ_TM * D_MODEL * 4 + PROJ_TM * D_MODEL * 2 + 2 * D_MODEL * PROJ_TN * 2 + 2 * PROJ_TM * PROJ_TN * 4
    return pl.pallas_call(
        functools.partial(_proj_kernel, layer=layer, base_row=base_row, rows_per_cond=rows_per_cond),
        grid=(m // PROJ_TM, ncols // PROJ_TN),
        in_specs=[
            pl.BlockSpec((PROJ_TM, D_MODEL), lambda i, j: (i, 0)),
            mod_spec(0), mod_spec(1),
            pl.BlockSpec((DEPTH, D_MODEL), lambda i, j: (0, 0)),
            pl.BlockSpec((None, PROJ_TN, D_MODEL), lambda i, j: (layer, j, 0)),
        ],
        out_specs=pl.BlockSpec((PROJ_TM, PROJ_TN), lambda i, j: (i, j)),
        out_shape=jax.ShapeDtypeStruct((m, ncols), F32),
        scratch_shapes=[pltpu.VMEM((PROJ_TM, D_MODEL), BF16)],
        compiler_params=pltpu.CompilerParams(
            dimension_semantics=("arbitrary", "arbitrary"), vmem_limit_bytes=_vmem_limit(est)),
        name="in_proj",
    )(x, mods, mods, g_norm2, w_packed)


def _rope_tables(t):
    pos = np.arange(t)
    inv_freq = (ROPE_BASE ** (-np.arange(16, dtype=np.float32) / 16)).astype(np.float32)

    def cs(p):
        ang = p.astype(np.float32)[:, None] * inv_freq[None, :]
        return np.concatenate([np.cos(ang)] * 2, axis=1), np.concatenate([np.sin(ang)] * 2, axis=1)

    cr, sr = cs(pos // GRID_W)
    cc, sc = cs(pos % GRID_W)
    sign = np.where(np.arange(64) % 32 < 16, -1.0, 1.0).astype(np.float32)
    return (np.concatenate([cr, cc], axis=1).astype(np.float32),
            (np.concatenate([sr, sc], axis=1) * sign[None, :]).astype(np.float32))


MLA_TR = 256
MLA_TQ = 256


def _pack_mla(w_uq_l, g_qn_l, g_kn_l):
    w = w_uq_l.reshape(MLA_Q_LORA, MLA_HEADS, MLA_QK)
    z64 = jnp.zeros((MLA_Q_LORA, MLA_HEADS, 64), w.dtype)
    wq = jnp.concatenate([w, z64], axis=-1).reshape(MLA_Q_LORA, -1).astype(BF16)
    v64 = jnp.zeros((64,), F32)
    gq_full = jnp.concatenate([g_qn_l, v64])[None]
    gk_n = g_kn_l[:MLA_NOPE][None]
    gk_r = jnp.concatenate([g_kn_l[MLA_NOPE:], v64])[None]
    return wq, gq_full, gk_n, gk_r


def _mla_kv_kernel(*refs, layer, normalize, rope, emit_ckv, values_t):
    it = iter(refs)
    ckv_ref, kpe_ref = next(it), next(it)
    if rope:
        c_ref, s_ref = next(it), next(it)
    w_ref, gkv_ref, gn_ref, gr_ref = next(it), next(it), next(it), next(it)
    k_ref, v_ref = next(it), next(it)
    ckv = ckv_ref[...]
    if normalize:
        ckv = (ckv * lax.rsqrt(jnp.mean(ckv * ckv, axis=-1, keepdims=True) + NORM_EPS)) * gkv_ref[pl.ds(layer, 1), :]
    if emit_ckv:
        next(it)[...] = ckv
    kv = jnp.dot(ckv.astype(BF16), w_ref[...].astype(BF16), preferred_element_type=F32)
    kpe = kpe_ref[...]
    ss_pe = jnp.sum(kpe * kpe, axis=-1, keepdims=True)
    kr = kpe * gr_ref[...]
    if rope:
        kr = kr * c_ref[...] + _rope_partner(kr) * s_ref[...]
    for h in range(MLA_HEADS):
        kn = kv[:, 256 * h:256 * h + 128]
        r = lax.rsqrt((jnp.sum(kn * kn, axis=-1, keepdims=True) + ss_pe) * (1.0 / MLA_QK) + NORM_EPS)
        k_ref[:, 256 * h:256 * h + 128] = ((kn * r) * gn_ref[...]).astype(BF16)
        k_ref[:, 256 * h + 128:256 * h + 256] = (kr * r).astype(BF16)
        v = kv[:, 256 * h + 128:256 * h + 256]
        if values_t:
            v_ref[128 * h:128 * h + 128, :] = v.T.astype(BF16)
        else:
            v_ref[:, 128 * h:128 * h + 128] = v.astype(BF16)


def _mla_kv(ckv_src, ckv_spec, kpe_src, kpe_spec, rope_tabs, w_ukv, g_kv, gk_n, gk_r,
            batch, rows, layer, normalize, emit_ckv, values_t):
    rope = rope_tabs is not None
    nt = rows // MLA_TR
    ins = [ckv_src, kpe_src]
    specs = [ckv_spec, kpe_spec]
    if rope:
        ins += list(rope_tabs)
        specs += [pl.BlockSpec((MLA_TR, 128), lambda b, t: (t, 0))] * 2
    ins += [w_ukv, g_kv, gk_n, gk_r]
    specs += [pl.BlockSpec((None, MLA_KV_LORA, 2048), lambda b, t: (layer, 0, 0)),
              pl.BlockSpec((DEPTH, MLA_KV_LORA), lambda b, t: (0, 0))] + [pl.BlockSpec((1, 128), lambda b, t: (0, 0))] * 2
    out_shape = [jax.ShapeDtypeStruct((batch, rows, MLA_HEADS * MLA_HEAD_PAD), BF16),
                 jax.ShapeDtypeStruct((batch, MLA_WIDTH, rows) if values_t else (batch, rows, MLA_WIDTH), BF16)]
    out_specs = [pl.BlockSpec((None, MLA_TR, MLA_HEADS * MLA_HEAD_PAD), lambda b, t: (b, t, 0)),
                 pl.BlockSpec((None, MLA_WIDTH, MLA_TR), lambda b, t: (b, 0, t)) if values_t
                 else pl.BlockSpec((None, MLA_TR, MLA_WIDTH), lambda b, t: (b, t, 0))]
    if emit_ckv:
        out_shape.append(jax.ShapeDtypeStruct((batch, rows, MLA_KV_LORA), F32))
        out_specs.append(pl.BlockSpec((None, MLA_TR, MLA_KV_LORA), lambda b, t: (b, t, 0)))
    est = 2 * MLA_KV_LORA * 2048 * 4 + 4 * MLA_TR * 2048 * 4
    return pl.pallas_call(
        functools.partial(_mla_kv_kernel, layer=layer, normalize=normalize, rope=rope, emit_ckv=emit_ckv,
                          values_t=values_t),
        grid=(batch, nt), in_specs=specs, out_specs=out_specs, out_shape=out_shape,
        compiler_params=pltpu.CompilerParams(
            dimension_semantics=("arbitrary", "arbitrary"), vmem_limit_bytes=_vmem_limit(est)),
        name="mla_kv",
    )(*ins)


def _mla_attn_kernel(*refs, layer, rope, has_ctx, values_t):
    it = iter(refs)
    cq_ref, wq_ref, gq_ref, gfull_ref = next(it), next(it), next(it), next(it)
    if rope:
        c_ref, s_ref = next(it), next(it)
    k_ref, v_ref = next(it), next(it)
    if has_ctx:
        kc_ref, vc_ref = next(it), next(it)
    o_ref = next(it)
    cq = cq_ref[...]
    ql = ((cq * lax.rsqrt(jnp.mean(cq * cq, axis=-1, keepdims=True) + NORM_EPS)) * gq_ref[pl.ds(layer, 1), :]).astype(BF16)
    q_raw = jnp.dot(ql, wq_ref[...], preferred_element_type=F32)
    for h in range(MLA_HEADS):
        sl = slice(256 * h, 256 * h + 256)
        vs = slice(128 * h, 128 * h + 128)
        q = q_raw[:, sl]
        r = lax.rsqrt(jnp.sum(q * q, axis=-1, keepdims=True) * (1.0 / MLA_QK) + NORM_EPS)
        qh = q * gfull_ref[...]
        if rope:
            q_rope = qh[:, 128:]
            qh = jnp.concatenate([qh[:, :128], q_rope * c_ref[...] + _rope_partner(q_rope) * s_ref[...]], axis=1)
        qh = ((qh * r) * (MLA_SCALE * LOG2_E)).astype(BF16)
        s = lax.dot_general(qh, k_ref[:, sl], NT_DIMS, preferred_element_type=F32)
        m = jnp.max(s, axis=-1, keepdims=True)
        if has_ctx:
            sc = lax.dot_general(qh, kc_ref[:, sl], NT_DIMS, preferred_element_type=F32)
            m = jnp.maximum(m, jnp.max(sc, axis=-1, keepdims=True))
        e = jnp.exp2(s - m)
        den = jnp.sum(e, axis=-1, keepdims=True)

        def weighted_values(p, val_ref):
            if values_t:
                return lax.dot_general(val_ref[vs, :], p.astype(BF16), NT_DIMS, preferred_element_type=F32)
            return jnp.dot(p.astype(BF16), val_ref[:, vs], preferred_element_type=F32)

        o = weighted_values(e, v_ref)
        if has_ctx:
            ec = jnp.exp2(sc - m)
            den = den + jnp.sum(ec, axis=-1, keepdims=True)
            o = o + weighted_values(ec, vc_ref)
        if values_t:
            o = o.T
        o_ref[:, vs] = (o * (1.0 / den)).astype(BF16)


def _mla_attn(proj, wq, g_q, gq_full, rope_tabs, k, v, ctx_kv, batch, t, layer, values_t):
    rope = rope_tabs is not None
    has_ctx = ctx_kv is not None
    nq = t // MLA_TQ
    v_spec = lambda n: (pl.BlockSpec((None, MLA_WIDTH, n), lambda b, i: (b, 0, 0)) if values_t
                        else pl.BlockSpec((None, n, MLA_WIDTH), lambda b, i: (b, 0, 0)))
    const = lambda shape: pl.BlockSpec(shape, lambda b, i: (0,) * len(shape))
    ins = [proj, wq, g_q, gq_full]
    specs = [pl.BlockSpec((MLA_TQ, MLA_Q_LORA), lambda b, i: (b * nq + i, COL_CQ // MLA_Q_LORA)),
             const((MLA_Q_LORA, 2048)), const((DEPTH, MLA_Q_LORA)), const((1, 256))]
    if rope:
        ins += list(rope_tabs)
        specs += [pl.BlockSpec((MLA_TQ, 128), lambda b, i: (i, 0))] * 2
    ins += [k, v]
    specs += [pl.BlockSpec((None, t, 2048), lambda b, i: (b, 0, 0)), v_spec(t)]
    est = 2 * (t * 2048 * 2 + t * MLA_WIDTH * 2) + 6 * MLA_TQ * t * 4
    if has_ctx:
        kc, vc = ctx_kv
        lc = kc.shape[1]
        ins += [kc, vc]
        specs += [pl.BlockSpec((None, lc, 2048), lambda b, i: (b, 0, 0)), v_spec(lc)]
        est += 2 * lc * 3072 * 2
    return pl.pallas_call(
        functools.partial(_mla_attn_kernel, layer=layer, rope=rope, has_ctx=has_ctx, values_t=values_t),
        grid=(batch, nq), in_specs=specs,
        out_specs=pl.BlockSpec((MLA_TQ, MLA_WIDTH), lambda b, i: (b * nq + i, 0)),
        out_shape=jax.ShapeDtypeStruct((batch * t, MLA_WIDTH), BF16),
        compiler_params=pltpu.CompilerParams(
            dimension_semantics=("arbitrary", "arbitrary"), vmem_limit_bytes=_vmem_limit(est)),
        name="mla_attn",
    )(*ins)


def _mla_branch(proj, batch, t, layer, mla_w, w_ukv, g_q, g_kv, rope, ctx):
    wq, gq_full, gk_n, gk_r = mla_w
    nt = t // MLA_TR
    ckv_spec = pl.BlockSpec((MLA_TR, MLA_KV_LORA), lambda b, i: (b * nt + i, COL_CKV // MLA_KV_LORA))
    kpe_spec = pl.BlockSpec((MLA_TR, 128), lambda b, i: (b * nt + i, COL_KPE // 128))
    if rope is None:
        k, v, ckv_n = _mla_kv(proj, ckv_spec, proj, kpe_spec, None, w_ukv, g_kv, gk_n, gk_r,
                              batch, t, layer, normalize=True, emit_ckv=True, values_t=False)
        return _mla_attn(proj, wq, g_q, gq_full, None, k, v, None, batch, t, layer, values_t=False), ckv_n
    c64, s64 = rope
    z64 = np.zeros_like(c64)
    tabs = (jnp.asarray(np.concatenate([c64, z64], axis=1)),
            jnp.asarray(np.concatenate([s64, z64], axis=1)))
    k, v = _mla_kv(proj, ckv_spec, proj, kpe_spec, tabs, w_ukv, g_kv, gk_n, gk_r,
                   batch, t, layer, normalize=True, emit_ckv=False, values_t=True)
    ctx_ckv, ctx_kpe = ctx
    lc = ctx_ckv.shape[2]
    kc, vc = _mla_kv(ctx_ckv, pl.BlockSpec((None, None, MLA_TR, MLA_KV_LORA), lambda b, i: (b, layer, i, 0)),
                     ctx_kpe, pl.BlockSpec((None, None, MLA_TR, 128), lambda b, i: (b, layer, i, 0)),
                     None, w_ukv, g_kv, gk_n, gk_r, batch, lc, layer, normalize=False, emit_ckv=False, values_t=True)
    return _mla_attn(proj, wq, g_q, gq_full, tabs, k, v, (kc, vc), batch, t, layer, values_t=True), None


SWA_TR = 128
SWA_NEG = -1e30


def _pair_sum_matrix():
    g = (np.arange(128)[:, None] // 64 == np.arange(128)[None, :] // 64).astype(np.float32)
    return jnp.asarray(np.concatenate([g, g], axis=0), BF16)


def _group_rms_scale(x, pair_ref):
    sq = x * x
    hi = sq.astype(BF16)
    lo = (sq - hi.astype(F32)).astype(BF16)
    ss = jnp.dot(jnp.concatenate([hi, lo], axis=1), pair_ref[...], preferred_element_type=F32)
    return lax.rsqrt(ss * (1.0 / SWA_HEAD_DIM) + NORM_EPS)


def _swa_kv_kernel(*refs, normalize, rope, emit, pad_blocks, layer):
    it = iter(refs)
    k_ref, v_ref = next(it), next(it)
    if rope:
        c_ref, s_ref = next(it), next(it)
    if normalize:
        g_ref, pair_ref = next(it), next(it)
    ko_ref, vo_ref = next(it), next(it)
    if emit:
        kn_ref, vn_ref = next(it), next(it)
    lane = lax.broadcasted_iota(jnp.int32, (k_ref.shape[0], 128), 1)
    low = lane < 64

    def halves(x, c, o_ref):
        sw = pltpu.roll(x, 64, axis=1)
        zero = jnp.zeros_like(x)
        o_ref[4 * c + 0] = jnp.where(low, x, zero).astype(BF16)
        o_ref[4 * c + 1] = jnp.where(low, zero, sw).astype(BF16)
        o_ref[4 * c + 2] = jnp.where(low, sw, zero).astype(BF16)
        o_ref[4 * c + 3] = jnp.where(low, zero, x).astype(BF16)

    def compute():
        for c in range(2):
            sl = slice(128 * c, 128 * c + 128)
            kb = k_ref[:, sl]
            if normalize:
                r = _group_rms_scale(kb, pair_ref)
                kn = (kb * r) * g_ref[...]
                if emit:
                    kn_ref[:, sl] = kn
                if rope:
                    kn = kn * c_ref[...] + _rope_partner(kn) * s_ref[...]
            else:
                kn = kb
            halves(kn, c, ko_ref)
            vb = v_ref[:, sl]
            if emit:
                vn_ref[:, sl] = vb
            halves(vb, c, vo_ref)

    if pad_blocks:
        t = pl.program_id(1)
        is_pad = jnp.logical_or(t == 0, t == pl.num_programs(1) - 1)

        @pl.when(is_pad)
        def _():
            ko_ref[...] = jnp.zeros_like(ko_ref)
            vo_ref[...] = jnp.zeros_like(vo_ref)

        pl.when(jnp.logical_not(is_pad))(compute)
    else:
        compute()


def _swa_kv(k_src, k_spec, v_src, v_spec, rope_tabs, norm_ins, batch, rows, tr, layer, emit, pad_blocks):
    rope = rope_tabs is not None
    normalize = norm_ins is not None
    nt = rows // tr + (2 if pad_blocks else 0)
    ins, specs = [k_src, v_src], [k_spec, v_spec]
    if rope:
        tab_row = (lambda t: jnp.clip(t - 1, 0, nt - 3)) if pad_blocks else (lambda t: t)
        ins += list(rope_tabs)
        specs += [pl.BlockSpec((tr, 128), lambda b, t: (tab_row(t), 0))] * 2
    if normalize:
        ins += list(norm_ins)
        specs += [pl.BlockSpec((1, 128), lambda b, t: (0, 0)), pl.BlockSpec((256, 128), lambda b, t: (0, 0))]
    out_shape = [jax.ShapeDtypeStruct((batch, 8, nt * tr, 128), BF16)] * 2
    out_specs = [pl.BlockSpec((None, 8, tr, 128), lambda b, t: (b, 0, t, 0))] * 2
    if emit:
        out_shape += [jax.ShapeDtypeStruct((batch, rows, 256), F32)] * 2
        out_specs += [pl.BlockSpec((None, tr, 256), lambda b, t: (b, t, 0))] * 2
    return pl.pallas_call(
        functools.partial(_swa_kv_kernel, normalize=normalize, rope=rope, emit=emit, pad_blocks=pad_blocks, layer=layer),
        grid=(batch, nt), in_specs=specs, out_specs=out_specs, out_shape=out_shape,
        compiler_params=pltpu.CompilerParams(dimension_semantics=("arbitrary", "arbitrary")),
        name="swa_kv",
    )(*ins)


def _swa_attn_kernel(*refs, rope, windowed, has_ctx, t_total, layer):
    it = iter(refs)
    sink_ref, q_ref = next(it), next(it)
    if rope:
        c_ref, s_ref = next(it), next(it)
    g_ref, pair_ref = next(it), next(it)
    k_ref, v_ref = next(it), next(it)
    if has_ctx:
        kc_ref, vc_ref = next(it), next(it)
    o_ref = next(it)
    n = pl.program_id(1)
    if windowed:
        start = pl.multiple_of(n * SWA_TR, SWA_TR)
        win = pl.ds(start, 3 * SWA_TR)
        r_i = lax.broadcasted_iota(jnp.int32, (2 * SWA_TR, 3 * SWA_TR), 0) & (SWA_TR - 1)
        c_i = lax.broadcasted_iota(jnp.int32, (2 * SWA_TR, 3 * SWA_TR), 1)
        kpos = (n - 1) * SWA_TR + c_i
        diff = SWA_TR + r_i - c_i
        valid2 = (kpos >= 0) & (kpos < t_total) & (diff <= SWA_WINDOW) & (diff >= -SWA_WINDOW)
    else:
        win = slice(None)
    def normed_queries(cb):
        sl = slice(128 * cb, 128 * cb + 128)
        qb = q_ref[:, sl]
        r = _group_rms_scale(qb, pair_ref)
        qn = (qb * r) * g_ref[...]
        if rope:
            qn = qn * c_ref[...] + _rope_partner(qn) * s_ref[...]
        return (qn * (SWA_SCALE * LOG2_E)).astype(BF16)

    tq = q_ref.shape[0]
    low_lanes = lax.broadcasted_iota(jnp.int32, (2 * tq, 128), 1) < SWA_HEAD_DIM

    lk = 3 * SWA_TR if windowed else k_ref.shape[1]
    lc = kc_ref.shape[1] if has_ctx else 0
    row_blk = lax.broadcasted_iota(jnp.int32, (2 * tq, 1), 0) // tq
    for j in range(SWA_KV_HEADS):
        q2 = jnp.concatenate([normed_queries(2 * j), normed_queries(2 * j + 1)], axis=0)
        k_parts, v_parts = [], []
        for e in range(2):
            k_parts.append(k_ref[2 * j + e, win, :])
            v_parts.append(v_ref[2 * j + e, win, :])
            if has_ctx:
                k_parts.append(kc_ref[2 * j + e])
                v_parts.append(vc_ref[2 * j + e])
        s = lax.dot_general(q2, jnp.concatenate(k_parts, axis=0), NT_DIMS, preferred_element_type=F32)
        p_parts, inv = [], []
        for e in range(2):
            off = e * (lk + lc)
            sink = jnp.where(row_blk == 0, sink_ref[layer, 4 * j + e], sink_ref[layer, 4 * j + 2 + e]) * LOG2_E
            s_loc = s[:, off:off + lk]
            if windowed:
                s_loc = jnp.where(valid2, s_loc, SWA_NEG)
            m = jnp.maximum(jnp.max(s_loc, axis=-1, keepdims=True), sink)
            if has_ctx:
                s_ctx = s[:, off + lk:off + lk + lc]
                m = jnp.maximum(m, jnp.max(s_ctx, axis=-1, keepdims=True))
            p = jnp.exp2(s_loc - m)
            den = jnp.sum(p, axis=-1, keepdims=True) + jnp.exp2(sink - m)
            p_parts.append(p.astype(BF16))
            if has_ctx:
                pc = jnp.exp2(s_ctx - m)
                den = den + jnp.sum(pc, axis=-1, keepdims=True)
                p_parts.append(pc.astype(BF16))
            inv.append(1.0 / den)
        o = jnp.dot(jnp.concatenate(p_parts, axis=1), jnp.concatenate(v_parts, axis=0), preferred_element_type=F32)
        o = o * jnp.where(low_lanes, inv[0], inv[1])
        o_ref[:, 256 * j:256 * j + 128] = o[:tq].astype(BF16)
        o_ref[:, 256 * j + 128:256 * j + 256] = o[tq:].astype(BF16)


def _swa_attn(sink, proj, rope_tabs, norm_ins, k, v, ctx_kv, batch, t, layer):
    rope = rope_tabs is not None
    has_ctx = ctx_kv is not None
    tq = SWA_TR if has_ctx else 2 * SWA_TR
    nq = t // tq
    ins = [sink, proj]
    specs = [pl.BlockSpec(memory_space=pltpu.SMEM),
             pl.BlockSpec((tq, SWA_WIDTH), lambda b, i: (b * nq + i, COL_SQ // SWA_WIDTH))]
    if rope:
        ins += list(rope_tabs)
        specs += [pl.BlockSpec((tq, 128), lambda b, i: (i, 0))] * 2
    ins += list(norm_ins)
    specs += [pl.BlockSpec((1, 128), lambda b, i: (0, 0)), pl.BlockSpec((256, 128), lambda b, i: (0, 0))]
    lk = k.shape[2]
    ins += [k, v]
    specs += [pl.BlockSpec((None, 8, lk, 128), lambda b, i: (b, 0, 0, 0))] * 2
    if has_ctx:
        lc = ctx_kv[0].shape[2]
        ins += list(ctx_kv)
        specs += [pl.BlockSpec((None, 8, lc, 128), lambda b, i: (b, 0, 0, 0))] * 2
    return pl.pallas_call(
        functools.partial(_swa_attn_kernel, rope=rope, windowed=has_ctx, has_ctx=has_ctx, t_total=t, layer=layer),
        grid=(batch, nq), in_specs=specs,
        out_specs=pl.BlockSpec((tq, SWA_WIDTH), lambda b, i: (b * nq + i, 0)),
        out_shape=jax.ShapeDtypeStruct((batch * t, SWA_WIDTH), BF16),
        compiler_params=pltpu.CompilerParams(dimension_semantics=("arbitrary", "arbitrary")),
        name="swa_attn",
    )(*ins)


def _swa_branch(proj, batch, t, layer, g_qn_l, g_kn_l, sink, rope, ctx):
    pair = _pair_sum_matrix()
    tile2 = lambda g: jnp.concatenate([g, g])[None]
    wide = 2 * SWA_TR
    if rope is None:
        nw = t // wide
        k_spec = pl.BlockSpec((wide, 256), lambda b, i: (b * nw + i, COL_SK // 256))
        v_spec = pl.BlockSpec((wide, 256), lambda b, i: (b * nw + i, COL_SV // 256))
        k, v, k_n, v_raw = _swa_kv(proj, k_spec, proj, v_spec, None, (tile2(g_kn_l), pair),
                                   batch, t, wide, layer, emit=True, pad_blocks=False)
        o = _swa_attn(sink, proj, None, (tile2(g_qn_l), pair), k, v, None, batch, t, layer)
        return o, k_n, v_raw
    nt = t // SWA_TR
    c64, s64 = rope
    tabs = (jnp.asarray(np.concatenate([c64, c64], axis=1)),
            jnp.asarray(np.concatenate([s64, s64], axis=1)))
    row = lambda b, i: b * nt + jnp.clip(i - 1, 0, nt - 1)
    k_spec = pl.BlockSpec((SWA_TR, 256), lambda b, i: (row(b, i), COL_SK // 256))
    v_spec = pl.BlockSpec((SWA_TR, 256), lambda b, i: (row(b, i), COL_SV // 256))
    k, v = _swa_kv(proj, k_spec, proj, v_spec, tabs, (tile2(g_kn_l), pair), batch, t, SWA_TR, layer,
                   emit=False, pad_blocks=True)
    ctx_k, ctx_v = ctx
    lc = ctx_k.shape[2]
    c_spec = pl.BlockSpec((None, None, wide, 256), lambda b, i: (b, layer, i, 0))
    kc, vc = _swa_kv(ctx_k, c_spec, ctx_v, c_spec, None, None, batch, lc, wide, layer, emit=False, pad_blocks=False)
    o = _swa_attn(sink, proj, tabs, (tile2(g_qn_l), pair), k, v, (kc, vc), batch, t, layer)
    return o, None, None


GLA_LEVELS = (64, 32, 16, 8, 4, 2, 1)
GLA_MXU_LEVELS = (2, 1)
GLA_ROWSETS = 1 + len(GLA_MXU_LEVELS)


def _gla_sum_matrix(backward):
    c = GLA_CHUNK
    t = np.arange(c)[:, None]
    j = np.arange(c)[None, :]
    sets = [j <= t]
    for g in GLA_MXU_LEVELS:
        e = (t // (2 * g)) * 2 * g + g - 1
        upper = (t // g) % 2 == 1
        sets.append(np.where(upper, (j > e) & (j <= t), (j > t) & (j <= e)))
    n = np.concatenate(sets, axis=0).astype(np.float32)
    if backward:
        n = n.reshape(GLA_ROWSETS, c, c)[:, ::-1, ::-1].reshape(GLA_ROWSETS * c, c)
    return jnp.asarray(np.concatenate([n, n, n], axis=1), BF16)


def _gla_kernel(*refs, has_s0):
    it = iter(refs)
    srcs = [tuple(next(it) for _ in range(4)) for _ in range(2)]
    nmat = (next(it), next(it))
    wg = (next(it), next(it))
    bg = (next(it), next(it))
    s0_ref = next(it) if has_s0 else None
    o_refs = (next(it), next(it))
    sfin_ref = next(it)
    s_ref = next(it)
    i = pl.program_id(1)
    c = GLA_CHUNK

    @pl.when(i == 0)
    def _():
        s_ref[...] = s0_ref[...] if has_s0 else jnp.zeros_like(s_ref)

    row = lax.broadcasted_iota(jnp.int32, (c, c), 0)
    col = lax.broadcasted_iota(jnp.int32, (c, c), 1)
    diag = row == col
    log2 = lambda g: int(g).bit_length() - 1
    second_half = [((row >> log2(g)) & 1) == 1 for g in GLA_LEVELS]
    same_block = [(row >> (log2(g) + 1)) == (col >> (log2(g) + 1)) for g in GLA_LEVELS]
    for d in range(2):
        q_ref, k_ref, v_ref, gg_ref = srcs[d]
        z = jnp.dot(gg_ref[...].astype(BF16), wg[d][...], preferred_element_type=F32) + bg[d][...]
        la = (jnp.minimum(z, 0.0) - jnp.log1p(jnp.exp(-jnp.abs(z)))) * (1.0 / GLA_GATE_NORM)
        hi = la.astype(BF16)
        r1 = la - hi.astype(F32)
        mid = r1.astype(BF16)
        lo = (r1 - mid.astype(F32)).astype(BF16)
        ex_all = jnp.dot(nmat[d][...], jnp.concatenate([hi, mid, lo], axis=0), preferred_element_type=F32)
        for h in range(GLA_HEADS):
            sl = slice(GLA_DK * h, GLA_DK * h + GLA_DK)
            vs = slice(GLA_DV * h, GLA_DV * h + GLA_DV)
            ex = ex_all[:, sl]
            q = q_ref[:, sl] * (GLA_DK ** -0.5)
            k = k_ref[:, sl]
            v = v_ref[:, vs].astype(BF16)
            b = ex[0:c]
            eb = jnp.exp(b)
            b_exit = b[c - 1:c] if d == 0 else b[0:1]
            ek = jnp.exp(b_exit - b)
            s_old = s_ref[d, h]
            o = jnp.dot((q * eb).astype(BF16), s_old.astype(BF16), preferred_element_type=F32)
            a = jnp.where(diag, jnp.sum(q * k, axis=-1, keepdims=True), 0.0)
            for li, g in enumerate(GLA_LEVELS):
                if g in GLA_MXU_LEVELS:
                    r = 1 + GLA_MXU_LEVELS.index(g)
                    eg = jnp.exp(ex[r * c:(r + 1) * c])
                else:
                    b3 = b.reshape(c // (2 * g), 2 * g, GLA_DK)
                    edge = g - 1 if d == 0 else g
                    b_edge = jnp.broadcast_to(b3[:, edge:edge + 1, :], b3.shape).reshape(c, GLA_DK)
                    eg = jnp.exp(-jnp.abs(b - b_edge))
                qe, ke = q * eg, k * eg
                late = second_half[li]
                qg = (jnp.where(late, qe, 0.0) if d == 0 else jnp.where(late, 0.0, qe)).astype(BF16)
                kg = (jnp.where(late, 0.0, ke) if d == 0 else jnp.where(late, ke, 0.0)).astype(BF16)
                ag = lax.dot_general(qg, kg, NT_DIMS, preferred_element_type=F32)
                if 2 * g < c:
                    ag = jnp.where(same_block[li], ag, 0.0)
                a = a + ag
            o = o + jnp.dot(a.astype(BF16), v, preferred_element_type=F32)
            o_refs[d][:, vs] = o
            a_col = eb.T[:, c - 1:c] if d == 0 else eb.T[:, 0:1]
            kt = (k * ek).T.astype(BF16)
            s_ref[d, h] = a_col * s_old + jnp.dot(kt, v, preferred_element_type=F32)

    @pl.when(i == pl.num_programs(1) - 1)
    def _():
        sfin_ref[...] = s_ref[...]


def _gla(proj, batch, t, layer, w_gf, b_gf, w_gb, b_gb, s0):
    c = GLA_CHUNK
    n = t // c
    fwd = lambda b, i: b * n + i
    bwd = lambda b, i: b * n + (n - 1 - i)
    ins, specs = [], []
    for rowf in (fwd, bwd):
        ins += [proj] * 4
        specs += [pl.BlockSpec((c, 512), lambda b, i, rowf=rowf: (rowf(b, i), COL_GQ // 512)),
                  pl.BlockSpec((c, 512), lambda b, i, rowf=rowf: (rowf(b, i), COL_GK // 512)),
                  pl.BlockSpec((c, GLA_WIDTH), lambda b, i, rowf=rowf: (rowf(b, i), COL_GV // GLA_WIDTH)),
                  pl.BlockSpec((c, 128), lambda b, i, rowf=rowf: (rowf(b, i), COL_GG // 128))]
    const = lambda shape: pl.BlockSpec(shape, lambda b, i: (0,) * len(shape))
    pad_w = lambda w, off: jnp.zeros((128, 512), F32).at[off:off + GLA_GATE_RANK].set(w).astype(BF16)
    ins += [_gla_sum_matrix(False), _gla_sum_matrix(True), pad_w(w_gf, 0), pad_w(w_gb, GLA_GATE_RANK), b_gf[None], b_gb[None]]
    specs += [const((GLA_ROWSETS * c, 3 * c))] * 2 + [const((128, 512))] * 2 + [const((1, 512))] * 2
    state_spec = pl.BlockSpec((None, 2, GLA_HEADS, GLA_DK, GLA_DV), lambda b, i: (b, 0, 0, 0, 0))
    if s0 is not None:
        ins.append(s0)
        specs.append(pl.BlockSpec((None, None, 2, GLA_HEADS, GLA_DK, GLA_DV), lambda b, i: (b, layer, 0, 0, 0, 0)))
    return pl.pallas_call(
        functools.partial(_gla_kernel, has_s0=s0 is not None),
        grid=(batch, n), in_specs=specs,
        out_specs=[pl.BlockSpec((c, GLA_WIDTH), lambda b, i: (fwd(b, i), 0)),
                   pl.BlockSpec((c, GLA_WIDTH), lambda b, i: (bwd(b, i), 0)), state_spec],
        out_shape=[jax.ShapeDtypeStruct((batch * t, GLA_WIDTH), F32)] * 2
        + [jax.ShapeDtypeStruct((batch, 2, GLA_HEADS, GLA_DK, GLA_DV), F32)],
        scratch_shapes=[pltpu.VMEM((2, GLA_HEADS, GLA_DK, GLA_DV), F32)],
        compiler_params=pltpu.CompilerParams(dimension_semantics=("arbitrary", "arbitrary")),
        name="gla",
    )(*ins)


GOUT_TM = 512


def _gla_out_kernel(of_ref, ob_ref, gate_ref, g_ref, o_ref, *, layer):
    g = g_ref[pl.ds(layer, 1), :]
    for h in range(GLA_HEADS):
        vs = slice(GLA_DV * h, GLA_DV * h + GLA_DV)
        o = of_ref[:, vs] + ob_ref[:, vs]
        y = (o * lax.rsqrt(jnp.mean(o * o, axis=-1, keepdims=True) + NORM_EPS)) * g
        gate = gate_ref[:, vs]
        o_ref[:, vs] = (y * (gate * jax.nn.sigmoid(gate))).astype(BF16)


def _gla_out(o_f, o_b, proj, g_gla_out, layer):
    m = o_f.shape[0]
    return pl.pallas_call(
        functools.partial(_gla_out_kernel, layer=layer),
        grid=(m // GOUT_TM,),
        in_specs=[pl.BlockSpec((GOUT_TM, GLA_WIDTH), lambda i: (i, 0)),
                  pl.BlockSpec((GOUT_TM, GLA_WIDTH), lambda i: (i, 0)),
                  pl.BlockSpec((GOUT_TM, GLA_WIDTH), lambda i: (i, COL_GOUT // GLA_WIDTH)),
                  pl.BlockSpec((DEPTH, GLA_DV), lambda i: (0, 0))],
        out_specs=pl.BlockSpec((GOUT_TM, GLA_WIDTH), lambda i: (i, 0)),
        out_shape=jax.ShapeDtypeStruct((m, GLA_WIDTH), BF16),
        compiler_params=pltpu.CompilerParams(dimension_semantics=("arbitrary",)),
        name="gla_out",
    )(o_f, o_b, proj, g_gla_out)


MERGE_TM = 1024
MERGE_TN = 512


def _merge_kernel(om_ref, og_ref, os_ref, gm_ref, gg_ref, gs_ref, wm_ref, wg_ref, ws_ref, o_ref):
    def branch(o_r, gate_r, w_r):
        return jax.nn.sigmoid(gate_r[...]) * jnp.dot(o_r[...], w_r[...], preferred_element_type=F32)

    merged = branch(om_ref, gm_ref, wm_ref) + branch(og_ref, gg_ref, wg_ref) + branch(os_ref, gs_ref, ws_ref)
    o_ref[...] = merged.astype(BF16)


def _merge(o_mla, o_gla, o_swa, proj, w_br_mla, w_br_gla, w_br_swa, layer):
    m = o_mla.shape[0]
    nn = D_MODEL // MERGE_TN
    o_spec = pl.BlockSpec((MERGE_TM, 1024), lambda i, j: (i, 0))
    gate_spec = lambda col: pl.BlockSpec((MERGE_TM, MERGE_TN), lambda i, j: (i, col // MERGE_TN + j))
    w_spec = pl.BlockSpec((None, 1024, MERGE_TN), lambda i, j: (layer, 0, j))
    est = 2 * (3 * MERGE_TM * 1024 * 2 + 3 * MERGE_TM * MERGE_TN * 4 + 3 * 1024 * MERGE_TN * 2 + MERGE_TM * MERGE_TN * 2)
    return pl.pallas_call(
        _merge_kernel,
        grid=(m // MERGE_TM, nn),
        in_specs=[o_spec, o_spec, o_spec, gate_spec(COL_GATE_MLA), gate_spec(COL_GATE_GLA), gate_spec(COL_GATE_SWA),
                  w_spec, w_spec, w_spec],
        out_specs=pl.BlockSpec((MERGE_TM, MERGE_TN), lambda i, j: (i, j)),
        out_shape=jax.ShapeDtypeStruct((m, D_MODEL), BF16),
        compiler_params=pltpu.CompilerParams(
            dimension_semantics=("arbitrary", "arbitrary"), vmem_limit_bytes=_vmem_limit(est)),
        name="merge",
    )(o_mla, o_gla, o_swa, proj, proj, proj, w_br_mla, w_br_gla, w_br_swa)


OUT_TM = 1024
OUT_TN = 512


def _out_proj_kernel(m_ref, w_ref, x_ref, gate_ref, o_ref, *, base_row, rows_per_cond):
    row = _cond_row(pl.program_id(0), m_ref.shape[0], base_row, rows_per_cond)
    y = jnp.dot(m_ref[...], w_ref[...], preferred_element_type=F32)
    o_ref[...] = x_ref[...] + gate_ref[pl.ds(row, 1), :] * y


def _out_proj(merged, x, mods, w_out, layer, base_row, rows_per_cond):
    m = x.shape[0]
    est = 2 * (OUT_TM * D_MODEL * 2 + D_MODEL * OUT_TN * 2 + 2 * OUT_TM * OUT_TN * 4)
    return pl.pallas_call(
        functools.partial(_out_proj_kernel, base_row=base_row, rows_per_cond=rows_per_cond),
        grid=(m // OUT_TM, D_MODEL // OUT_TN),
        in_specs=[pl.BlockSpec((OUT_TM, D_MODEL), lambda i, j: (i, 0)),
                  pl.BlockSpec((None, D_MODEL, OUT_TN), lambda i, j: (layer, 0, j)),
                  pl.BlockSpec((OUT_TM, OUT_TN), lambda i, j: (i, j)),
                  pl.BlockSpec((None, None, 8, OUT_TN), lambda i, j: (layer, 5, 0, j))],
        out_specs=pl.BlockSpec((OUT_TM, OUT_TN), lambda i, j: (i, j)),
        out_shape=jax.ShapeDtypeStruct((m, D_MODEL), F32),
        compiler_params=pltpu.CompilerParams(
            dimension_semantics=("arbitrary", "arbitrary"), vmem_limit_bytes=_vmem_limit(est)),
        name="out_proj",
    )(merged, w_out, x, mods)


def _trunk_layer(x, mods, layer, w, group):
    batch, t, base_row, rows_per_cond, rope, ctx = group
    x = _ffn(x, mods, layer, 0, w['g_norm1'], w['w_ff1_gu'], w['w_ff1_down'], base_row, rows_per_cond)
    proj = _in_proj(x, mods, layer, w['g_norm2'], w['w_in_packed'], base_row, rows_per_cond)
    mla_ctx = swa_ctx = s0 = None
    if ctx is not None:
        mla_ctx, swa_ctx, s0 = ctx[:2], ctx[2:4], ctx[4]
    o_mla, ckv_n = _mla_branch(proj, batch, t, layer, w['mla_packed'][layer], w['w_mla_ukv'], w['g_mla_q'], w['g_mla_kv'],
                               rope, mla_ctx)
    o_swa, k_n, v_raw = _swa_branch(proj, batch, t, layer, w['g_swa_qn'][layer], w['g_swa_kn'][layer], w['swa_sink'],
                                    rope, swa_ctx)
    o_f, o_b, s_fin = _gla(proj, batch, t, layer, w['w_gla_gf'][layer], w['b_gla_gf'][layer],
                           w['w_gla_gb'][layer], w['b_gla_gb'][layer], s0)
    o_gla = _gla_out(o_f, o_b, proj, w['g_gla_out'], layer)
    merged = _merge(o_mla, o_gla, o_swa, proj, w['w_br_mla'], w['w_br_gla'], w['w_br_swa'], layer)
    x = _out_proj(merged, x, mods, w['w_out'], layer, base_row, rows_per_cond)
    x = _ffn(x, mods, layer, 6, w['g_norm3'], w['w_ff2_gu'], w['w_ff2_down'], base_row, rows_per_cond)
    new_ctx = None
    if ctx is None:
        kpe = proj[:, COL_KPE:COL_KPE + MLA_ROPE].reshape(batch, t, MLA_ROPE)
        new_ctx = (ckv_n, kpe, k_n.reshape(batch, t, SWA_KV_HEADS, SWA_HEAD_DIM),
                   v_raw.reshape(batch, t, SWA_KV_HEADS, SWA_HEAD_DIM), s_fin)
    return x, new_ctx


def kernel(x_prompt, x_sample, cache_mla_ckv, cache_mla_kpe, cache_swa_k, cache_swa_v, state_gla,
           c, c_ctx, w_ada, b_ada, g_norm1, g_norm2, g_norm3,
           w_ff1_gu, w_ff1_down, w_ff2_gu, w_ff2_down, w_in,
           g_mla_q, w_mla_uq, g_mla_kv, w_mla_ukv, g_mla_qn, g_mla_kn,
           w_gla_gf, b_gla_gf, w_gla_gb, b_gla_gb, g_gla_out,
           g_swa_qn, g_swa_kn, swa_sink, w_br_mla, w_br_gla, w_br_swa, w_out):
    bp, tp, _ = x_prompt.shape
    bs, ts, _ = x_sample.shape
    assert bs + 1 <= 8, "conditioning rows are packed into one 8-row tile"
    cond8 = jnp.zeros((8, D_MODEL), F32).at[0].set(c_ctx).at[1:1 + bs].set(c)
    mods = _adaln(cond8, w_ada, b_ada)
    w = dict(g_norm1=g_norm1, g_norm2=g_norm2, g_norm3=g_norm3,
             w_ff1_gu=w_ff1_gu, w_ff1_down=w_ff1_down, w_ff2_gu=w_ff2_gu, w_ff2_down=w_ff2_down,
             w_in_packed=_pack_w_in(w_in),
             mla_packed=[_pack_mla(w_mla_uq[l], g_mla_qn[l], g_mla_kn[l]) for l in range(DEPTH)],
             w_mla_ukv=w_mla_ukv, g_mla_q=g_mla_q, g_mla_kv=g_mla_kv,
             g_swa_qn=g_swa_qn, g_swa_kn=g_swa_kn, swa_sink=swa_sink,
             w_gla_gf=w_gla_gf, b_gla_gf=b_gla_gf, w_gla_gb=w_gla_gb, b_gla_gb=b_gla_gb, g_gla_out=g_gla_out,
             w_br_mla=w_br_mla.astype(BF16), w_br_gla=w_br_gla.astype(BF16), w_br_swa=w_br_swa.astype(BF16),
             w_out=w_out.astype(BF16))
    past = cache_mla_ckv.shape[2]
    ctx = (cache_mla_ckv, jnp.pad(cache_mla_kpe, ((0, 0), (0, 0), (0, 0), (0, 128 - MLA_ROPE))),
           cache_swa_k.reshape(bs, DEPTH, past, SWA_KV_HEADS * SWA_HEAD_DIM),
           cache_swa_v.reshape(bs, DEPTH, past, SWA_KV_HEADS * SWA_HEAD_DIM), state_gla)
    group_p = (bp, tp, 0, bp * tp, None, None)
    group_s = (bs, ts, 1, ts, _rope_tables(ts), ctx)
    y_p = x_prompt.reshape(bp * tp, D_MODEL)
    y_s = x_sample.reshape(bs * ts, D_MODEL)
    new = []
    for l in range(DEPTH):
        y_p, new_ctx = _trunk_layer(y_p, mods, l, w, group_p)
        new.append(new_ctx)
        y_s, _ = _trunk_layer(y_s, mods, l, w, group_s)
    stacked = tuple(jnp.stack([new[l][k] for l in range(DEPTH)], axis=1) for k in range(5))
    return (y_p.reshape(bp, tp, D_MODEL), y_s.reshape(bs, ts, D_MODEL)) + stacked
```

```python
import functools

import numpy as np
import jax
import jax.numpy as jnp
from jax import lax
from jax.experimental import pallas as pl
from jax.experimental.pallas import tpu as pltpu

F32 = jnp.float32
BF16 = jnp.bfloat16

D_MODEL = 2048
DEPTH = 2
GRID_W = 64
ROPE_BASE = 10000.0
NORM_EPS = 1e-6
MLA_HEADS = 8
MLA_Q_LORA = 512
MLA_KV_LORA = 512
MLA_NOPE = 128
MLA_ROPE = 64
MLA_V = 128
MLA_QK = MLA_NOPE + MLA_ROPE
MLA_SCALE = MLA_QK ** -0.5
MLA_WIDTH = MLA_HEADS * MLA_V
MLA_HEAD_PAD = 256
GLA_HEADS = 4
GLA_DK = 128
GLA_DV = 256
GLA_GATE_RANK = 16
GLA_GATE_NORM = 16.0
GLA_WIDTH = GLA_HEADS * GLA_DV
GLA_CHUNK = 128
SWA_HEADS = 16
SWA_KV_HEADS = 4
SWA_HEAD_DIM = 64
SWA_WINDOW = 128
SWA_SCALE = SWA_HEAD_DIM ** -0.5
SWA_WIDTH = SWA_HEADS * SWA_HEAD_DIM
D_FF = 5632
N_MOD = 9
IN_SPLITS = (MLA_Q_LORA, MLA_KV_LORA, MLA_ROPE,
             GLA_HEADS * GLA_DK, GLA_HEADS * GLA_DK, GLA_WIDTH, GLA_WIDTH, GLA_GATE_RANK, GLA_GATE_RANK,
             SWA_WIDTH, SWA_KV_HEADS * SWA_HEAD_DIM, SWA_KV_HEADS * SWA_HEAD_DIM,
             D_MODEL, D_MODEL, D_MODEL)

V7X_LANES = 128
V7X_SUBLANES = 8
V7X_VMEM_BYTES = 64 * 1024 * 1024
MIB = 1024 * 1024

NT_DIMS = (((1,), (1,)), ((), ()))
LOG2_E = 1.4426950408889634

COL_GATE_MLA = 0
COL_GATE_GLA = 2048
COL_GATE_SWA = 4096
COL_CQ = 6144
COL_CKV = 6656
COL_GQ = 7168
COL_GK = 7680
COL_GV = 8192
COL_GOUT = 9216
COL_SQ = 10240
COL_SK = 11264
COL_SV = 11520
COL_KPE = 11776
COL_GG = 11904
PROJ_COLS = 12288
PROJ_TN = 1024


def _vmem_limit(nbytes):
    return int(min(nbytes + 12 * MIB, V7X_VMEM_BYTES - 6 * MIB))


def _rope_partner(x):
    lane = lax.broadcasted_iota(jnp.int32, x.shape, 1)
    return jnp.where((lane & 31) < 16, pltpu.roll(x, 112, axis=1), pltpu.roll(x, 16, axis=1))


ADA_TN = 1024


def _adaln_kernel(cond_ref, w_ref, b_ref, o_ref):
    c = cond_ref[...]
    s = (c * jax.nn.sigmoid(c)).astype(BF16)
    o_ref[...] = jnp.dot(s, w_ref[...].astype(BF16), preferred_element_type=F32) + b_ref[...]


def _adaln(cond8, w_ada, b_ada):
    halves = D_MODEL // ADA_TN
    nj = N_MOD * halves
    return pl.pallas_call(
        _adaln_kernel,
        grid=(DEPTH, nj),
        in_specs=[
            pl.BlockSpec((8, D_MODEL), lambda l, j: (0, 0)),
            pl.BlockSpec((None, D_MODEL, ADA_TN), lambda l, j: (l, 0, j)),
            pl.BlockSpec((None, 1, ADA_TN), lambda l, j: (l, 0, j)),
        ],
        out_specs=pl.BlockSpec((None, None, 8, ADA_TN), lambda l, j: (l, j // halves, 0, j % halves)),
        out_shape=jax.ShapeDtypeStruct((DEPTH, N_MOD, 8, D_MODEL), F32),
        compiler_params=pltpu.CompilerParams(
            dimension_semantics=("arbitrary", "arbitrary"),
            vmem_limit_bytes=_vmem_limit(2 * D_MODEL * ADA_TN * 4)),
        name="adaln",
    )(cond8, w_ada, b_ada.reshape(DEPTH, 1, N_MOD * D_MODEL))


NORM_ROWS = 128


def _modulated_norm_to(h_ref, x_ref, g, shift, scale):
    gain = g * (1.0 + scale)

    def body(c, carry):
        r0 = pl.multiple_of(c * NORM_ROWS, NORM_ROWS)
        x = x_ref[pl.ds(r0, NORM_ROWS), :]
        ms = jnp.mean(x * x, axis=-1, keepdims=True)
        h_ref[pl.ds(r0, NORM_ROWS), :] = ((x * lax.rsqrt(ms + NORM_EPS)) * gain + shift).astype(BF16)
        return carry

    lax.fori_loop(0, x_ref.shape[0] // NORM_ROWS, body, 0, unroll=2)


def _cond_row(i, tm, base_row, rows_per_cond):
    return base_row + (i * tm) // rows_per_cond


FFN_TM = 1024
FFN_TF = 256
FFN_TN = 512


def _ffn_kernel(x_ref, shift_ref, scale_ref, gate_ref, g_ref, wg_ref, wu_ref, wd_ref, o_ref, h_ref,
                *, layer, base_row, rows_per_cond):
    i = pl.program_id(0)
    j = pl.program_id(1)
    row = _cond_row(i, x_ref.shape[0], base_row, rows_per_cond)

    @pl.when(j == 0)
    def _():
        _modulated_norm_to(h_ref, x_ref, g_ref[pl.ds(layer, 1), :],
                           shift_ref[pl.ds(row, 1), :], scale_ref[pl.ds(row, 1), :])
        o_ref[...] = jnp.zeros_like(o_ref)

    h = h_ref[...]
    a = jnp.dot(h, wg_ref[...].astype(BF16), preferred_element_type=F32)
    u = jnp.dot(h, wu_ref[...].astype(BF16), preferred_element_type=F32)
    act = (a * jax.nn.sigmoid(a) * u).astype(BF16)
    for n in range(0, D_MODEL, FFN_TN):
        o_ref[:, n:n + FFN_TN] += jnp.dot(act, wd_ref[:, n:n + FFN_TN].astype(BF16), preferred_element_type=F32)

    @pl.when(j == pl.num_programs(1) - 1)
    def _():
        o_ref[...] = x_ref[...] + (0.5 * gate_ref[pl.ds(row, 1), :]) * o_ref[...]


def _ffn(x, mods, layer, first_mod, g_norm, w_gu, w_down, base_row, rows_per_cond):
    m = x.shape[0]
    nf = D_FF // FFN_TF
    mod_spec = lambda k: pl.BlockSpec((None, None, 8, D_MODEL), lambda i, j: (layer, first_mod + k, 0, 0))
    est = (2 * FFN_TM * D_MODEL * 4 * 2 + FFN_TM * D_MODEL * 2
           + 2 * 3 * D_MODEL * FFN_TF * w_gu.dtype.itemsize)
    return pl.pallas_call(
        functools.partial(_ffn_kernel, layer=layer, base_row=base_row, rows_per_cond=rows_per_cond),
        grid=(m // FFN_TM, nf),
        in_specs=[
            pl.BlockSpec((FFN_TM, D_MODEL), lambda i, j: (i, 0)),
            mod_spec(0), mod_spec(1), mod_spec(2),
            pl.BlockSpec((DEPTH, D_MODEL), lambda i, j: (0, 0)),
            pl.BlockSpec((None, D_MODEL, FFN_TF), lambda i, j: (layer, 0, j)),
            pl.BlockSpec((None, D_MODEL, FFN_TF), lambda i, j: (layer, 0, j + nf)),
            pl.BlockSpec((None, FFN_TF, D_MODEL), lambda i, j: (layer, j, 0)),
        ],
        out_specs=pl.BlockSpec((FFN_TM, D_MODEL), lambda i, j: (i, 0)),
        out_shape=jax.ShapeDtypeStruct((m, D_MODEL), F32),
        scratch_shapes=[pltpu.VMEM((FFN_TM, D_MODEL), BF16)],
        compiler_params=pltpu.CompilerParams(
            dimension_semantics=("arbitrary", "arbitrary"), vmem_limit_bytes=_vmem_limit(est)),
        name="ffn",
    )(x, mods, mods, mods, g_norm, w_gu, w_gu, w_down)


PROJ_TM = 1024


IN_COLS = sum(IN_SPLITS)
_SRC = dict(zip(('cq', 'ckv', 'kpe', 'gq', 'gk', 'gv', 'gout', 'ggf', 'ggb', 'sq', 'sk', 'sv', 'gate_mla'),
                np.concatenate([[0], np.cumsum(IN_SPLITS)]).tolist()))
PACK_ROWS = 512
PACK_TILES = PROJ_COLS // PACK_ROWS


def _pack_plan():
    small, a, b = [], [], []
    for dst, src, width in ((COL_GATE_MLA, _SRC['gate_mla'], 3 * D_MODEL), (COL_CQ, _SRC['cq'], 1024),
                            (COL_GQ, _SRC['gq'], 3072), (COL_SQ, _SRC['sq'], 1536)):
        assert dst == len(a) * PACK_ROWS and width % PACK_ROWS == 0
        for t in range(width // PACK_ROWS):
            small.append(0); a.append(src + t * PACK_ROWS); b.append(_SRC['ggf'])
    assert COL_KPE == len(a) * PACK_ROWS and COL_GG == COL_KPE + 128
    small.append(1); a.append(_SRC['kpe']); b.append(_SRC['ggf'])
    assert len(a) == PACK_TILES
    return np.asarray([small, a, b], np.int32)


def _pack_kernel(plan_ref, a3_ref, b3_ref, o_ref):
    is_small = plan_ref[0, pl.program_id(1)] == 1
    a_ref, b_ref = a3_ref.at[0], b3_ref.at[0]

    @pl.when(jnp.logical_not(is_small))
    def _():
        o_ref[...] = a_ref[...].astype(BF16)

    @pl.when(is_small)
    def _():
        o_ref[...] = jnp.zeros_like(o_ref)
        o_ref[0:MLA_ROPE, :] = a_ref[0:MLA_ROPE, :].astype(BF16)
        o_ref[128:128 + 2 * GLA_GATE_RANK, :] = b_ref[0:2 * GLA_GATE_RANK, :].astype(BF16)


def _pack_w_in(w_in):
    w_t = jnp.swapaxes(w_in, 1, 2)
    window = lambda which: pl.BlockSpec((pl.Element(1), pl.Element(PACK_ROWS), pl.Element(D_MODEL)),
                                        lambda l, j, plan: (l, pl.multiple_of(plan[which, j], 32), 0))
    return pl.pallas_call(
        _pack_kernel,
        grid_spec=pltpu.PrefetchScalarGridSpec(
            num_scalar_prefetch=1, grid=(DEPTH, PACK_TILES),
            in_specs=[window(1), window(2)],
            out_specs=pl.BlockSpec((None, PACK_ROWS, D_MODEL), lambda l, j, plan: (l, j, 0))),
        out_shape=jax.ShapeDtypeStruct((DEPTH, PROJ_COLS, D_MODEL), BF16),
        compiler_params=pltpu.CompilerParams(
            dimension_semantics=("arbitrary", "arbitrary"),
            vmem_limit_bytes=_vmem_limit(2 * PACK_ROWS * D_MODEL * (4 + 4 + 2))),
        name="pack_w_in",
    )(jnp.asarray(_pack_plan()), w_t, w_t)


def _proj_kernel(x_ref, shift_ref, scale_ref, g_ref, w_ref, o_ref, h_ref, *, layer, base_row, rows_per_cond):
    i = pl.program_id(0)
    j = pl.program_id(1)
    row = _cond_row(i, x_ref.shape[0], base_row, rows_per_cond)

    @pl.when(j == 0)
    def _():
        _modulated_norm_to(h_ref, x_ref, g_ref[pl.ds(layer, 1), :],
                           shift_ref[pl.ds(row, 1), :], scale_ref[pl.ds(row, 1), :])

    o_ref[...] = lax.dot_general(h_ref[...], w_ref[...], NT_DIMS, preferred_element_type=F32)


def _in_proj(x, mods, layer, g_norm2, w_packed, base_row, rows_per_cond):
    m = x.shape[0]
    ncols = w_packed.shape[1]
    mod_spec = lambda k: pl.BlockSpec((None, None, 8, D_MODEL), lambda i, j: (layer, 3 + k, 0, 0))
    est = 2 * PROJ_TM * D_MODEL * 4 + PROJ_TM * D_MODEL * 2 + 2 * D_MODEL * PROJ_TN * 2 + 2 * PROJ_TM * PROJ_TN * 4
    return pl.pallas_call(
        functools.partial(_proj_kernel, layer=layer, base_row=base_row, rows_per_cond=rows_per_cond),
        grid=(m // PROJ_TM, ncols // PROJ_TN),
        in_specs=[
            pl.BlockSpec((PROJ_TM, D_MODEL), lambda i, j: (i, 0)),
            mod_spec(0), mod_spec(1),
            pl.BlockSpec((DEPTH, D_MODEL), lambda i, j: (0, 0)),
            pl.BlockSpec((None, PROJ_TN, D_MODEL), lambda i, j: (layer, j, 0)),
        ],
        out_specs=pl.BlockSpec((PROJ_TM, PROJ_TN), lambda i, j: (i, j)),
        out_shape=jax.ShapeDtypeStruct((m, ncols), F32),
        scratch_shapes=[pltpu.VMEM((PROJ_TM, D_MODEL), BF16)],
        compiler_params=pltpu.CompilerParams(
            dimension_semantics=("arbitrary", "arbitrary"), vmem_limit_bytes=_vmem_limit(est)),
        name="in_proj",
    )(x, mods, mods, g_norm2, w_packed)


def _rope_tables(t):
    pos = np.arange(t)
    inv_freq = (ROPE_BASE ** (-np.arange(16, dtype=np.float32) / 16)).astype(np.float32)

    def cs(p):
        ang = p.astype(np.float32)[:, None] * inv_freq[None, :]
        return np.concatenate([np.cos(ang)] * 2, axis=1), np.concatenate([np.sin(ang)] * 2, axis=1)

    cr, sr = cs(pos // GRID_W)
    cc, sc = cs(pos % GRID_W)
    sign = np.where(np.arange(64) % 32 < 16, -1.0, 1.0).astype(np.float32)
    return (np.concatenate([cr, cc], axis=1).astype(np.float32),
            (np.concatenate([sr, sc], axis=1) * sign[None, :]).astype(np.float32))


MLA_TR = 256
MLA_TQ = 256


def _pack_mla(w_uq_l, g_qn_l, g_kn_l):
    w = w_uq_l.reshape(MLA_Q_LORA, MLA_HEADS, MLA_QK)
    z64 = jnp.zeros((MLA_Q_LORA, MLA_HEADS, 64), w.dtype)
    wq = jnp.concatenate([w, z64], axis=-1).reshape(MLA_Q_LORA, -1).astype(BF16)
    v64 = jnp.zeros((64,), F32)
    gq_full = jnp.concatenate([g_qn_l, v64])[None]
    gk_n = g_kn_l[:MLA_NOPE][None]
    gk_r = jnp.concatenate([g_kn_l[MLA_NOPE:], v64])[None]
    return wq, gq_full, gk_n, gk_r


def _mla_kv_kernel(*refs, layer, normalize, rope, emit_ckv, values_t):
    it = iter(refs)
    ckv_ref, kpe_ref = next(it), next(it)
    if rope:
        c_ref, s_ref = next(it), next(it)
    w_ref, gkv_ref, gn_ref, gr_ref = next(it), next(it), next(it), next(it)
    k_ref, v_ref = next(it), next(it)
    ckv = ckv_ref[...]
    if normalize:
        ckv = (ckv * lax.rsqrt(jnp.mean(ckv * ckv, axis=-1, keepdims=True) + NORM_EPS)) * gkv_ref[pl.ds(layer, 1), :]
    if emit_ckv:
        next(it)[...] = ckv
    kv = jnp.dot(ckv.astype(BF16), w_ref[...].astype(BF16), preferred_element_type=F32)
    kpe = kpe_ref[...]
    ss_pe = jnp.sum(kpe * kpe, axis=-1, keepdims=True)
    kr = kpe * gr_ref[...]
    if rope:
        kr = kr * c_ref[...] + _rope_partner(kr) * s_ref[...]
    for h in range(MLA_HEADS):
        kn = kv[:, 256 * h:256 * h + 128]
        r = lax.rsqrt((jnp.sum(kn * kn, axis=-1, keepdims=True) + ss_pe) * (1.0 / MLA_QK) + NORM_EPS)
        k_ref[:, 256 * h:256 * h + 128] = ((kn * r) * gn_ref[...]).astype(BF16)
        k_ref[:, 256 * h + 128:256 * h + 256] = (kr * r).astype(BF16)
        v = kv[:, 256 * h + 128:256 * h + 256]
        if values_t:
            v_ref[128 * h:128 * h + 128, :] = v.T.astype(BF16)
        else:
            v_ref[:, 128 * h:128 * h + 128] = v.astype(BF16)


def _mla_kv(ckv_src, ckv_spec, kpe_src, kpe_spec, rope_tabs, w_ukv, g_kv, gk_n, gk_r,
            batch, rows, layer, normalize, emit_ckv, values_t):
    rope = rope_tabs is not None
    nt = rows // MLA_TR
    ins = [ckv_src, kpe_src]
    specs = [ckv_spec, kpe_spec]
    if rope:
        ins += list(rope_tabs)
        specs += [pl.BlockSpec((MLA_TR, 128), lambda b, t: (t, 0))] * 2
    ins += [w_ukv, g_kv, gk_n, gk_r]
    specs += [pl.BlockSpec((None, MLA_KV_LORA, 2048), lambda b, t: (layer, 0, 0)),
              pl.BlockSpec((DEPTH, MLA_KV_LORA), lambda b, t: (0, 0))] + [pl.BlockSpec((1, 128), lambda b, t: (0, 0))] * 2
    out_shape = [jax.ShapeDtypeStruct((batch, rows, MLA_HEADS * MLA_HEAD_PAD), BF16),
                 jax.ShapeDtypeStruct((batch, MLA_WIDTH, rows) if values_t else (batch, rows, MLA_WIDTH), BF16)]
    out_specs = [pl.BlockSpec((None, MLA_TR, MLA_HEADS * MLA_HEAD_PAD), lambda b, t: (b, t, 0)),
                 pl.BlockSpec((None, MLA_WIDTH, MLA_TR), lambda b, t: (b, 0, t)) if values_t
                 else pl.BlockSpec((None, MLA_TR, MLA_WIDTH), lambda b, t: (b, t, 0))]
    if emit_ckv:
        out_shape.append(jax.ShapeDtypeStruct((batch, rows, MLA_KV_LORA), F32))
        out_specs.append(pl.BlockSpec((None, MLA_TR, MLA_KV_LORA), lambda b, t: (b, t, 0)))
    est = 2 * MLA_KV_LORA * 2048 * 4 + 4 * MLA_TR * 2048 * 4
    return pl.pallas_call(
        functools.partial(_mla_kv_kernel, layer=layer, normalize=normalize, rope=rope, emit_ckv=emit_ckv,
                          values_t=values_t),
        grid=(batch, nt), in_specs=specs, out_specs=out_specs, out_shape=out_shape,
        compiler_params=pltpu.CompilerParams(
            dimension_semantics=("arbitrary", "arbitrary"), vmem_limit_bytes=_vmem_limit(est)),
        name="mla_kv",
    )(*ins)


def _mla_attn_kernel(*refs, layer, rope, has_ctx, values_t):
    it = iter(refs)
    cq_ref, wq_ref, gq_ref, gfull_ref = next(it), next(it), next(it), next(it)
    if rope:
        c_ref, s_ref = next(it), next(it)
    k_ref, v_ref = next(it), next(it)
    if has_ctx:
        kc_ref, vc_ref = next(it), next(it)
    o_ref = next(it)
    cq = cq_ref[...]
    ql = ((cq * lax.rsqrt(jnp.mean(cq * cq, axis=-1, keepdims=True) + NORM_EPS)) * gq_ref[pl.ds(layer, 1), :]).astype(BF16)
    q_raw = jnp.dot(ql, wq_ref[...], preferred_element_type=F32)
    for h in range(MLA_HEADS):
        sl = slice(256 * h, 256 * h + 256)
        vs = slice(128 * h, 128 * h + 128)
        q = q_raw[:, sl]
        r = lax.rsqrt(jnp.sum(q * q, axis=-1, keepdims=True) * (1.0 / MLA_QK) + NORM_EPS)
        qh = q * gfull_ref[...]
        if rope:
            q_rope = qh[:, 128:]
            qh = jnp.concatenate([qh[:, :128], q_rope * c_ref[...] + _rope_partner(q_rope) * s_ref[...]], axis=1)
        qh = ((qh * r) * (MLA_SCALE * LOG2_E)).astype(BF16)
        s = lax.dot_general(qh, k_ref[:, sl], NT_DIMS, preferred_element_type=F32)
        m = jnp.max(s, axis=-1, keepdims=True)
        if has_ctx:
            sc = lax.dot_general(qh, kc_ref[:, sl], NT_DIMS, preferred_element_type=F32)
            m = jnp.maximum(m, jnp.max(sc, axis=-1, keepdims=True))
        e = jnp.exp2(s - m)
        den = jnp.sum(e, axis=-1, keepdims=True)

        def weighted_values(p, val_ref):
            if values_t:
                return lax.dot_general(val_ref[vs, :], p.astype(BF16), NT_DIMS, preferred_element_type=F32)
            return jnp.dot(p.astype(BF16), val_ref[:, vs], preferred_element_type=F32)

        o = weighted_values(e, v_ref)
        if has_ctx:
            ec = jnp.exp2(sc - m)
            den = den + jnp.sum(ec, axis=-1, keepdims=True)
            o = o + weighted_values(ec, vc_ref)
        if values_t:
            o = o.T
        o_ref[:, vs] = (o * (1.0 / den)).astype(BF16)


def _mla_attn(proj, wq, g_q, gq_full, rope_tabs, k, v, ctx_kv, batch, t, layer, values_t):
    rope = rope_tabs is not None
    has_ctx = ctx_kv is not None
    nq = t // MLA_TQ
    v_spec = lambda n: (pl.BlockSpec((None, MLA_WIDTH, n), lambda b, i: (b, 0, 0)) if values_t
                        else pl.BlockSpec((None, n, MLA_WIDTH), lambda b, i: (b, 0, 0)))
    const = lambda shape: pl.BlockSpec(shape, lambda b, i: (0,) * len(shape))
    ins = [proj, wq, g_q, gq_full]
    specs = [pl.BlockSpec((MLA_TQ, MLA_Q_LORA), lambda b, i: (b * nq + i, COL_CQ // MLA_Q_LORA)),
             const((MLA_Q_LORA, 2048)), const((DEPTH, MLA_Q_LORA)), const((1, 256))]
    if rope:
        ins += list(rope_tabs)
        specs += [pl.BlockSpec((MLA_TQ, 128), lambda b, i: (i, 0))] * 2
    ins += [k, v]
    specs += [pl.BlockSpec((None, t, 2048), lambda b, i: (b, 0, 0)), v_spec(t)]
    est = 2 * (t * 2048 * 2 + t * MLA_WIDTH * 2) + 6 * MLA_TQ * t * 4
    if has_ctx:
        kc, vc = ctx_kv
        lc = kc.shape[1]
        ins += [kc, vc]
        specs += [pl.BlockSpec((None, lc, 2048), lambda b, i: (b, 0, 0)), v_spec(lc)]
        est += 2 * lc * 3072 * 2
    return pl.pallas_call(
        functools.partial(_mla_attn_kernel, layer=layer, rope=rope, has_ctx=has_ctx, values_t=values_t),
        grid=(batch, nq), in_specs=specs,
        out_specs=pl.BlockSpec((MLA_TQ, MLA_WIDTH), lambda b, i: (b * nq + i, 0)),
        out_shape=jax.ShapeDtypeStruct((batch * t, MLA_WIDTH), BF16),
        compiler_params=pltpu.CompilerParams(
            dimension_semantics=("arbitrary", "arbitrary"), vmem_limit_bytes=_vmem_limit(est)),
        name="mla_attn",
    )(*ins)


def _mla_branch(proj, batch, t, layer, mla_w, w_ukv, g_q, g_kv, rope, ctx):
    wq, gq_full, gk_n, gk_r = mla_w
    nt = t // MLA_TR
    ckv_spec = pl.BlockSpec((MLA_TR, MLA_KV_LORA), lambda b, i: (b * nt + i, COL_CKV // MLA_KV_LORA))
    kpe_spec = pl.BlockSpec((MLA_TR, 128), lambda b, i: (b * nt + i, COL_KPE // 128))
    if rope is None:
        k, v, ckv_n = _mla_kv(proj, ckv_spec, proj, kpe_spec, None, w_ukv, g_kv, gk_n, gk_r,
                              batch, t, layer, normalize=True, emit_ckv=True, values_t=False)
        return _mla_attn(proj, wq, g_q, gq_full, None, k, v, None, batch, t, layer, values_t=False), ckv_n
    c64, s64 = rope
    z64 = np.zeros_like(c64)
    tabs = (jnp.asarray(np.concatenate([c64, z64], axis=1)),
            jnp.asarray(np.concatenate([s64, z64], axis=1)))
    k, v = _mla_kv(proj, ckv_spec, proj, kpe_spec, tabs, w_ukv, g_kv, gk_n, gk_r,
                   batch, t, layer, normalize=True, emit_ckv=False, values_t=True)
    ctx_ckv, ctx_kpe = ctx
    lc = ctx_ckv.shape[2]
    kc, vc = _mla_kv(ctx_ckv, pl.BlockSpec((None, None, MLA_TR, MLA_KV_LORA), lambda b, i: (b, layer, i, 0)),
                     ctx_kpe, pl.BlockSpec((None, None, MLA_TR, 128), lambda b, i: (b, layer, i, 0)),
                     None, w_ukv, g_kv, gk_n, gk_r, batch, lc, layer, normalize=False, emit_ckv=False, values_t=True)
    return _mla_attn(proj, wq, g_q, gq_full, tabs, k, v, (kc, vc), batch, t, layer, values_t=True), None


SWA_TR = 128
SWA_NEG = -1e30


def _pair_sum_matrix():
    g = (np.arange(128)[:, None] // 64 == np.arange(128)[None, :] // 64).astype(np.float32)
    return jnp.asarray(np.concatenate([g, g], axis=0), BF16)


def _group_rms_scale(x, pair_ref):
    sq = x * x
    hi = sq.astype(BF16)
    lo = (sq - hi.astype(F32)).astype(BF16)
    ss = jnp.dot(jnp.concatenate([hi, lo], axis=1), pair_ref[...], preferred_element_type=F32)
    return lax.rsqrt(ss * (1.0 / SWA_HEAD_DIM) + NORM_EPS)


def _swa_kv_kernel(*refs, normalize, rope, emit, pad_blocks, layer):
    it = iter(refs)
    k_ref, v_ref = next(it), next(it)
    if rope:
        c_ref, s_ref = next(it), next(it)
    if normalize:
        g_ref, pair_ref = next(it), next(it)
    ko_ref, vo_ref = next(it), next(it)
    if emit:
        kn_ref, vn_ref = next(it), next(it)
    lane = lax.broadcasted_iota(jnp.int32, (k_ref.shape[0], 128), 1)
    low = lane < 64

    def halves(x, c, o_ref):
        sw = pltpu.roll(x, 64, axis=1)
        zero = jnp.zeros_like(x)
        o_ref[4 * c + 0] = jnp.where(low, x, zero).astype(BF16)
        o_ref[4 * c + 1] = jnp.where(low, zero, sw).astype(BF16)
        o_ref[4 * c + 2] = jnp.where(low, sw, zero).astype(BF16)
        o_ref[4 * c + 3] = jnp.where(low, zero, x).astype(BF16)

    def compute():
        for c in range(2):
            sl = slice(128 * c, 128 * c + 128)
            kb = k_ref[:, sl]
            if normalize:
                r = _group_rms_scale(kb, pair_ref)
                kn = (kb * r) * g_ref[...]
                if emit:
                    kn_ref[:, sl] = kn
                if rope:
                    kn = kn * c_ref[...] + _rope_partner(kn) * s_ref[...]
            else:
                kn = kb
            halves(kn, c, ko_ref)
            vb = v_ref[:, sl]
            if emit:
                vn_ref[:, sl] = vb
            halves(vb, c, vo_ref)

    if pad_blocks:
        t = pl.program_id(1)
        is_pad = jnp.logical_or(t == 0, t == pl.num_programs(1) - 1)

        @pl.when(is_pad)
        def _():
            ko_ref[...] = jnp.zeros_like(ko_ref)
            vo_ref[...] = jnp.zeros_like(vo_ref)

        pl.when(jnp.logical_not(is_pad))(compute)
    else:
        compute()


def _swa_kv(k_src, k_spec, v_src, v_spec, rope_tabs, norm_ins, batch, rows, tr, layer, emit, pad_blocks):
    rope = rope_tabs is not None
    normalize = norm_ins is not None
    nt = rows // tr + (2 if pad_blocks else 0)
    ins, specs = [k_src, v_src], [k_spec, v_spec]
    if rope:
        tab_row = (lambda t: jnp.clip(t - 1, 0, nt - 3)) if pad_blocks else (lambda t: t)
        ins += list(rope_tabs)
        specs += [pl.BlockSpec((tr, 128), lambda b, t: (tab_row(t), 0))] * 2
    if normalize:
        ins += list(norm_ins)
        specs += [pl.BlockSpec((1, 128), lambda b, t: (0, 0)), pl.BlockSpec((256, 128), lambda b, t: (0, 0))]
    out_shape = [jax.ShapeDtypeStruct((batch, 8, nt * tr, 128), BF16)] * 2
    out_specs = [pl.BlockSpec((None, 8, tr, 128), lambda b, t: (b, 0, t, 0))] * 2
    if emit:
        out_shape += [jax.ShapeDtypeStruct((batch, rows, 256), F32)] * 2
        out_specs += [pl.BlockSpec((None, tr, 256), lambda b, t: (b, t, 0))] * 2
    return pl.pallas_call(
        functools.partial(_swa_kv_kernel, normalize=normalize, rope=rope, emit=emit, pad_blocks=pad_blocks, layer=layer),
        grid=(batch, nt), in_specs=specs, out_specs=out_specs, out_shape=out_shape,
        compiler_params=pltpu.CompilerParams(dimension_semantics=("arbitrary", "arbitrary")),
        name="swa_kv",
    )(*ins)


def _swa_attn_kernel(*refs, rope, windowed, has_ctx, t_total, layer):
    it = iter(refs)
    sink_ref, q_ref = next(it), next(it)
    if rope:
        c_ref, s_ref = next(it), next(it)
    g_ref, pair_ref = next(it), next(it)
    k_ref, v_ref = next(it), next(it)
    if has_ctx:
        kc_ref, vc_ref = next(it), next(it)
    o_ref = next(it)
    n = pl.program_id(1)
    if windowed:
        start = pl.multiple_of(n * SWA_TR, SWA_TR)
        win = pl.ds(start, 3 * SWA_TR)
        r_i = lax.broadcasted_iota(jnp.int32, (2 * SWA_TR, 3 * SWA_TR), 0) & (SWA_TR - 1)
        c_i = lax.broadcasted_iota(jnp.int32, (2 * SWA_TR, 3 * SWA_TR), 1)
        kpos = (n - 1) * SWA_TR + c_i
        diff = SWA_TR + r_i - c_i
        valid2 = (kpos >= 0) & (kpos < t_total) & (diff <= SWA_WINDOW) & (diff >= -SWA_WINDOW)
    else:
        win = slice(None)
    def normed_queries(cb):
        sl = slice(128 * cb, 128 * cb + 128)
        qb = q_ref[:, sl]
        r = _group_rms_scale(qb, pair_ref)
        qn = (qb * r) * g_ref[...]
        if rope:
            qn = qn * c_ref[...] + _rope_partner(qn) * s_ref[...]
        return (qn * (SWA_SCALE * LOG2_E)).astype(BF16)

    tq = q_ref.shape[0]
    low_lanes = lax.broadcasted_iota(jnp.int32, (2 * tq, 128), 1) < SWA_HEAD_DIM

    lk = 3 * SWA_TR if windowed else k_ref.shape[1]
    lc = kc_ref.shape[1] if has_ctx else 0
    row_blk = lax.broadcasted_iota(jnp.int32, (2 * tq, 1), 0) // tq
    for j in range(SWA_KV_HEADS):
        q2 = jnp.concatenate([normed_queries(2 * j), normed_queries(2 * j + 1)], axis=0)
        k_parts, v_parts = [], []
        for e in range(2):
            k_parts.append(k_ref[2 * j + e, win, :])
            v_parts.append(v_ref[2 * j + e, win, :])
            if has_ctx:
                k_parts.append(kc_ref[2 * j + e])
                v_parts.append(vc_ref[2 * j + e])
        s = lax.dot_general(q2, jnp.concatenate(k_parts, axis=0), NT_DIMS, preferred_element_type=F32)
        p_parts, inv = [], []
        for e in range(2):
            off = e * (lk + lc)
            sink = jnp.where(row_blk == 0, sink_ref[layer, 4 * j + e], sink_ref[layer, 4 * j + 2 + e]) * LOG2_E
            s_loc = s[:, off:off + lk]
            if windowed:
                s_loc = jnp.where(valid2, s_loc, SWA_NEG)
            m = jnp.maximum(jnp.max(s_loc, axis=-1, keepdims=True), sink)
            if has_ctx:
                s_ctx = s[:, off + lk:off + lk + lc]
                m = jnp.maximum(m, jnp.max(s_ctx, axis=-1, keepdims=True))
            p = jnp.exp2(s_loc - m)
            den = jnp.sum(p, axis=-1, keepdims=True) + jnp.exp2(sink - m)
            p_parts.append(p.astype(BF16))
            if has_ctx:
                pc = jnp.exp2(s_ctx - m)
                den = den + jnp.sum(pc, axis=-1, keepdims=True)
                p_parts.append(pc.astype(BF16))
            inv.append(1.0 / den)
        o = jnp.dot(jnp.concatenate(p_parts, axis=1), jnp.concatenate(v_parts, axis=0), preferred_element_type=F32)
        o = o * jnp.where(low_lanes, inv[0], inv[1])
        o_ref[:, 256 * j:256 * j + 128] = o[:tq].astype(BF16)
        o_ref[:, 256 * j + 128:256 * j + 256] = o[tq:].astype(BF16)


def _swa_attn(sink, proj, rope_tabs, norm_ins, k, v, ctx_kv, batch, t, layer):
    rope = rope_tabs is not None
    has_ctx = ctx_kv is not None
    tq = SWA_TR if has_ctx else 2 * SWA_TR
    nq = t // tq
    ins = [sink, proj]
    specs = [pl.BlockSpec(memory_space=pltpu.SMEM),
             pl.BlockSpec((tq, SWA_WIDTH), lambda b, i: (b * nq + i, COL_SQ // SWA_WIDTH))]
    if rope:
        ins += list(rope_tabs)
        specs += [pl.BlockSpec((tq, 128), lambda b, i: (i, 0))] * 2
    ins += list(norm_ins)
    specs += [pl.BlockSpec((1, 128), lambda b, i: (0, 0)), pl.BlockSpec((256, 128), lambda b, i: (0, 0))]
    lk = k.shape[2]
    ins += [k, v]
    specs += [pl.BlockSpec((None, 8, lk, 128), lambda b, i: (b, 0, 0, 0))] * 2
    if has_ctx:
        lc = ctx_kv[0].shape[2]
        ins += list(ctx_kv)
        specs += [pl.BlockSpec((None, 8, lc, 128), lambda b, i: (b, 0, 0, 0))] * 2
    return pl.pallas_call(
        functools.partial(_swa_attn_kernel, rope=rope, windowed=has_ctx, has_ctx=has_ctx, t_total=t, layer=layer),
        grid=(batch, nq), in_specs=specs,
        out_specs=pl.BlockSpec((tq, SWA_WIDTH), lambda b, i: (b * nq + i, 0)),
        out_shape=jax.ShapeDtypeStruct((batch * t, SWA_WIDTH), BF16),
        compiler_params=pltpu.CompilerParams(dimension_semantics=("arbitrary", "arbitrary")),
        name="swa_attn",
    )(*ins)


def _swa_branch(proj, batch, t, layer, g_qn_l, g_kn_l, sink, rope, ctx):
    pair = _pair_sum_matrix()
    tile2 = lambda g: jnp.concatenate([g, g])[None]
    wide = 2 * SWA_TR
    if rope is None:
        nw = t // wide
        k_spec = pl.BlockSpec((wide, 256), lambda b, i: (b * nw + i, COL_SK // 256))
        v_spec = pl.BlockSpec((wide, 256), lambda b, i: (b * nw + i, COL_SV // 256))
        k, v, k_n, v_raw = _swa_kv(proj, k_spec, proj, v_spec, None, (tile2(g_kn_l), pair),
                                   batch, t, wide, layer, emit=True, pad_blocks=False)
        o = _swa_attn(sink, proj, None, (tile2(g_qn_l), pair), k, v, None, batch, t, layer)
        return o, k_n, v_raw
    nt = t // SWA_TR
    c64, s64 = rope
    tabs = (jnp.asarray(np.concatenate([c64, c64], axis=1)),
            jnp.asarray(np.concatenate([s64, s64], axis=1)))
    row = lambda b, i: b * nt + jnp.clip(i - 1, 0, nt - 1)
    k_spec = pl.BlockSpec((SWA_TR, 256), lambda b, i: (row(b, i), COL_SK // 256))
    v_spec = pl.BlockSpec((SWA_TR, 256), lambda b, i: (row(b, i), COL_SV // 256))
    k, v = _swa_kv(proj, k_spec, proj, v_spec, tabs, (tile2(g_kn_l), pair), batch, t, SWA_TR, layer,
                   emit=False, pad_blocks=True)
    ctx_k, ctx_v = ctx
    lc = ctx_k.shape[2]
    c_spec = pl.BlockSpec((None, None, wide, 256), lambda b, i: (b, layer, i, 0))
    kc, vc = _swa_kv(ctx_k, c_spec, ctx_v, c_spec, None, None, batch, lc, wide, layer, emit=False, pad_blocks=False)
    o = _swa_attn(sink, proj, tabs, (tile2(g_qn_l), pair), k, v, (kc, vc), batch, t, layer)
    return o, None, None


GLA_LEVELS = (64, 32, 16, 8, 4, 2, 1)
GLA_MXU_LEVELS = (2, 1)
GLA_ROWSETS = 1 + len(GLA_MXU_LEVELS)


def _gla_sum_matrix(backward):
    c = GLA_CHUNK
    t = np.arange(c)[:, None]
    j = np.arange(c)[None, :]
    sets = [j <= t]
    for g in GLA_MXU_LEVELS:
        e = (t // (2 * g)) * 2 * g + g - 1
        upper = (t // g) % 2 == 1
        sets.append(np.where(upper, (j > e) & (j <= t), (j > t) & (j <= e)))
    n = np.concatenate(sets, axis=0).astype(np.float32)
    if backward:
        n = n.reshape(GLA_ROWSETS, c, c)[:, ::-1, ::-1].reshape(GLA_ROWSETS * c, c)
    return jnp.asarray(np.concatenate([n, n, n], axis=1), BF16)


def _gla_kernel(*refs, has_s0):
    it = iter(refs)
    srcs = [tuple(next(it) for _ in range(4)) for _ in range(2)]
    nmat = (next(it), next(it))
    wg = (next(it), next(it))
    bg = (next(it), next(it))
    s0_ref = next(it) if has_s0 else None
    o_refs = (next(it), next(it))
    sfin_ref = next(it)
    s_ref = next(it)
    i = pl.program_id(1)
    c = GLA_CHUNK

    @pl.when(i == 0)
    def _():
        s_ref[...] = s0_ref[...] if has_s0 else jnp.zeros_like(s_ref)

    row = lax.broadcasted_iota(jnp.int32, (c, c), 0)
    col = lax.broadcasted_iota(jnp.int32, (c, c), 1)
    diag = row == col
    log2 = lambda g: int(g).bit_length() - 1
    second_half = [((row >> log2(g)) & 1) == 1 for g in GLA_LEVELS]
    same_block = [(row >> (log2(g) + 1)) == (col >> (log2(g) + 1)) for g in GLA_LEVELS]
    for d in range(2):
        q_ref, k_ref, v_ref, gg_ref = srcs[d]
        z = jnp.dot(gg_ref[...].astype(BF16), wg[d][...], preferred_element_type=F32) + bg[d][...]
        la = (jnp.minimum(z, 0.0) - jnp.log1p(jnp.exp(-jnp.abs(z)))) * (1.0 / GLA_GATE_NORM)
        hi = la.astype(BF16)
        r1 = la - hi.astype(F32)
        mid = r1.astype(BF16)
        lo = (r1 - mid.astype(F32)).astype(BF16)
        ex_all = jnp.dot(nmat[d][...], jnp.concatenate([hi, mid, lo], axis=0), preferred_element_type=F32)
        for h in range(GLA_HEADS):
            sl = slice(GLA_DK * h, GLA_DK * h + GLA_DK)
            vs = slice(GLA_DV * h, GLA_DV * h + GLA_DV)
            ex = ex_all[:, sl]
            q = q_ref[:, sl] * (GLA_DK ** -0.5)
            k = k_ref[:, sl]
            v = v_ref[:, vs].astype(BF16)
            b = ex[0:c]
            eb = jnp.exp(b)
            b_exit = b[c - 1:c] if d == 0 else b[0:1]
            ek = jnp.exp(b_exit - b)
            s_old = s_ref[d, h]
            o = jnp.dot((q * eb).astype(BF16), s_old.astype(BF16), preferred_element_type=F32)
            a = jnp.where(diag, jnp.sum(q * k, axis=-1, keepdims=True), 0.0)
            for li, g in enumerate(GLA_LEVELS):
                if g in GLA_MXU_LEVELS:
                    r = 1 + GLA_MXU_LEVELS.index(g)
                    eg = jnp.exp(ex[r * c:(r + 1) * c])
                else:
                    b3 = b.reshape(c // (2 * g), 2 * g, GLA_DK)
                    edge = g - 1 if d == 0 else g
                    b_edge = jnp.broadcast_to(b3[:, edge:edge + 1, :], b3.shape).reshape(c, GLA_DK)
                    eg = jnp.exp(-jnp.abs(b - b_edge))
                qe, ke = q * eg, k * eg
                late = second_half[li]
                qg = (jnp.where(late, qe, 0.0) if d == 0 else jnp.where(late, 0.0, qe)).astype(BF16)
                kg = (jnp.where(late, 0.0, ke) if d == 0 else jnp.where(late, ke, 0.0)).astype(BF16)
                ag = lax.dot_general(qg, kg, NT_DIMS, preferred_element_type=F32)
                if 2 * g < c:
                    ag = jnp.where(same_block[li], ag, 0.0)
                a = a + ag
            o = o + jnp.dot(a.astype(BF16), v, preferred_element_type=F32)
            o_refs[d][:, vs] = o
            a_col = eb.T[:, c - 1:c] if d == 0 else eb.T[:, 0:1]
            kt = (k * ek).T.astype(BF16)
            s_ref[d, h] = a_col * s_old + jnp.dot(kt, v, preferred_element_type=F32)

    @pl.when(i == pl.num_programs(1) - 1)
    def _():
        sfin_ref[...] = s_ref[...]


def _gla(proj, batch, t, layer, w_gf, b_gf, w_gb, b_gb, s0):
    c = GLA_CHUNK
    n = t // c
    fwd = lambda b, i: b * n + i
    bwd = lambda b, i: b * n + (n - 1 - i)
    ins, specs = [], []
    for rowf in (fwd, bwd):
        ins += [proj] * 4
        specs += [pl.BlockSpec((c, 512), lambda b, i, rowf=rowf: (rowf(b, i), COL_GQ // 512)),
                  pl.BlockSpec((c, 512), lambda b, i, rowf=rowf: (rowf(b, i), COL_GK // 512)),
                  pl.BlockSpec((c, GLA_WIDTH), lambda b, i, rowf=rowf: (rowf(b, i), COL_GV // GLA_WIDTH)),
                  pl.BlockSpec((c, 128), lambda b, i, rowf=rowf: (rowf(b, i), COL_GG // 128))]
    const = lambda shape: pl.BlockSpec(shape, lambda b, i: (0,) * len(shape))
    pad_w = lambda w, off: jnp.zeros((128, 512), F32).at[off:off + GLA_GATE_RANK].set(w).astype(BF16)
    ins += [_gla_sum_matrix(False), _gla_sum_matrix(True), pad_w(w_gf, 0), pad_w(w_gb, GLA_GATE_RANK), b_gf[None], b_gb[None]]
    specs += [const((GLA_ROWSETS * c, 3 * c))] * 2 + [const((128, 512))] * 2 + [const((1, 512))] * 2
    state_spec = pl.BlockSpec((None, 2, GLA_HEADS, GLA_DK, GLA_DV), lambda b, i: (b, 0, 0, 0, 0))
    if s0 is not None:
        ins.append(s0)
        specs.append(pl.BlockSpec((None, None, 2, GLA_HEADS, GLA_DK, GLA_DV), lambda b, i: (b, layer, 0, 0, 0, 0)))
    return pl.pallas_call(
        functools.partial(_gla_kernel, has_s0=s0 is not None),
        grid=(batch, n), in_specs=specs,
        out_specs=[pl.BlockSpec((c, GLA_WIDTH), lambda b, i: (fwd(b, i), 0)),
                   pl.BlockSpec((c, GLA_WIDTH), lambda b, i: (bwd(b, i), 0)), state_spec],
        out_shape=[jax.ShapeDtypeStruct((batch * t, GLA_WIDTH), F32)] * 2
        + [jax.ShapeDtypeStruct((batch, 2, GLA_HEADS, GLA_DK, GLA_DV), F32)],
        scratch_shapes=[pltpu.VMEM((2, GLA_HEADS, GLA_DK, GLA_DV), F32)],
        compiler_params=pltpu.CompilerParams(dimension_semantics=("arbitrary", "arbitrary")),
        name="gla",
    )(*ins)


GOUT_TM = 512


def _gla_out_kernel(of_ref, ob_ref, gate_ref, g_ref, o_ref, *, layer):
    g = g_ref[pl.ds(layer, 1), :]
    for h in range(GLA_HEADS):
        vs = slice(GLA_DV * h, GLA_DV * h + GLA_DV)
        o = of_ref[:, vs] + ob_ref[:, vs]
        y = (o * lax.rsqrt(jnp.mean(o * o, axis=-1, keepdims=True) + NORM_EPS)) * g
        gate = gate_ref[:, vs]
        o_ref[:, vs] = (y * (gate * jax.nn.sigmoid(gate))).astype(BF16)


def _gla_out(o_f, o_b, proj, g_gla_out, layer):
    m = o_f.shape[0]
    return pl.pallas_call(
        functools.partial(_gla_out_kernel, layer=layer),
        grid=(m // GOUT_TM,),
        in_specs=[pl.BlockSpec((GOUT_TM, GLA_WIDTH), lambda i: (i, 0)),
                  pl.BlockSpec((GOUT_TM, GLA_WIDTH), lambda i: (i, 0)),
                  pl.BlockSpec((GOUT_TM, GLA_WIDTH), lambda i: (i, COL_GOUT // GLA_WIDTH)),
                  pl.BlockSpec((DEPTH, GLA_DV), lambda i: (0, 0))],
        out_specs=pl.BlockSpec((GOUT_TM, GLA_WIDTH), lambda i: (i, 0)),
        out_shape=jax.ShapeDtypeStruct((m, GLA_WIDTH), BF16),
        compiler_params=pltpu.CompilerParams(dimension_semantics=("arbitrary",)),
        name="gla_out",
    )(o_f, o_b, proj, g_gla_out)


MERGE_TM = 1024
MERGE_TN = 512


def _merge_kernel(om_ref, og_ref, os_ref, gm_ref, gg_ref, gs_ref, wm_ref, wg_ref, ws_ref, o_ref):
    def branch(o_r, gate_r, w_r):
        return jax.nn.sigmoid(gate_r[...]) * jnp.dot(o_r[...], w_r[...], preferred_element_type=F32)

    merged = branch(om_ref, gm_ref, wm_ref) + branch(og_ref, gg_ref, wg_ref) + branch(os_ref, gs_ref, ws_ref)
    o_ref[...] = merged.astype(BF16)


def _merge(o_mla, o_gla, o_swa, proj, w_br_mla, w_br_gla, w_br_swa, layer):
    m = o_mla.shape[0]
    nn = D_MODEL // MERGE_TN
    o_spec = pl.BlockSpec((MERGE_TM, 1024), lambda i, j: (i, 0))
    gate_spec = lambda col: pl.BlockSpec((MERGE_TM, MERGE_TN), lambda i, j: (i, col // MERGE_TN + j))
    w_spec = pl.BlockSpec((None, 1024, MERGE_TN), lambda i, j: (layer, 0, j))
    est = 2 * (3 * MERGE_TM * 1024 * 2 + 3 * MERGE_TM * MERGE_TN * 4 + 3 * 1024 * MERGE_TN * 2 + MERGE_TM * MERGE_TN * 2)
    return pl.pallas_call(
        _merge_kernel,
        grid=(m // MERGE_TM, nn),
        in_specs=[o_spec, o_spec, o_spec, gate_spec(COL_GATE_MLA), gate_spec(COL_GATE_GLA), gate_spec(COL_GATE_SWA),
                  w_spec, w_spec, w_spec],
        out_specs=pl.BlockSpec((MERGE_TM, MERGE_TN), lambda i, j: (i, j)),
        out_shape=jax.ShapeDtypeStruct((m, D_MODEL), BF16),
        compiler_params=pltpu.CompilerParams(
            dimension_semantics=("arbitrary", "arbitrary"), vmem_limit_bytes=_vmem_limit(est)),
        name="merge",
    )(o_mla, o_gla, o_swa, proj, proj, proj, w_br_mla, w_br_gla, w_br_swa)


OUT_TM = 1024
OUT_TN = 512


def _out_proj_kernel(m_ref, w_ref, x_ref, gate_ref, o_ref, *, base_row, rows_per_cond):
    row = _cond_row(pl.program_id(0), m_ref.shape[0], base_row, rows_per_cond)
    y = jnp.dot(m_ref[...], w_ref[...], preferred_element_type=F32)
    o_ref[...] = x_ref[...] + gate_ref[pl.ds(row, 1), :] * y


def _out_proj(merged, x, mods, w_out, layer, base_row, rows_per_cond):
    m = x.shape[0]
    est = 2 * (OUT_TM * D_MODEL * 2 + D_MODEL * OUT_TN * 2 + 2 * OUT_TM * OUT_TN * 4)
    return pl.pallas_call(
        functools.partial(_out_proj_kernel, base_row=base_row, rows_per_cond=rows_per_cond),
        grid=(m // OUT_TM, D_MODEL // OUT_TN),
        in_specs=[pl.BlockSpec((OUT_TM, D_MODEL), lambda i, j: (i, 0)),
                  pl.BlockSpec((None, D_MODEL, OUT_TN), lambda i, j: (layer, 0, j)),
                  pl.BlockSpec((OUT_TM, OUT_TN), lambda i, j: (i, j)),
                  pl.BlockSpec((None, None, 8, OUT_TN), lambda i, j: (layer, 5, 0, j))],
        out_specs=pl.BlockSpec((OUT_TM, OUT_TN), lambda i, j: (i, j)),
        out_shape=jax.ShapeDtypeStruct((m, D_MODEL), F32),
        compiler_params=pltpu.CompilerParams(
            dimension_semantics=("arbitrary", "arbitrary"), vmem_limit_bytes=_vmem_limit(est)),
        name="out_proj",
    )(merged, w_out, x, mods)


def _trunk_layer(x, mods, layer, w, group):
    batch, t, base_row, rows_per_cond, rope, ctx = group
    x = _ffn(x, mods, layer, 0, w['g_norm1'], w['w_ff1_gu'], w['w_ff1_down'], base_row, rows_per_cond)
    proj = _in_proj(x, mods, layer, w['g_norm2'], w['w_in_packed'], base_row, rows_per_cond)
    mla_ctx = swa_ctx = s0 = None
    if ctx is not None:
        mla_ctx, swa_ctx, s0 = ctx[:2], ctx[2:4], ctx[4]
    o_mla, ckv_n = _mla_branch(proj, batch, t, layer, w['mla_packed'][layer], w['w_mla_ukv'], w['g_mla_q'], w['g_mla_kv'],
                               rope, mla_ctx)
    o_swa, k_n, v_raw = _swa_branch(proj, batch, t, layer, w['g_swa_qn'][layer], w['g_swa_kn'][layer], w['swa_sink'],
                                    rope, swa_ctx)
    o_f, o_b, s_fin = _gla(proj, batch, t, layer, w['w_gla_gf'][layer], w['b_gla_gf'][layer],
                           w['w_gla_gb'][layer], w['b_gla_gb'][layer], s0)
    o_gla = _gla_out(o_f, o_b, proj, w['g_gla_out'], layer)
    merged = _merge(o_mla, o_gla, o_swa, proj, w['w_br_mla'], w['w_br_gla'], w['w_br_swa'], layer)
    x = _out_proj(merged, x, mods, w['w_out'], layer, base_row, rows_per_cond)
    x = _ffn(x, mods, layer, 6, w['g_norm3'], w['w_ff2_gu'], w['w_ff2_down'], base_row, rows_per_cond)
    new_ctx = None
    if ctx is None:
        kpe = proj[:, COL_KPE:COL_KPE + MLA_ROPE].reshape(batch, t, MLA_ROPE)
        new_ctx = (ckv_n, kpe, k_n.reshape(batch, t, SWA_KV_HEADS, SWA_HEAD_DIM),
                   v_raw.reshape(batch, t, SWA_KV_HEADS, SWA_HEAD_DIM), s_fin)
    return x, new_ctx


def kernel(x_prompt, x_sample, cache_mla_ckv, cache_mla_kpe, cache_swa_k, cache_swa_v, state_gla,
           c, c_ctx, w_ada, b_ada, g_norm1, g_norm2, g_norm3,
           w_ff1_gu, w_ff1_down, w_ff2_gu, w_ff2_down, w_in,
           g_mla_q, w_mla_uq, g_mla_kv, w_mla_ukv, g_mla_qn, g_mla_kn,
           w_gla_gf, b_gla_gf, w_gla_gb, b_gla_gb, g_gla_out,
           g_swa_qn, g_swa_kn, swa_sink, w_br_mla, w_br_gla, w_br_swa, w_out):
    bp, tp, _ = x_prompt.shape
    bs, ts, _ = x_sample.shape
    assert bs + 1 <= 8, "conditioning rows are packed into one 8-row tile"
    cond8 = jnp.zeros((8, D_MODEL), F32).at[0].set(c_ctx).at[1:1 + bs].set(c)
    mods = _adaln(cond8, w_ada, b_ada)
    w = dict(g_norm1=g_norm1, g_norm2=g_norm2, g_norm3=g_norm3,
             w_ff1_gu=w_ff1_gu, w_ff1_down=w_ff1_down, w_ff2_gu=w_ff2_gu, w_ff2_down=w_ff2_down,
             w_in_packed=_pack_w_in(w_in),
             mla_packed=[_pack_mla(w_mla_uq[l], g_mla_qn[l], g_mla_kn[l]) for l in range(DEPTH)],
             w_mla_ukv=w_mla_ukv, g_mla_q=g_mla_q, g_mla_kv=g_mla_kv,
             g_swa_qn=g_swa_qn, g_swa_kn=g_swa_kn, swa_sink=swa_sink,
             w_gla_gf=w_gla_gf, b_gla_gf=b_gla_gf, w_gla_gb=w_gla_gb, b_gla_gb=b_gla_gb, g_gla_out=g_gla_out,
             w_br_mla=w_br_mla.astype(BF16), w_br_gla=w_br_gla.astype(BF16), w_br_swa=w_br_swa.astype(BF16),
             w_out=w_out.astype(BF16))
    past = cache_mla_ckv.shape[2]
    ctx = (cache_mla_ckv, jnp.pad(cache_mla_kpe, ((0, 0), (0, 0), (0, 0), (0, 128 - MLA_ROPE))),
           cache_swa_k.reshape(bs, DEPTH, past, SWA_KV_HEADS * SWA_HEAD_DIM),
           cache_swa_v.reshape(bs, DEPTH, past, SWA_KV_HEADS * SWA_HEAD_DIM), state_gla)
    group_p = (bp, tp, 0, bp * tp, None, None)
    group_s = (bs, ts, 1, ts, _rope_tables(ts), ctx)
    y_p = x_prompt.reshape(bp * tp, D_MODEL)
    y_s = x_sample.reshape(bs * ts, D_MODEL)
    new = []
    for l in range(DEPTH):
        y_p, new_ctx = _trunk_layer(y_p, mods, l, w, group_p)
        new.append(new_ctx)
        y_s, _ = _trunk_layer(y_s, mods, l, w, group_s)
    stacked = tuple(jnp.stack([new[l][k] for l in range(DEPTH)], axis=1) for k in range(5))
    return (y_p.reshape(bp, tp, D_MODEL), y_s.reshape(bs, ts, D_MODEL)) + stacked
```
